```python
import math
import jax, jax.numpy as jnp
from jax import lax
import numpy as np

D_MODEL = 1024
BATCH = 8
SEQ = 2048
DEPTH = 4

GRID_W = 64
CTX_LEN = 256
EPS = 1e-6
N_MOD = 9
DN_HEADS = 4
DN_HEAD_DIM = 128
DN_WIDTH = DN_HEADS * DN_HEAD_DIM
CONV_WIDTH = 3
DN_CHUNK = 64
DIFF_HEADS = 4
DIFF_QK_DIM = 64
DIFF_V_DIM = 2 * DIFF_QK_DIM
DIFF_WIDTH = DIFF_HEADS * DIFF_V_DIM
Q_BLOCK = 128
ROPE_BASE = 10000.0
RET_HEADS = 8
RET_KEY_DIM = D_MODEL // RET_HEADS
RET_VALUE_DIM = 2 * RET_KEY_DIM
RET_V_WIDTH = RET_HEADS * RET_VALUE_DIM
RET_CHUNK = 64
D_FF = 2816
N_AB_LAYERS = (DEPTH + 1) // 2
N_RET_LAYERS = DEPTH // 2
AB_SIZES = [DN_WIDTH] * 4 + [2 * DN_HEADS] * 2 + [DIFF_WIDTH] * 3
AB_IN = sum(AB_SIZES)
RET_SIZES = [RET_HEADS * RET_KEY_DIM] * 2 + [RET_V_WIDTH] * 2
RET_IN = sum(RET_SIZES)

kernel_name = "hybrid_deltanet_diffattn_retention_macaron_dit"

F32 = jnp.float32


def rms_norm(x, w=None):
    xf = x.astype(F32)
    y = xf * lax.rsqrt(jnp.mean(xf * xf, axis=-1, keepdims=True) + EPS)
    if w is not None:
        y = y * w.astype(F32)
    return y.astype(x.dtype)


def l2norm(t):
    return t * lax.rsqrt(jnp.sum(t * t, axis=-1, keepdims=True) + EPS)


def split_cols(p, sizes):
    return jnp.split(p, [int(s) for s in np.cumsum(sizes)[:-1]], axis=-1)


def to_heads(t, n):
    B, L, _ = t.shape
    return t.reshape(B, L, n, -1).transpose(0, 2, 1, 3)


def ada_in(x, g, m, s):
    return rms_norm(x, g) * (1 + m[:, 3 * s + 1][:, None, :]) + m[:, 3 * s][:, None, :]


def ada_gate(m, s):
    return m[:, 3 * s + 2][:, None, :]


def swiglu(h, w_in, w_out):
    g, u = jnp.split(h @ w_in, 2, axis=-1)
    return (jax.nn.silu(g) * u) @ w_out


def rotate_half(x):
    x1, x2 = jnp.split(x, 2, axis=-1)
    return jnp.concatenate([-x2, x1], axis=-1)


def rope(x, ang):
    cos = jnp.concatenate([jnp.cos(ang)] * 2, axis=-1)
    sin = jnp.concatenate([jnp.sin(ang)] * 2, axis=-1)
    return x * cos + rotate_half(x) * sin


def axial_angles(L):
    rows = L // GRID_W
    r, cidx = jnp.meshgrid(jnp.arange(rows), jnp.arange(GRID_W), indexing='ij')
    axis_dim = DIFF_QK_DIM // 2
    inv = ROPE_BASE ** (-jnp.arange(0, axis_dim, 2, dtype=F32) / axis_dim)
    return (r.reshape(-1).astype(F32)[:, None] * inv, cidx.reshape(-1).astype(F32)[:, None] * inv)


def axial_rope(x, ang_r, ang_c):
    xr, xc = jnp.split(x, 2, axis=-1)
    return jnp.concatenate([rope(xr, ang_r), rope(xc, ang_c)], axis=-1)


def short_conv(u, w):
    k = w.shape[0]
    return lax.conv_general_dilated(u, w.astype(u.dtype)[:, None, :], window_strides=(1,),
                                    padding=[(k // 2, k // 2)],
                                    dimension_numbers=('NWC', 'WIO', 'NWC'),
                                    feature_group_count=u.shape[-1])


def two_pass(fn, ctx_seqs, lat_seqs, S0):
    outs_c, outs_l = [], []
    for d in range(2):
        f = (lambda t: jnp.flip(t, 2)) if d == 1 else (lambda t: t)
        oc, S = fn(d, [f(t) for t in ctx_seqs], S0)
        ol, _ = fn(d, [f(t) for t in lat_seqs], S)
        outs_c.append(f(oc))
        outs_l.append(f(ol))
    return outs_c[0] + outs_c[1], outs_l[0] + outs_l[1]


def gated_delta_chunked(q, k, v, log_a, beta, S0):
    B, H, L, _ = q.shape
    dv = v.shape[-1]
    C = DN_CHUNK
    n = L // C
    q, k, v = (t.reshape(B, H, n, C, t.shape[-1]) for t in (q, k, v))
    log_a, beta = (t.reshape(B, H, n, C) for t in (log_a, beta))
    g = jnp.cumsum(log_a, axis=-1)
    tril = jnp.tril(jnp.ones((C, C), bool))
    strict = jnp.tril(jnp.ones((C, C), bool), -1)
    gamma = jnp.exp(jnp.where(tril, g[..., :, None] - g[..., None, :], -jnp.inf))
    kb = k * beta[..., None]
    lower = jnp.where(strict, jnp.einsum('bhnid,bhnjd->bhnij', kb, k) * gamma, 0.0)
    tmat = lower + jnp.eye(C, dtype=lower.dtype)
    solve = lambda rhs: lax.linalg.triangular_solve(tmat, rhs, left_side=True, lower=True,
                                                    unit_diagonal=True)
    u = solve(v * beta[..., None])
    w = solve(kb * jnp.exp(g)[..., None])
    qk = jnp.einsum('bhnid,bhnjd->bhnij', q, k) * gamma

    def step(S, inp):
        q_i, k_i, u_i, w_i, g_i, qk_i = inp
        v_new = u_i - jnp.einsum('bhcd,bhde->bhce', w_i, S)
        o = (jnp.einsum('bhcd,bhde->bhce', q_i * jnp.exp(g_i)[..., None], S)
             + jnp.einsum('bhij,bhje->bhie', qk_i, v_new))
        g_last = g_i[..., -1:]
        S = (S * jnp.exp(g_last)[..., None]
             + jnp.einsum('bhcd,bhce->bhde', k_i * jnp.exp(g_last - g_i)[..., None], v_new))
        return S, o

    xs = tuple(jnp.moveaxis(t, 2, 0) for t in (q, k, u, w, g, qk))
    S, o = lax.scan(step, S0, xs)
    return jnp.moveaxis(o, 0, 2).reshape(B, H, L, dv), S


def retention_chunked(q, k, v, log_gamma, S0):
    B, H, L, _ = q.shape
    dv = v.shape[-1]
    C = RET_CHUNK
    n = L // C
    q, k, v = (t.reshape(B, H, n, C, t.shape[-1]) for t in (q, k, v))
    pos = jnp.arange(C, dtype=F32)
    lg = log_gamma[:, None]
    tril = jnp.tril(jnp.ones((C, C), bool))
    dec = jnp.exp(jnp.where(tril, (pos[:, None] - pos[None, :]) * lg[:, :, None], -jnp.inf))
    o_inner = jnp.einsum('bhnij,bhnje->bhnie',
                         jnp.einsum('bhnid,bhnjd->bhnij', q, k) * dec[:, None], v)
    q_dec = jnp.exp(lg * (pos + 1))[:, :, None]
    k_dec = jnp.exp(lg * (C - 1 - pos))[:, :, None]
    c_dec = jnp.exp(log_gamma * C)[:, None, None]

    def step(S, inp):
        q_i, k_i, v_i, o_i = inp
        o = o_i + jnp.einsum('bhcd,bhde->bhce', q_i, S) * q_dec
        S = S * c_dec + jnp.einsum('bhcd,bhce->bhde', k_i * k_dec, v_i)
        return S, o

    xs = tuple(jnp.moveaxis(t, 2, 0) for t in (q, k, v, o_inner))
    S, o = lax.scan(step, S0, xs)
    return jnp.moveaxis(o, 0, 2).reshape(B, H, L, dv), S


def diff_attend(q, k, v, lam_full):
    s = jnp.einsum('bhmqd,bhmkd->bhmqk', q, k) * DIFF_QK_DIM ** -0.5
    p = jax.nn.softmax(s, axis=-1)
    return jnp.einsum('bhqk,bhkd->bhqd', p[:, :, 0] - lam_full * p[:, :, 1], v)


def ab_mixer(hc, hl, w_in, conv_w, A_log, dt_bias, dn_norm_w, lam, subln_w, w_out,
             layer_idx, with_ctx_out):
    pc = split_cols(hc @ w_in, AB_SIZES)
    pl = split_cols(hl @ w_in, AB_SIZES)
    B, L, _ = hl.shape
    A_log = A_log.astype(F32)
    dt_bias = dt_bias.astype(F32)

    def dn_seqs(p):
        q, k, v, _, a, b = p[:6]
        qkv = jax.nn.silu(short_conv(jnp.concatenate([q, k, v], axis=-1), conv_w)).astype(F32)
        q, k, v = (to_heads(t, DN_HEADS) for t in jnp.split(qkv, 3, axis=-1))
        q = l2norm(q) * DN_HEAD_DIM ** -0.5
        k = l2norm(k)
        Bs, Ls, _ = a.shape
        a = a.astype(F32).reshape(Bs, Ls, 2, DN_HEADS)
        b = b.astype(F32).reshape(Bs, Ls, 2, DN_HEADS)
        log_a = (-jnp.exp(A_log) * jax.nn.softplus(a + dt_bias)).transpose(0, 3, 1, 2)
        beta = jax.nn.sigmoid(b).transpose(0, 3, 1, 2)
        return [q, k, v, log_a, beta]

    S0 = jnp.zeros((B, DN_HEADS, DN_HEAD_DIM, DN_HEAD_DIM), F32)
    dn_fn = lambda d, s, S: gated_delta_chunked(s[0], s[1], s[2], s[3][..., d], s[4][..., d], S)
    dn_c, dn_l = two_pass(dn_fn, dn_seqs(pc), dn_seqs(pl), S0)

    def dn_merge(o, z):
        Bs, H, Ls, dv = o.shape
        y = rms_norm(o.transpose(0, 2, 1, 3), dn_norm_w) * jax.nn.silu(z.astype(F32)).reshape(Bs, Ls, H, dv)
        return y.reshape(Bs, Ls, H * dv)

    lambda_init = 0.8 - 0.6 * math.exp(-0.3 * layer_idx)
    lamf = lam.astype(F32)
    lam_full = jnp.exp(jnp.sum(lamf[0] * lamf[1])) - jnp.exp(jnp.sum(lamf[2] * lamf[3])) + lambda_init

    def diff_qkv(p):
        q, k, v = p[6:]
        Bs, Ls, _ = q.shape
        q, k = (t.astype(F32).reshape(Bs, Ls, DIFF_HEADS, 2, DIFF_QK_DIM).transpose(0, 2, 3, 1, 4)
                for t in (q, k))
        return q, k, to_heads(v, DIFF_HEADS).astype(F32)

    def diff_merge(o):
        Bs, H, Ls, dv = o.shape
        return (rms_norm(o, subln_w) * (1 - lambda_init)).transpose(0, 2, 1, 3).reshape(Bs, Ls, H * dv)

    qc, kc, vc = diff_qkv(pc)
    ql, kl, vl = diff_qkv(pl)
    ang_r, ang_c = axial_angles(L)
    ql, kl = axial_rope(ql, ang_r, ang_c), axial_rope(kl, ang_r, ang_c)
    k_all = jnp.concatenate([kc, kl], axis=3)
    v_all = jnp.concatenate([vc, vl], axis=2)
    nb = L // Q_BLOCK
    qb = jnp.moveaxis(ql.reshape(B, DIFF_HEADS, 2, nb, Q_BLOCK, DIFF_QK_DIM), 3, 0)
    ol = lax.map(lambda qi: diff_attend(qi, k_all, v_all, lam_full), qb)
    ol = jnp.moveaxis(ol, 0, 2).reshape(B, DIFF_HEADS, L, DIFF_V_DIM)

    y_l = jnp.concatenate([dn_merge(dn_l, pl[3]), diff_merge(ol)], axis=-1).astype(hl.dtype) @ w_out
    if not with_ctx_out:
        return None, y_l
    oc = diff_attend(qc, kc, vc, lam_full)
    y_c = jnp.concatenate([dn_merge(dn_c, pc[3]), diff_merge(oc)], axis=-1).astype(hc.dtype) @ w_out
    return y_c, y_l


def ret_mixer(hc, hl, w_in, decay_logit, w_out, with_ctx_out):
    def stream(h, rotate):
        q, k, v, g = split_cols(h @ w_in, RET_SIZES)
        q, k, v = (to_heads(t, RET_HEADS).astype(F32) for t in (q, k, v))
        if rotate:
            inv = ROPE_BASE ** (-jnp.linspace(0.0, 1.0, RET_KEY_DIM // 2, dtype=F32))
            ang = jnp.arange(h.shape[1], dtype=F32)[:, None] * inv
            q, k = rope(q, ang), rope(k, ang)
        return [q, k * RET_KEY_DIM ** -0.5, v], g

    sc, gc = stream(hc, False)
    sl, gl = stream(hl, True)
    log_gamma = jax.nn.log_sigmoid(decay_logit.astype(F32))
    S0 = jnp.zeros((hl.shape[0], RET_HEADS, RET_KEY_DIM, RET_VALUE_DIM), F32)
    ret_fn = lambda d, s, S: retention_chunked(s[0], s[1], s[2], log_gamma[d], S)
    oc, ol = two_pass(ret_fn, sc, sl, S0)

    def merge(o, g, dtype):
        Bs, H, Ls, dv = o.shape
        y = rms_norm(o.transpose(0, 2, 1, 3)).reshape(Bs, Ls, H * dv) * jax.nn.silu(g.astype(F32))
        return y.astype(dtype) @ w_out

    y_l = merge(ol, gl, hl.dtype)
    if not with_ctx_out:
        return None, y_l
    return merge(oc, gc, hc.dtype), y_l


def setup_inputs(seed: int = 0) -> dict:
    key = jax.random.key(seed)
    ks = jax.random.split(key, 24)
    nrm = lambda k, shape, s: jax.random.normal(k, shape, F32) * s
    x = nrm(ks[0], (BATCH, SEQ, D_MODEL), 1.0)
    c = nrm(ks[1], (BATCH, D_MODEL), 1.0)
    ctx = nrm(ks[2], (BATCH, CTX_LEN, D_MODEL), 1.0)
    c_ctx = nrm(ks[3], (D_MODEL,), 1.0)
    ada_w = nrm(ks[4], (DEPTH, D_MODEL, N_MOD * D_MODEL), D_MODEL ** -0.5)
    ada_b = nrm(ks[5], (DEPTH, N_MOD * D_MODEL), 0.01)
    norm_w = 1.0 + nrm(ks[6], (DEPTH, 3, D_MODEL), 0.02)
    final_norm_w = 1.0 + nrm(ks[7], (D_MODEL,), 0.02)
    ffn_w_in = nrm(ks[8], (DEPTH, 2, D_MODEL, 2 * D_FF), D_MODEL ** -0.5)
    ffn_w_out = nrm(ks[9], (DEPTH, 2, D_FF, D_MODEL), D_FF ** -0.5)
    ab_w_in = nrm(ks[10], (N_AB_LAYERS, D_MODEL, AB_IN), D_MODEL ** -0.5)
    ab_conv_w = nrm(ks[11], (N_AB_LAYERS, CONV_WIDTH, 3 * DN_WIDTH), CONV_WIDTH ** -0.5)
    dn_A_log = jnp.log(jax.random.uniform(ks[12], (N_AB_LAYERS, 2, DN_HEADS), F32, 1.0, 16.0))
    dt = jnp.exp(jax.random.uniform(ks[13], (N_AB_LAYERS, 2, DN_HEADS), F32,
                                    math.log(1e-3), math.log(1e-1)))
    dn_dt_bias = dt + jnp.log(-jnp.expm1(-dt))
    dn_norm_w = 1.0 + nrm(ks[14], (N_AB_LAYERS, DN_HEAD_DIM), 0.02)
    diff_lambda = nrm(ks[15], (N_AB_LAYERS, 4, DIFF_QK_DIM), 0.1)
    diff_subln_w = 1.0 + nrm(ks[16], (N_AB_LAYERS, DIFF_V_DIM), 0.02)
    ab_w_out = nrm(ks[17], (N_AB_LAYERS, DN_WIDTH + DIFF_WIDTH, D_MODEL), (DN_WIDTH + DIFF_WIDTH) ** -0.5)
    ret_w_in = nrm(ks[18], (N_RET_LAYERS, D_MODEL, RET_IN), D_MODEL ** -0.5)
    base_logit = jnp.asarray(np.log(2.0 ** (5.0 + np.arange(RET_HEADS)) - 1.0).astype(np.float32))
    ret_decay_logit = base_logit + nrm(ks[19], (N_RET_LAYERS, 2, RET_HEADS), 0.1)
    ret_w_out = nrm(ks[20], (N_RET_LAYERS, RET_V_WIDTH, D_MODEL), RET_V_WIDTH ** -0.5)
    return {"x": x, "c": c, "ctx": ctx, "c_ctx": c_ctx, "ada_w": ada_w, "ada_b": ada_b,
            "norm_w": norm_w, "final_norm_w": final_norm_w, "ffn_w_in": ffn_w_in,
            "ffn_w_out": ffn_w_out, "ab_w_in": ab_w_in, "ab_conv_w": ab_conv_w,
            "dn_A_log": dn_A_log, "dn_dt_bias": dn_dt_bias, "dn_norm_w": dn_norm_w,
            "diff_lambda": diff_lambda, "diff_subln_w": diff_subln_w, "ab_w_out": ab_w_out,
            "ret_w_in": ret_w_in, "ret_decay_logit": ret_decay_logit, "ret_w_out": ret_w_out}


def reference(x, c, ctx, c_ctx, ada_w, ada_b, norm_w, final_norm_w, ffn_w_in, ffn_w_out,
              ab_w_in, ab_conv_w, dn_A_log, dn_dt_bias, dn_norm_w, diff_lambda, diff_subln_w,
              ab_w_out, ret_w_in, ret_decay_logit, ret_w_out):
    xc = ctx
    s_lat = jax.nn.silu(c)
    s_ctx = jax.nn.silu(c_ctx)[None]
    for l in range(DEPTH):
        last = l == DEPTH - 1
        m_lat = (s_lat @ ada_w[l] + ada_b[l]).reshape(-1, N_MOD, D_MODEL)
        m_ctx = (s_ctx @ ada_w[l] + ada_b[l]).reshape(1, N_MOD, D_MODEL)
        x = x + 0.5 * ada_gate(m_lat, 0) * swiglu(ada_in(x, norm_w[l, 0], m_lat, 0), ffn_w_in[l, 0], ffn_w_out[l, 0])
        xc = xc + 0.5 * ada_gate(m_ctx, 0) * swiglu(ada_in(xc, norm_w[l, 0], m_ctx, 0), ffn_w_in[l, 0], ffn_w_out[l, 0])
        hl = ada_in(x, norm_w[l, 1], m_lat, 1)
        hc = ada_in(xc, norm_w[l, 1], m_ctx, 1)
        i = l // 2
        if l % 2 == 0:
            yc, yl = ab_mixer(hc, hl, ab_w_in[i], ab_conv_w[i], dn_A_log[i], dn_dt_bias[i],
                              dn_norm_w[i], diff_lambda[i], diff_subln_w[i], ab_w_out[i],
                              l, not last)
        else:
            yc, yl = ret_mixer(hc, hl, ret_w_in[i], ret_decay_logit[i], ret_w_out[i], not last)
        x = x + ada_gate(m_lat, 1) * yl
        x = x + 0.5 * ada_gate(m_lat, 2) * swiglu(ada_in(x, norm_w[l, 2], m_lat, 2), ffn_w_in[l, 1], ffn_w_out[l, 1])
        if not last:
            xc = xc + ada_gate(m_ctx, 1) * yc
            xc = xc + 0.5 * ada_gate(m_ctx, 2) * swiglu(ada_in(xc, norm_w[l, 2], m_ctx, 2), ffn_w_in[l, 1], ffn_w_out[l, 1])
    return rms_norm(x, final_norm_w)
```

```python
import functools
import math

import jax
import jax.numpy as jnp
import numpy as np
from jax import lax
from jax.experimental import pallas as pl
from jax.experimental.pallas import tpu as pltpu

D_MODEL = 1024
BATCH = 8
SEQ = 2048
DEPTH = 4
GRID_W = 64
CTX_LEN = 256
EPS = 1e-6
N_MOD = 9
DN_HEADS = 4
DN_HEAD_DIM = 128
DN_WIDTH = DN_HEADS * DN_HEAD_DIM
CONV_WIDTH = 3
DN_CHUNK = 64
DIFF_HEADS = 4
DIFF_QK_DIM = 64
DIFF_V_DIM = 2 * DIFF_QK_DIM
DIFF_WIDTH = DIFF_HEADS * DIFF_V_DIM
Q_BLOCK = 128
ROPE_BASE = 10000.0
RET_HEADS = 8
RET_KEY_DIM = D_MODEL // RET_HEADS
RET_VALUE_DIM = 2 * RET_KEY_DIM
RET_V_WIDTH = RET_HEADS * RET_VALUE_DIM
RET_CHUNK = 64
D_FF = 2816
AB_SIZES = [DN_WIDTH] * 4 + [2 * DN_HEADS] * 2 + [DIFF_WIDTH] * 3
AB_IN = sum(AB_SIZES)
RET_SIZES = [RET_HEADS * RET_KEY_DIM] * 2 + [RET_V_WIDTH] * 2
RET_IN = sum(RET_SIZES)

F32 = jnp.float32
BF16 = jnp.bfloat16

LT = CTX_LEN + SEQ
TM = LT // 2
TILES_PER_BATCH = LT // TM
MOD_ROWS = 16
CTX_ROW = BATCH
V7X_VMEM_LIMIT = 56 * 1024 * 1024
FF_CHUNK = 256


def _cparams(sem):
    return pltpu.CompilerParams(dimension_semantics=sem, vmem_limit_bytes=V7X_VMEM_LIMIT)


def _silu(x):
    return x * (1.0 / (1.0 + jnp.exp(-x)))


def _dot(a, b):
    return jnp.dot(a, b, preferred_element_type=F32)


def _ada_kernel(s_ref, w_ref, b_ref, o_ref):
    s = _silu(s_ref[...]).astype(BF16)
    o_ref[0] = _dot(s, w_ref[0].astype(BF16)) + b_ref[0]


def _ada_mods(cvec, ada_w, ada_b):
    tn = 1024
    n = N_MOD * D_MODEL
    return pl.pallas_call(
        _ada_kernel,
        grid=(DEPTH, n // tn),
        in_specs=[pl.BlockSpec((MOD_ROWS, D_MODEL), lambda l, j: (0, 0)),
                  pl.BlockSpec((1, D_MODEL, tn), lambda l, j: (l, 0, j)),
                  pl.BlockSpec((1, 1, tn), lambda l, j: (l, 0, j))],
        out_specs=pl.BlockSpec((1, MOD_ROWS, tn), lambda l, j: (l, 0, j)),
        out_shape=jax.ShapeDtypeStruct((DEPTH, MOD_ROWS, n), F32),
        compiler_params=_cparams(("parallel", "parallel")),
        name="ada_mods",
    )(cvec, ada_w, ada_b.reshape(DEPTH, 1, n))


def _ada_in_tile(x, g_ref, mod_ref, h_ref):
    i = pl.program_id(0)
    b = i // TILES_PER_BATCH
    y = x * lax.rsqrt(jnp.mean(x * x, axis=-1, keepdims=True) + EPS) * g_ref[...]
    h_ref[...] = (y * (1.0 + mod_ref[1, pl.ds(b, 1), :]) + mod_ref[0, pl.ds(b, 1), :]).astype(BF16)

    @pl.when(i % TILES_PER_BATCH == 0)
    def _():
        yc = y[:CTX_LEN]
        h_ref[:CTX_LEN, :] = (yc * (1.0 + mod_ref[1, CTX_ROW:CTX_ROW + 1, :])
                              + mod_ref[0, CTX_ROW:CTX_ROW + 1, :]).astype(BF16)


def _gated_residual_store(x_ref, upd, mod_ref, o_ref, gate_scale, final_w_ref):
    i = pl.program_id(0)
    b = i // TILES_PER_BATCH

    def fin(v):
        if final_w_ref is None:
            return v
        return v * lax.rsqrt(jnp.mean(v * v, axis=-1, keepdims=True) + EPS) * final_w_ref[...]

    o_ref[...] = fin(x_ref[...] + (gate_scale * mod_ref[2, pl.ds(b, 1), :]) * upd)

    @pl.when(i % TILES_PER_BATCH == 0)
    def _():
        o_ref[:CTX_LEN, :] = fin(x_ref[:CTX_LEN, :]
                                 + (gate_scale * mod_ref[2, CTX_ROW:CTX_ROW + 1, :]) * upd[:CTX_LEN])


def _mod_spec(sub):
    return pl.BlockSpec((3, MOD_ROWS, D_MODEL), lambda *idx: (sub, 0, 0))


def _ffn_kernel(*refs, with_final):
    if with_final:
        x_ref, g_ref, mod_ref, win_ref, wout_ref, fw_ref, o_ref, h_ref, acc_ref = refs
    else:
        x_ref, g_ref, mod_ref, win_ref, wout_ref, o_ref, h_ref, acc_ref = refs
        fw_ref = None
    _ada_in_tile(x_ref[...], g_ref, mod_ref, h_ref)
    for f in range(D_FF // FF_CHUNK):
        h = h_ref[...]
        gg = _dot(h, win_ref[:, f * FF_CHUNK:(f + 1) * FF_CHUNK])
        uu = _dot(h, win_ref[:, D_FF + f * FF_CHUNK:D_FF + (f + 1) * FF_CHUNK])
        a = (_silu(gg) * uu).astype(BF16)
        part = _dot(a, wout_ref[f * FF_CHUNK:(f + 1) * FF_CHUNK, :])
        if f == 0:
            acc_ref[...] = part
        else:
            acc_ref[...] += part
    _gated_residual_store(x_ref, acc_ref[...], mod_ref, o_ref, 0.5, fw_ref)


def _ffn(x, g, mods, sub, w_in, w_out, final_w=None):
    t = x.shape[0]
    single = pl.Buffered(1)
    in_specs = [pl.BlockSpec((TM, D_MODEL), lambda i: (i, 0)),
                pl.BlockSpec((1, D_MODEL), lambda i: (0, 0)),
                _mod_spec(sub),
                pl.BlockSpec((D_MODEL, 2 * D_FF), lambda i: (0, 0), pipeline_mode=single),
                pl.BlockSpec((D_FF, D_MODEL), lambda i: (0, 0), pipeline_mode=single)]
    args = [x, g.reshape(1, D_MODEL), mods, w_in, w_out]
    if final_w is not None:
        in_specs.append(pl.BlockSpec((1, D_MODEL), lambda i: (0, 0)))
        args.append(final_w.reshape(1, D_MODEL))
    return pl.pallas_call(
        functools.partial(_ffn_kernel, with_final=final_w is not None),
        grid=(t // TM,),
        in_specs=in_specs,
        out_specs=pl.BlockSpec((TM, D_MODEL), lambda i: (i, 0)),
        out_shape=jax.ShapeDtypeStruct((t, D_MODEL), F32),
        scratch_shapes=[pltpu.VMEM((TM, D_MODEL), BF16), pltpu.VMEM((TM, D_MODEL), F32)],
        compiler_params=_cparams(("parallel",)),
        name="ffn",
    )(*args)


def _proj_kernel(x_ref, g_ref, mod_ref, w_ref, o_ref, h_ref):
    @pl.when(pl.program_id(1) == 0)
    def _():
        _ada_in_tile(x_ref[...], g_ref, mod_ref, h_ref)

    o_ref[...] = _dot(h_ref[...], w_ref[...])


def _proj(x, g, mods, w, tn):
    t = x.shape[0]
    n = w.shape[1]
    return pl.pallas_call(
        _proj_kernel,
        grid=(t // TM, n // tn),
        in_specs=[pl.BlockSpec((TM, D_MODEL), lambda i, j: (i, 0)),
                  pl.BlockSpec((1, D_MODEL), lambda i, j: (0, 0)),
                  _mod_spec(1),
                  pl.BlockSpec((D_MODEL, tn), lambda i, j: (0, j))],
        out_specs=pl.BlockSpec((TM, tn), lambda i, j: (i, j)),
        out_shape=jax.ShapeDtypeStruct((t, n), F32),
        scratch_shapes=[pltpu.VMEM((TM, D_MODEL), BF16)],
        compiler_params=_cparams(("parallel", "arbitrary")),
        name="mixer_in_proj",
    )(x, g.reshape(1, D_MODEL), mods, w)


def _out_proj_kernel(*refs, n_parts):
    x_ref, mod_ref = refs[0], refs[1]
    y_refs = refs[2:2 + n_parts]
    w_refs = refs[2 + n_parts:2 + 2 * n_parts]
    o_ref = refs[2 + 2 * n_parts]
    upd = None
    for y_ref, w_ref in zip(y_refs, w_refs):
        part = _dot(y_ref[...].astype(BF16), w_ref[...])
        upd = part if upd is None else upd + part
    _gated_residual_store(x_ref, upd, mod_ref, o_ref, 1.0, None)


def _out_proj(x, mods, ys, ws):
    t = x.shape[0]
    n_parts = len(ys)
    in_specs = [pl.BlockSpec((TM, D_MODEL), lambda i: (i, 0)), _mod_spec(1)]
    in_specs += [pl.BlockSpec((TM, y.shape[1]), lambda i: (i, 0)) for y in ys]
    in_specs += [pl.BlockSpec(w.shape, lambda i: (0, 0)) for w in ws]
    return pl.pallas_call(
        functools.partial(_out_proj_kernel, n_parts=n_parts),
        grid=(t // TM,),
        in_specs=in_specs,
        out_specs=pl.BlockSpec((TM, D_MODEL), lambda i: (i, 0)),
        out_shape=jax.ShapeDtypeStruct((t, D_MODEL), F32),
        compiler_params=_cparams(("parallel",)),
        name="mixer_out_proj",
    )(x, mods, *ys, *ws)


def _rms_norm(x, w=None):
    y = x * lax.rsqrt(jnp.mean(x * x, axis=-1, keepdims=True) + EPS)
    return y if w is None else y * w


def _l2norm(t):
    return t * lax.rsqrt(jnp.sum(t * t, axis=-1, keepdims=True) + EPS)


def _to_heads(t, n):
    B, L, _ = t.shape
    return t.reshape(B, L, n, -1).transpose(0, 2, 1, 3)


def _rotate_half(x):
    x1, x2 = jnp.split(x, 2, axis=-1)
    return jnp.concatenate([-x2, x1], axis=-1)


def _rope(x, ang):
    cos = jnp.concatenate([jnp.cos(ang)] * 2, axis=-1)
    sin = jnp.concatenate([jnp.sin(ang)] * 2, axis=-1)
    return x * cos + _rotate_half(x) * sin


def _axial_angles(L):
    rows = L // GRID_W
    r, cidx = jnp.meshgrid(jnp.arange(rows), jnp.arange(GRID_W), indexing='ij')
    axis_dim = DIFF_QK_DIM // 2
    inv = ROPE_BASE ** (-jnp.arange(0, axis_dim, 2, dtype=F32) / axis_dim)
    return (r.reshape(-1).astype(F32)[:, None] * inv, cidx.reshape(-1).astype(F32)[:, None] * inv)


def _axial_rope(x, ang_r, ang_c):
    xr, xc = jnp.split(x, 2, axis=-1)
    return jnp.concatenate([_rope(xr, ang_r), _rope(xc, ang_c)], axis=-1)


def _short_conv(u, w):
    k = w.shape[0]
    return lax.conv_general_dilated(u, w.astype(u.dtype)[:, None, :], window_strides=(1,),
                                    padding=[(k // 2, k // 2)],
                                    dimension_numbers=('NWC', 'WIO', 'NWC'),
                                    feature_group_count=u.shape[-1])


def _two_pass(fn, ctx_seqs, lat_seqs, S0):
    outs_c, outs_l = [], []
    for d in range(2):
        f = (lambda t: jnp.flip(t, 2)) if d == 1 else (lambda t: t)
        oc, S = fn(d, [f(t) for t in ctx_seqs], S0)
        ol, _ = fn(d, [f(t) for t in lat_seqs], S)
        outs_c.append(f(oc))
        outs_l.append(f(ol))
    return outs_c[0] + outs_c[1], outs_l[0] + outs_l[1]


def _gated_delta_chunked(q, k, v, log_a, beta, S0):
    B, H, L, _ = q.shape
    dv = v.shape[-1]
    C = DN_CHUNK
    n = L // C
    q, k, v = (t.reshape(B, H, n, C, t.shape[-1]) for t in (q, k, v))
    log_a, beta = (t.reshape(B, H, n, C) for t in (log_a, beta))
    g = jnp.cumsum(log_a, axis=-1)
    tril = jnp.tril(jnp.ones((C, C), bool))
    strict = jnp.tril(jnp.ones((C, C), bool), -1)
    gamma = jnp.exp(jnp.where(tril, g[..., :, None] - g[..., None, :], -jnp.inf))
    kb = k * beta[..., None]
    lower = jnp.where(strict, jnp.einsum('bhnid,bhnjd->bhnij', kb, k) * gamma, 0.0)
    tmat = lower + jnp.eye(C, dtype=lower.dtype)
    solve = lambda rhs: lax.linalg.triangular_solve(tmat, rhs, left_side=True, lower=True,
                                                    unit_diagonal=True)
    u = solve(v * beta[..., None])
    w = solve(kb * jnp.exp(g)[..., None])
    qk = jnp.einsum('bhnid,bhnjd->bhnij', q, k) * gamma

    def step(S, inp):
        q_i, k_i, u_i, w_i, g_i, qk_i = inp
        v_new = u_i - jnp.einsum('bhcd,bhde->bhce', w_i, S)
        o = (jnp.einsum('bhcd,bhde->bhce', q_i * jnp.exp(g_i)[..., None], S)
             + jnp.einsum('bhij,bhje->bhie', qk_i, v_new))
        g_last = g_i[..., -1:]
        S = (S * jnp.exp(g_last)[..., None]
             + jnp.einsum('bhcd,bhce->bhde', k_i * jnp.exp(g_last - g_i)[..., None], v_new))
        return S, o

    xs = tuple(jnp.moveaxis(t, 2, 0) for t in (q, k, u, w, g, qk))
    S, o = lax.scan(step, S0, xs)
    return jnp.moveaxis(o, 0, 2).reshape(B, H, L, dv), S


def _retention_chunked(q, k, v, log_gamma, S0):
    B, H, L, _ = q.shape
    dv = v.shape[-1]
    C = RET_CHUNK
    n = L // C
    q, k, v = (t.reshape(B, H, n, C, t.shape[-1]) for t in (q, k, v))
    pos = jnp.arange(C, dtype=F32)
    lg = log_gamma[:, None]
    tril = jnp.tril(jnp.ones((C, C), bool))
    dec = jnp.exp(jnp.where(tril, (pos[:, None] - pos[None, :]) * lg[:, :, None], -jnp.inf))
    o_inner = jnp.einsum('bhnij,bhnje->bhnie',
                         jnp.einsum('bhnid,bhnjd->bhnij', q, k) * dec[:, None], v)
    q_dec = jnp.exp(lg * (pos + 1))[:, :, None]
    k_dec = jnp.exp(lg * (C - 1 - pos))[:, :, None]
    c_dec = jnp.exp(log_gamma * C)[:, None, None]

    def step(S, inp):
        q_i, k_i, v_i, o_i = inp
        o = o_i + jnp.einsum('bhcd,bhde->bhce', q_i, S) * q_dec
        S = S * c_dec + jnp.einsum('bhcd,bhce->bhde', k_i * k_dec, v_i)
        return S, o

    xs = tuple(jnp.moveaxis(t, 2, 0) for t in (q, k, v, o_inner))
    S, o = lax.scan(step, S0, xs)
    return jnp.moveaxis(o, 0, 2).reshape(B, H, L, dv), S


def _diff_attend(q, k, v, lam_full):
    s = jnp.einsum('bhmqd,bhmkd->bhmqk', q, k) * DIFF_QK_DIM ** -0.5
    p = jax.nn.softmax(s, axis=-1)
    return jnp.einsum('bhqk,bhkd->bhqd', p[:, :, 0] - lam_full * p[:, :, 1], v)


def _ab_mixer_jax(pc, pl_, conv_w, A_log, dt_bias, dn_norm_w, lam, subln_w, layer_idx):
    B, L, _ = pl_[0].shape

    def dn_seqs(p):
        q, k, v, _, a, b = p[:6]
        qkv = jax.nn.silu(_short_conv(jnp.concatenate([q, k, v], axis=-1), conv_w))
        q, k, v = (_to_heads(t, DN_HEADS) for t in jnp.split(qkv, 3, axis=-1))
        q = _l2norm(q) * DN_HEAD_DIM ** -0.5
        k = _l2norm(k)
        Bs, Ls, _ = a.shape
        a = a.reshape(Bs, Ls, 2, DN_HEADS)
        b = b.reshape(Bs, Ls, 2, DN_HEADS)
        log_a = (-jnp.exp(A_log) * jax.nn.softplus(a + dt_bias)).transpose(0, 3, 1, 2)
        beta = jax.nn.sigmoid(b).transpose(0, 3, 1, 2)
        return [q, k, v, log_a, beta]

    S0 = jnp.zeros((B, DN_HEADS, DN_HEAD_DIM, DN_HEAD_DIM), F32)
    dn_fn = lambda d, s, S: _gated_delta_chunked(s[0], s[1], s[2], s[3][..., d], s[4][..., d], S)
    dn_c, dn_l = _two_pass(dn_fn, dn_seqs(pc), dn_seqs(pl_), S0)

    def dn_merge(o, z):
        Bs, H, Ls, dv = o.shape
        y = _rms_norm(o.transpose(0, 2, 1, 3), dn_norm_w) * jax.nn.silu(z).reshape(Bs, Ls, H, dv)
        return y.reshape(Bs, Ls, H * dv)

    lambda_init = 0.8 - 0.6 * math.exp(-0.3 * layer_idx)
    lam_full = jnp.exp(jnp.sum(lam[0] * lam[1])) - jnp.exp(jnp.sum(lam[2] * lam[3])) + lambda_init

    def diff_qkv(p):
        q, k, v = p[6:]
        Bs, Ls, _ = q.shape
        q, k = (t.reshape(Bs, Ls, DIFF_HEADS, 2, DIFF_QK_DIM).transpose(0, 2, 3, 1, 4) for t in (q, k))
        return q, k, _to_heads(v, DIFF_HEADS)

    def diff_merge(o):
        Bs, H, Ls, dv = o.shape
        return (_rms_norm(o, subln_w) * (1 - lambda_init)).transpose(0, 2, 1, 3).reshape(Bs, Ls, H * dv)

    qc, kc, vc = diff_qkv(pc)
    ql, kl, vl = diff_qkv(pl_)
    ang_r, ang_c = _axial_angles(L)
    ql, kl = _axial_rope(ql, ang_r, ang_c), _axial_rope(kl, ang_r, ang_c)
    k_all = jnp.concatenate([kc, kl], axis=3)
    v_all = jnp.concatenate([vc, vl], axis=2)
    nb = L // Q_BLOCK
    qb = jnp.moveaxis(ql.reshape(B, DIFF_HEADS, 2, nb, Q_BLOCK, DIFF_QK_DIM), 3, 0)
    ol = lax.map(lambda qi: _diff_attend(qi, k_all, v_all, lam_full), qb)
    ol = jnp.moveaxis(ol, 0, 2).reshape(B, DIFF_HEADS, L, DIFF_V_DIM)
    oc = _diff_attend(qc, kc, vc, lam_full)
    y_l = (dn_merge(dn_l, pl_[3]), diff_merge(ol))
    y_c = (dn_merge(dn_c, pc[3]), diff_merge(oc))
    return y_c, y_l


def _ret_mixer_jax(pc, pl_, decay_logit):
    def stream(p, rotate):
        q, k, v, g = p
        q, k, v = (_to_heads(t, RET_HEADS) for t in (q, k, v))
        if rotate:
            inv = ROPE_BASE ** (-jnp.linspace(0.0, 1.0, RET_KEY_DIM // 2, dtype=F32))
            ang = jnp.arange(q.shape[2], dtype=F32)[:, None] * inv
            q, k = _rope(q, ang), _rope(k, ang)
        return [q, k * RET_KEY_DIM ** -0.5, v], g

    sc, gc = stream(pc, False)
    sl, gl = stream(pl_, True)
    log_gamma = jax.nn.log_sigmoid(decay_logit)
    S0 = jnp.zeros((pl_[0].shape[0], RET_HEADS, RET_KEY_DIM, RET_VALUE_DIM), F32)
    ret_fn = lambda d, s, S: _retention_chunked(s[0], s[1], s[2], log_gamma[d], S)
    oc, ol = _two_pass(ret_fn, sc, sl, S0)

    def merge(o, g):
        Bs, H, Ls, dv = o.shape
        return _rms_norm(o.transpose(0, 2, 1, 3)).reshape(Bs, Ls, H * dv) * jax.nn.silu(g)

    return merge(oc, gc), merge(ol, gl)


AB_MAIN_COLS = 4 * DN_WIDTH + 3 * DIFF_WIDTH
AB_GATE_COLS = 256
AB_PROJ_TN = (AB_MAIN_COLS + AB_GATE_COLS) // 3
RET_PROJ_TN = RET_IN // 3


def _split_ab_weight(w):
    o_ab = 4 * DN_WIDTH
    n_ab = 4 * DN_HEADS
    main = jnp.concatenate([w[:, :o_ab], w[:, o_ab + n_ab:]], axis=1)
    gates = jnp.pad(w[:, o_ab:o_ab + n_ab], ((0, 0), (0, AB_GATE_COLS - n_ab)))
    return jnp.concatenate([main, gates], axis=1)


def kernel(x, c, ctx, c_ctx, ada_w, ada_b, norm_w, final_norm_w, ffn_w_in, ffn_w_out, ab_w_in,
           ab_conv_w, dn_A_log, dn_dt_bias, dn_norm_w, diff_lambda, diff_subln_w, ab_w_out,
           ret_w_in, ret_decay_logit, ret_w_out):
    B = x.shape[0]
    xs = jnp.concatenate([ctx, x], axis=1).reshape(B * LT, D_MODEL)
    cvec = jnp.concatenate([c, c_ctx[None], jnp.zeros((MOD_ROWS - B - 1, D_MODEL), F32)], axis=0)
    mods_all = _ada_mods(cvec, ada_w, ada_b)
    mods_all = mods_all.reshape(DEPTH, MOD_ROWS, N_MOD, D_MODEL).transpose(0, 2, 1, 3)
    ffn_w_in_b = ffn_w_in.astype(BF16)
    ffn_w_out_b = ffn_w_out.astype(BF16)
    for l in range(DEPTH):
        last = l == DEPTH - 1
        mods = mods_all[l]
        xs = _ffn(xs, norm_w[l, 0], mods, 0, ffn_w_in_b[l, 0], ffn_w_out_b[l, 0])
        i = l // 2
        if l % 2 == 0:
            w = _split_ab_weight(ab_w_in[i]).astype(BF16)
            p = _proj(xs, norm_w[l, 1], mods, w, AB_PROJ_TN).reshape(B, LT, -1)
            cols = np.cumsum([0] + [DN_WIDTH] * 4 + [DIFF_WIDTH] * 3)
            parts = [p[:, :, cols[j]:cols[j + 1]] for j in range(7)]
            a = p[:, :, AB_MAIN_COLS:AB_MAIN_COLS + 2 * DN_HEADS]
            bb = p[:, :, AB_MAIN_COLS + 2 * DN_HEADS:AB_MAIN_COLS + 4 * DN_HEADS]
            plist = parts[:4] + [a, bb] + parts[4:]
            pc = [t[:, :CTX_LEN] for t in plist]
            pl_ = [t[:, CTX_LEN:] for t in plist]
            y_c, y_l = _ab_mixer_jax(pc, pl_, ab_conv_w[i], dn_A_log[i], dn_dt_bias[i], dn_norm_w[i],
                                     diff_lambda[i], diff_subln_w[i], l)
            ys = [jnp.concatenate([yc_, yl_], axis=1).reshape(B * LT, -1) for yc_, yl_ in zip(y_c, y_l)]
            wo = ab_w_out[i].astype(BF16)
            xs = _out_proj(xs, mods, ys, [wo[:DN_WIDTH], wo[DN_WIDTH:]])
        else:
            p = _proj(xs, norm_w[l, 1], mods, ret_w_in[i].astype(BF16), RET_PROJ_TN).reshape(B, LT, -1)
            cols = np.cumsum([0] + RET_SIZES)
            plist = [p[:, :, cols[j]:cols[j + 1]] for j in range(4)]
            pc = [t[:, :CTX_LEN] for t in plist]
            pl_ = [t[:, CTX_LEN:] for t in plist]
            y_c, y_l = _ret_mixer_jax(pc, pl_, ret_decay_logit[i])
            ys = [jnp.concatenate([y_c, y_l], axis=1).reshape(B * LT, -1)]
            xs = _out_proj(xs, mods, ys, [ret_w_out[i].astype(BF16)])
        xs = _ffn(xs, norm_w[l, 2], mods, 2, ffn_w_in_b[l, 1], ffn_w_out_b[l, 1],
                  final_w=final_norm_w if last else None)
    return xs.reshape(B, LT, D_MODEL)[:, CTX_LEN:]
```

```python
import functools
import math

import jax
import jax.numpy as jnp
import numpy as np
from jax import lax
from jax.experimental import pallas as pl
from jax.experimental.pallas import tpu as pltpu

D_MODEL = 1024
BATCH = 8
SEQ = 2048
DEPTH = 4
GRID_W = 64
CTX_LEN = 256
EPS = 1e-6
N_MOD = 9
DN_HEADS = 4
DN_HEAD_DIM = 128
DN_WIDTH = DN_HEADS * DN_HEAD_DIM
CONV_WIDTH = 3
DN_CHUNK = 64
DIFF_HEADS = 4
DIFF_QK_DIM = 64
DIFF_V_DIM = 2 * DIFF_QK_DIM
DIFF_WIDTH = DIFF_HEADS * DIFF_V_DIM
Q_BLOCK = 128
ROPE_BASE = 10000.0
RET_HEADS = 8
RET_KEY_DIM = D_MODEL // RET_HEADS
RET_VALUE_DIM = 2 * RET_KEY_DIM
RET_V_WIDTH = RET_HEADS * RET_VALUE_DIM
RET_CHUNK = 64
D_FF = 2816
AB_SIZES = [DN_WIDTH] * 4 + [2 * DN_HEADS] * 2 + [DIFF_WIDTH] * 3
AB_IN = sum(AB_SIZES)
RET_SIZES = [RET_HEADS * RET_KEY_DIM] * 2 + [RET_V_WIDTH] * 2
RET_IN = sum(RET_SIZES)

F32 = jnp.float32
BF16 = jnp.bfloat16

LT = CTX_LEN + SEQ
TM = LT // 2
TILES_PER_BATCH = LT // TM
MOD_ROWS = 16
CTX_ROW = BATCH
V7X_VMEM_LIMIT = 56 * 1024 * 1024
FF_CHUNK = 256


def _cparams(sem):
    return pltpu.CompilerParams(dimension_semantics=sem, vmem_limit_bytes=V7X_VMEM_LIMIT)


def _silu(x):
    return x * (1.0 / (1.0 + jnp.exp(-x)))


def _dot(a, b):
    return jnp.dot(a, b, preferred_element_type=F32)


def _ada_kernel(s_ref, w_ref, b_ref, o_ref):
    s = _silu(s_ref[...]).astype(BF16)
    o_ref[0] = _dot(s, w_ref[0].astype(BF16)) + b_ref[0]


def _ada_mods(cvec, ada_w, ada_b):
    tn = 1024
    n = N_MOD * D_MODEL
    return pl.pallas_call(
        _ada_kernel,
        grid=(DEPTH, n // tn),
        in_specs=[pl.BlockSpec((MOD_ROWS, D_MODEL), lambda l, j: (0, 0)),
                  pl.BlockSpec((1, D_MODEL, tn), lambda l, j: (l, 0, j)),
                  pl.BlockSpec((1, 1, tn), lambda l, j: (l, 0, j))],
        out_specs=pl.BlockSpec((1, MOD_ROWS, tn), lambda l, j: (l, 0, j)),
        out_shape=jax.ShapeDtypeStruct((DEPTH, MOD_ROWS, n), F32),
        compiler_params=_cparams(("parallel", "parallel")),
        name="ada_mods",
    )(cvec, ada_w, ada_b.reshape(DEPTH, 1, n))


def _ada_in_tile(x, g_ref, mod_ref, h_ref):
    i = pl.program_id(0)
    b = i // TILES_PER_BATCH
    y = x * lax.rsqrt(jnp.mean(x * x, axis=-1, keepdims=True) + EPS) * g_ref[...]
    h_ref[...] = (y * (1.0 + mod_ref[1, pl.ds(b, 1), :]) + mod_ref[0, pl.ds(b, 1), :]).astype(BF16)

    @pl.when(i % TILES_PER_BATCH == 0)
    def _():
        yc = y[:CTX_LEN]
        h_ref[:CTX_LEN, :] = (yc * (1.0 + mod_ref[1, CTX_ROW:CTX_ROW + 1, :])
                              + mod_ref[0, CTX_ROW:CTX_ROW + 1, :]).astype(BF16)


def _gated_residual_store(x_ref, upd, mod_ref, o_ref, gate_scale, final_w_ref):
    i = pl.program_id(0)
    b = i // TILES_PER_BATCH

    def fin(v):
        if final_w_ref is None:
            return v
        return v * lax.rsqrt(jnp.mean(v * v, axis=-1, keepdims=True) + EPS) * final_w_ref[...]

    o_ref[...] = fin(x_ref[...] + (gate_scale * mod_ref[2, pl.ds(b, 1), :]) * upd)

    @pl.when(i % TILES_PER_BATCH == 0)
    def _():
        o_ref[:CTX_LEN, :] = fin(x_ref[:CTX_LEN, :]
                                 + (gate_scale * mod_ref[2, CTX_ROW:CTX_ROW + 1, :]) * upd[:CTX_LEN])


def _mod_spec(sub):
    return pl.BlockSpec((3, MOD_ROWS, D_MODEL), lambda *idx: (sub, 0, 0))


def _ffn_kernel(*refs, with_final):
    if with_final:
        x_ref, g_ref, mod_ref, win_ref, wout_ref, fw_ref, o_ref, h_ref, acc_ref = refs
    else:
        x_ref, g_ref, mod_ref, win_ref, wout_ref, o_ref, h_ref, acc_ref = refs
        fw_ref = None
    _ada_in_tile(x_ref[...], g_ref, mod_ref, h_ref)
    for f in range(D_FF // FF_CHUNK):
        h = h_ref[...]
        gg = _dot(h, win_ref[:, f * FF_CHUNK:(f + 1) * FF_CHUNK])
        uu = _dot(h, win_ref[:, D_FF + f * FF_CHUNK:D_FF + (f + 1) * FF_CHUNK])
        a = (_silu(gg) * uu).astype(BF16)
        part = _dot(a, wout_ref[f * FF_CHUNK:(f + 1) * FF_CHUNK, :])
        if f == 0:
            acc_ref[...] = part
        else:
            acc_ref[...] += part
    _gated_residual_store(x_ref, acc_ref[...], mod_ref, o_ref, 0.5, fw_ref)


def _ffn(x, g, mods, sub, w_in, w_out, final_w=None):
    t = x.shape[0]
    single = pl.Buffered(1)
    in_specs = [pl.BlockSpec((TM, D_MODEL), lambda i: (i, 0)),
                pl.BlockSpec((1, D_MODEL), lambda i: (0, 0)),
                _mod_spec(sub),
                pl.BlockSpec((D_MODEL, 2 * D_FF), lambda i: (0, 0), pipeline_mode=single),
                pl.BlockSpec((D_FF, D_MODEL), lambda i: (0, 0), pipeline_mode=single)]
    args = [x, g.reshape(1, D_MODEL), mods, w_in, w_out]
    if final_w is not None:
        in_specs.append(pl.BlockSpec((1, D_MODEL), lambda i: (0, 0)))
        args.append(final_w.reshape(1, D_MODEL))
    return pl.pallas_call(
        functools.partial(_ffn_kernel, with_final=final_w is not None),
        grid=(t // TM,),
        in_specs=in_specs,
        out_specs=pl.BlockSpec((TM, D_MODEL), lambda i: (i, 0)),
        out_shape=jax.ShapeDtypeStruct((t, D_MODEL), F32),
        scratch_shapes=[pltpu.VMEM((TM, D_MODEL), BF16), pltpu.VMEM((TM, D_MODEL), F32)],
        compiler_params=_cparams(("parallel",)),
        name="ffn",
    )(*args)


def _proj_kernel(x_ref, g_ref, mod_ref, w_ref, o_ref, h_ref):
    @pl.when(pl.program_id(1) == 0)
    def _():
        _ada_in_tile(x_ref[...], g_ref, mod_ref, h_ref)

    o_ref[...] = _dot(h_ref[...], w_ref[...])


def _proj(x, g, mods, w, tn):
    t = x.shape[0]
    n = w.shape[1]
    return pl.pallas_call(
        _proj_kernel,
        grid=(t // TM, n // tn),
        in_specs=[pl.BlockSpec((TM, D_MODEL), lambda i, j: (i, 0)),
                  pl.BlockSpec((1, D_MODEL), lambda i, j: (0, 0)),
                  _mod_spec(1),
                  pl.BlockSpec((D_MODEL, tn), lambda i, j: (0, j))],
        out_specs=pl.BlockSpec((TM, tn), lambda i, j: (i, j)),
        out_shape=jax.ShapeDtypeStruct((t, n), F32),
        scratch_shapes=[pltpu.VMEM((TM, D_MODEL), BF16)],
        compiler_params=_cparams(("parallel", "arbitrary")),
        name="mixer_in_proj",
    )(x, g.reshape(1, D_MODEL), mods, w)


def _out_proj_kernel(*refs, n_parts):
    x_ref, mod_ref = refs[0], refs[1]
    y_refs = refs[2:2 + n_parts]
    w_refs = refs[2 + n_parts:2 + 2 * n_parts]
    o_ref = refs[2 + 2 * n_parts]
    upd = None
    for y_ref, w_ref in zip(y_refs, w_refs):
        part = _dot(y_ref[...].astype(BF16), w_ref[...])
        upd = part if upd is None else upd + part
    _gated_residual_store(x_ref, upd, mod_ref, o_ref, 1.0, None)


def _out_proj(x, mods, ys, ws):
    t = x.shape[0]
    n_parts = len(ys)
    in_specs = [pl.BlockSpec((TM, D_MODEL), lambda i: (i, 0)), _mod_spec(1)]
    in_specs += [pl.BlockSpec((TM, y.shape[1]), lambda i: (i, 0)) for y in ys]
    in_specs += [pl.BlockSpec(w.shape, lambda i: (0, 0)) for w in ws]
    return pl.pallas_call(
        functools.partial(_out_proj_kernel, n_parts=n_parts),
        grid=(t // TM,),
        in_specs=in_specs,
        out_specs=pl.BlockSpec((TM, D_MODEL), lambda i: (i, 0)),
        out_shape=jax.ShapeDtypeStruct((t, D_MODEL), F32),
        compiler_params=_cparams(("parallel",)),
        name="mixer_out_proj",
    )(x, mods, *ys, *ws)


RC = CTX_LEN
RET_NCHUNK = LT // RC


def _log_sigmoid(x):
    return jnp.minimum(x, 0.0) - jnp.log1p(jnp.exp(-jnp.abs(x)))


def _ret_kernel(q_ref, k_ref, v_ref, g_ref, cos_ref, sin_ref, dl_ref, o_ref,
                qb_ref, kb_ref, kv_ref, s_ref):
    lg = _log_sigmoid(dl_ref[0])
    lg_f, lg_b = lg[0:1, :], lg[1:2, :]
    cos, sin = cos_ref[...], sin_ref[...]
    q = q_ref[0]
    k = k_ref[0] * (RET_KEY_DIM ** -0.5)
    half = RET_KEY_DIM // 2
    qb_ref[...] = (q * cos + pltpu.roll(q, half, 1) * sin).astype(BF16)
    kr = k * cos + pltpu.roll(k, half, 1) * sin
    kb_ref[...] = kr.astype(BF16)

    pos = lax.broadcasted_iota(jnp.int32, (RC, 1), 0).astype(F32)
    kdec_f = jnp.exp(lg_f * (RC - 1.0 - pos))
    kdec_b = jnp.exp(lg_b * pos)
    qdec_f = jnp.exp(lg_f * (pos + 1.0))
    qdec_b = jnp.exp(lg_b * (RC - pos))
    c_f = jnp.exp(lg_f * RC)
    c_b = jnp.exp(lg_b * RC)
    ii = lax.broadcasted_iota(jnp.int32, (RC, RC), 0)
    jj = lax.broadcasted_iota(jnp.int32, (RC, RC), 1)
    dist = (ii - jj).astype(F32)
    dmat = (jnp.where(ii >= jj, jnp.exp(lg_f * dist), 0.0)
            + jnp.where(ii <= jj, jnp.exp(-lg_b * dist), 0.0))

    tn_dims = (((0,), (0,)), ((), ()))
    for n in range(RET_NCHUNK):
        rows = slice(n * RC, (n + 1) * RC)
        kn = kr[rows]
        vn = v_ref[0, rows, :].astype(BF16)
        kv_ref[0, n] = lax.dot_general((kn * kdec_f).astype(BF16), vn, tn_dims,
                                       preferred_element_type=F32)
        kv_ref[1, n] = lax.dot_general((kn * kdec_b).astype(BF16), vn, tn_dims,
                                       preferred_element_type=F32)

    sf = jnp.zeros((RET_KEY_DIM, RET_VALUE_DIM), F32)
    for n in range(RET_NCHUNK):
        s_ref[n, :, :RET_VALUE_DIM] = sf.astype(BF16)
        sf = sf * c_f + kv_ref[0, n]
    s_ref[0, :, RET_VALUE_DIM:] = jnp.zeros((RET_KEY_DIM, RET_VALUE_DIM), BF16)
    sb = kv_ref[1, 0]
    for n in range(RET_NCHUNK - 1, 0, -1):
        s_ref[n, :, RET_VALUE_DIM:] = sb.astype(BF16)
        sb = sb * c_b + kv_ref[1, n]

    nt_dims = (((1,), (1,)), ((), ()))
    for n in range(RET_NCHUNK):
        rows = slice(n * RC, (n + 1) * RC)
        qn = qb_ref[rows, :]
        s = lax.dot_general(qn, kb_ref[rows, :], nt_dims, preferred_element_type=F32)
        o = _dot((s * dmat).astype(BF16), v_ref[0, rows, :].astype(BF16))
        cross = _dot(qn, s_ref[n])
        o = o + cross[:, :RET_VALUE_DIM] * qdec_f + cross[:, RET_VALUE_DIM:] * qdec_b
        y = o * lax.rsqrt(jnp.mean(o * o, axis=-1, keepdims=True) + EPS)
        o_ref[0, rows, :] = (y * _silu(g_ref[0, rows, :])).astype(BF16)


def _ret_rope_tables():
    inv = ROPE_BASE ** (-jnp.linspace(0.0, 1.0, RET_KEY_DIM // 2, dtype=F32))
    ang = jnp.arange(SEQ, dtype=F32)[:, None] * inv
    ang = jnp.concatenate([jnp.zeros((CTX_LEN, RET_KEY_DIM // 2), F32), ang], axis=0)
    cos = jnp.concatenate([jnp.cos(ang)] * 2, axis=-1)
    sin = jnp.concatenate([-jnp.sin(ang), jnp.sin(ang)], axis=-1)
    return cos, sin


def _ret_mixer(p, decay_logit):
    bsz = p.shape[0]
    cos, sin = _ret_rope_tables()
    dl = decay_logit.T.reshape(RET_HEADS, 2, 1)
    kq = RET_HEADS
    kv = (2 * RET_HEADS * RET_KEY_DIM) // RET_VALUE_DIM
    return pl.pallas_call(
        _ret_kernel,
        grid=(bsz, RET_HEADS),
        in_specs=[pl.BlockSpec((1, LT, RET_KEY_DIM), lambda b, h: (b, 0, h)),
                  pl.BlockSpec((1, LT, RET_KEY_DIM), lambda b, h: (b, 0, kq + h)),
                  pl.BlockSpec((1, LT, RET_VALUE_DIM), lambda b, h: (b, 0, kv + h)),
                  pl.BlockSpec((1, LT, RET_VALUE_DIM), lambda b, h: (b, 0, kv + RET_HEADS + h)),
                  pl.BlockSpec((LT, RET_KEY_DIM), lambda b, h: (0, 0)),
                  pl.BlockSpec((LT, RET_KEY_DIM), lambda b, h: (0, 0)),
                  pl.BlockSpec((1, 2, 1), lambda b, h: (h, 0, 0))],
        out_specs=pl.BlockSpec((1, LT, RET_VALUE_DIM), lambda b, h: (b, 0, h)),
        out_shape=jax.ShapeDtypeStruct((bsz, LT, RET_V_WIDTH), BF16),
        scratch_shapes=[pltpu.VMEM((LT, RET_KEY_DIM), BF16),
                        pltpu.VMEM((LT, RET_KEY_DIM), BF16),
                        pltpu.VMEM((2, RET_NCHUNK, RET_KEY_DIM, RET_VALUE_DIM), F32),
                        pltpu.VMEM((RET_NCHUNK, RET_KEY_DIM, 2 * RET_VALUE_DIM), BF16)],
        compiler_params=_cparams(("parallel", "parallel")),
        name="retention",
    )(p, p, p, p, cos, sin, dl)


TQ = CTX_LEN
DIFF_NQ = LT // TQ


def _axial_rope_tile(x, cos, sin_lo, sin_hi):
    q = DIFF_QK_DIM // 4
    return x * cos + pltpu.roll(x, 128 - q, 1) * sin_lo + pltpu.roll(x, q, 1) * sin_hi


def _diff_kernel(q_ref, k_ref, v_ref, cos_ref, slo_ref, shi_ref, lam_ref, w_ref, o_ref,
                 kb_ref, vb_ref, *, lambda_init):
    i = pl.program_id(2)

    @pl.when(i == 0)
    def _():
        kb_ref[...] = _axial_rope_tile(k_ref[0], cos_ref[...], slo_ref[...], shi_ref[...]).astype(BF16)
        vb_ref[...] = v_ref[0].astype(BF16)

    lam = lam_ref[...]
    lam_full = (jnp.exp(jnp.sum(lam[0:1] * lam[1:2], axis=-1, keepdims=True))
                - jnp.exp(jnp.sum(lam[2:3] * lam[3:4], axis=-1, keepdims=True)) + lambda_init)
    r0 = pl.multiple_of(i * TQ, TQ)
    q = q_ref[0] * (DIFF_QK_DIM ** -0.5)
    qr = _axial_rope_tile(q, cos_ref[pl.ds(r0, TQ), :], slo_ref[pl.ds(r0, TQ), :], shi_ref[pl.ds(r0, TQ), :])
    first_map = lax.broadcasted_iota(jnp.int32, (TQ, DIFF_V_DIM), 1) < DIFF_QK_DIM
    qs = jnp.concatenate([jnp.where(first_map, qr, 0.0), jnp.where(first_map, 0.0, qr)],
                         axis=0).astype(BF16)

    def attend(n_keys):
        s = lax.dot_general(qs, kb_ref[:n_keys, :], (((1,), (1,)), ((), ())),
                            preferred_element_type=F32)
        e = jnp.exp(s - jnp.max(s, axis=-1, keepdims=True))
        l = jnp.sum(e, axis=-1, keepdims=True)
        ob = _dot(e.astype(BF16), vb_ref[:n_keys, :]) / l
        o = ob[:TQ] - lam_full * ob[TQ:]
        y = o * lax.rsqrt(jnp.mean(o * o, axis=-1, keepdims=True) + EPS) * w_ref[...]
        o_ref[0] = (y * (1.0 - lambda_init)).astype(BF16)

    @pl.when(i == 0)
    def _():
        attend(CTX_LEN)

    @pl.when(i > 0)
    def _():
        attend(LT)


def _diff_rope_tables():
    rows = SEQ // GRID_W
    r, cidx = jnp.meshgrid(jnp.arange(rows), jnp.arange(GRID_W), indexing='ij')
    axis_dim = DIFF_QK_DIM // 2
    inv = ROPE_BASE ** (-jnp.arange(0, axis_dim, 2, dtype=F32) / axis_dim)
    ang_r = r.reshape(-1).astype(F32)[:, None] * inv
    ang_c = cidx.reshape(-1).astype(F32)[:, None] * inv
    zero = jnp.zeros_like(ang_r)

    def table(fn_lo, fn_hi):
        one = jnp.concatenate([fn_lo(ang_r), fn_hi(ang_r), fn_lo(ang_c), fn_hi(ang_c)], axis=-1)
        return jnp.concatenate([one, one], axis=-1)

    cos = table(jnp.cos, jnp.cos)
    sin_lo = table(lambda a: -jnp.sin(a), lambda a: zero)
    sin_hi = table(lambda a: zero, jnp.sin)
    ident = [jnp.ones((CTX_LEN, DIFF_V_DIM), F32), jnp.zeros((CTX_LEN, DIFF_V_DIM), F32),
             jnp.zeros((CTX_LEN, DIFF_V_DIM), F32)]
    return [jnp.concatenate([c, t], axis=0) for c, t in zip(ident, (cos, sin_lo, sin_hi))]


def _diff_mixer(p, lam, subln_w, layer_idx):
    bsz = p.shape[0]
    lambda_init = 0.8 - 0.6 * math.exp(-0.3 * layer_idx)
    cos, sin_lo, sin_hi = _diff_rope_tables()
    q0 = 4 * DN_WIDTH // DIFF_V_DIM
    k0 = q0 + DIFF_HEADS
    v0 = k0 + DIFF_HEADS
    tab = pl.BlockSpec((LT, DIFF_V_DIM), lambda b, h, i: (0, 0))
    return pl.pallas_call(
        functools.partial(_diff_kernel, lambda_init=lambda_init),
        grid=(bsz, DIFF_HEADS, DIFF_NQ),
        in_specs=[pl.BlockSpec((1, TQ, DIFF_V_DIM), lambda b, h, i: (b, i, q0 + h)),
                  pl.BlockSpec((1, LT, DIFF_V_DIM), lambda b, h, i: (b, 0, k0 + h)),
                  pl.BlockSpec((1, LT, DIFF_V_DIM), lambda b, h, i: (b, 0, v0 + h)),
                  tab, tab, tab,
                  pl.BlockSpec((4, DIFF_QK_DIM), lambda b, h, i: (0, 0)),
                  pl.BlockSpec((1, DIFF_V_DIM), lambda b, h, i: (0, 0))],
        out_specs=pl.BlockSpec((1, TQ, DIFF_V_DIM), lambda b, h, i: (b, i, h)),
        out_shape=jax.ShapeDtypeStruct((bsz, LT, DIFF_WIDTH), BF16),
        scratch_shapes=[pltpu.VMEM((LT, DIFF_V_DIM), BF16), pltpu.VMEM((LT, DIFF_V_DIM), BF16)],
        compiler_params=_cparams(("parallel", "parallel", "arbitrary")),
        name="diff_attention",
    )(p, p, p, cos, sin_lo, sin_hi, lam, subln_w.reshape(1, DIFF_V_DIM))


DC = DN_CHUNK
DN_NCHUNK = LT // DC
DN_CTX_CHUNKS = CTX_LEN // DC


def _split2(x):
    hi = x.astype(BF16)
    return hi, (x - hi.astype(F32)).astype(BF16)


def _mm_hilo(a, b):
    ah, al = _split2(a)
    bh, bl = _split2(b)
    return _dot(ah, bh) + (_dot(ah, bl) + _dot(al, bh))


def _mm_exact_lhs(a_bf16, x):
    h1 = x.astype(BF16)
    r1 = x - h1.astype(F32)
    h2 = r1.astype(BF16)
    h3 = (r1 - h2.astype(F32)).astype(BF16)
    return _dot(a_bf16, h1) + (_dot(a_bf16, h2) + _dot(a_bf16, h3))


def _softplus(x):
    return jnp.maximum(x, 0.0) + jnp.log1p(jnp.exp(-jnp.abs(x)))


def _dn_kernel(q_ref, k_ref, v_ref, z_ref, gt_ref, cq_ref, ck_ref, cv_ref, prm_ref, nw_ref, o_ref,
               qn_ref, kn_ref, vn_ref, la_ref, be_ref, wq_ref, u_ref, qk_ref, kdt_ref, e_ref,
               vnew_ref, oacc_ref, s_ref):
    h = pl.program_id(1)
    tpos = lax.broadcasted_iota(jnp.int32, (LT, 1), 0)
    has_prev = jnp.logical_and(tpos != 0, tpos != CTX_LEN)
    has_next = jnp.logical_and(tpos != CTX_LEN - 1, tpos != LT - 1)

    def conv_silu(x_ref, c_ref):
        x = x_ref[0]
        w = c_ref[...]
        prev = jnp.where(has_prev, pltpu.roll(x, 1, 0), 0.0)
        nxt = jnp.where(has_next, pltpu.roll(x, LT - 1, 0), 0.0)
        return _silu(prev * w[0:1] + x * w[1:2] + nxt * w[2:3])

    def l2n(t):
        return t * lax.rsqrt(jnp.sum(t * t, axis=-1, keepdims=True) + EPS)

    qn_ref[...] = l2n(conv_silu(q_ref, cq_ref)) * (DN_HEAD_DIM ** -0.5)
    kn_ref[...] = l2n(conv_silu(k_ref, ck_ref))
    vn_ref[...] = conv_silu(v_ref, cv_ref)

    gates = gt_ref[0]
    glane = lax.broadcasted_iota(jnp.int32, gates.shape, 1)
    prm = prm_ref[0]
    for d in range(2):
        a = jnp.sum(jnp.where(glane == d * DN_HEADS + h, gates, 0.0), axis=-1, keepdims=True)
        b = jnp.sum(jnp.where(glane == (2 + d) * DN_HEADS + h, gates, 0.0), axis=-1, keepdims=True)
        log_a = -jnp.exp(prm[d:d + 1, :]) * _softplus(a + prm[2 + d:3 + d, :])
        beta = 1.0 / (1.0 + jnp.exp(-b))
        la_ref[d] = jnp.broadcast_to(log_a, (LT, DN_HEAD_DIM))
        be_ref[d] = jnp.broadcast_to(beta, (LT, DN_HEAD_DIM))

    row = lax.broadcasted_iota(jnp.int32, (DC, 2 * DC), 0)
    lane = lax.broadcasted_iota(jnp.int32, (DC, 2 * DC), 1)
    col = lane & (DC - 1)
    isf = lane < DC
    eye2 = row == col
    isb = jnp.logical_not(isf)
    incl = jnp.logical_or(jnp.logical_and(isf, row >= col), jnp.logical_and(isb, row <= col))
    strict = jnp.logical_or(jnp.logical_and(isf, row > col), jnp.logical_and(isb, row < col))
    r64 = lax.broadcasted_iota(jnp.int32, (DC, DC), 0)
    c64 = lax.broadcasted_iota(jnp.int32, (DC, DC), 1)
    cum_lhs = jnp.concatenate([(r64 >= c64).astype(BF16), jnp.ones((DC, DC), BF16)], axis=0)
    ones_lhs = jnp.ones((DC, DC), BF16)
    rr = lax.broadcasted_iota(jnp.int32, (2 * DC, 2 * DC), 0)
    cc = lax.broadcasted_iota(jnp.int32, (2 * DC, 2 * DC), 1)
    eye_bd = (rr == cc).astype(F32)
    same16 = (rr >> 4) == (cc >> 4)
    same32 = (rr >> 5) == (cc >> 5)
    same64 = (rr >> 6) == (cc >> 6)
    nt_dims = (((1,), (1,)), ((), ()))

    def prep(n, carry):
        r0 = pl.multiple_of(n * DC, DC)
        rows = pl.ds(r0, DC)
        qn, kn, vn = qn_ref[rows, :], kn_ref[rows, :], vn_ref[rows, :]
        la_f, la_b = la_ref[0, rows, :], la_ref[1, rows, :]
        be_f, be_b = be_ref[0, rows, :], be_ref[1, rows, :]
        pt_f = _mm_exact_lhs(cum_lhs, la_f)
        pt_b = _mm_exact_lhs(cum_lhs, la_b)
        g_f, tot_f = pt_f[:DC], pt_f[DC:]
        tot_b = pt_b[DC:]
        g_b = tot_b - pt_b[:DC] + la_b
        gg = jnp.where(isf, g_f, g_b)
        g_row = _mm_exact_lhs(ones_lhs, jnp.where(eye2, gg, 0.0))
        gam = jnp.where(incl, jnp.exp(gg - g_row), 0.0)
        kb16 = kn.astype(BF16)
        kcat = jnp.concatenate([kb16, kb16], axis=0)
        kk2 = lax.dot_general(kb16, kcat, nt_dims, preferred_element_type=F32)
        qk2 = lax.dot_general(qn.astype(BF16), kcat, nt_dims, preferred_element_type=F32)
        a2 = jnp.where(strict, kk2 * jnp.where(isf, be_f, be_b) * gam, 0.0)
        abd = jnp.concatenate([jnp.where(isf, a2, 0.0), jnp.where(isf, 0.0, a2)], axis=0)
        dmat = jnp.where(same16, abd, 0.0)
        d2 = _mm_hilo(dmat, dmat)
        d4 = _mm_hilo(d2, d2)
        d8 = _mm_hilo(d4, d4)
        inv = _mm_hilo(_mm_hilo(_mm_hilo(eye_bd - dmat, eye_bd + d2), eye_bd + d4), eye_bd + d8)
        e16 = jnp.where(jnp.logical_and(same32, jnp.logical_not(same16)), abd, 0.0)
        inv = inv - _mm_hilo(_mm_hilo(inv, e16), inv)
        e32 = jnp.where(jnp.logical_and(same64, jnp.logical_not(same32)), abd, 0.0)
        inv = inv - _mm_hilo(_mm_hilo(inv, e32), inv)
        eg_f, eg_b = jnp.exp(g_f), jnp.exp(g_b)
        rhs = jnp.concatenate([jnp.concatenate([vn * be_f, kn * be_f * eg_f], axis=1),
                               jnp.concatenate([vn * be_b, kn * be_b * eg_b], axis=1)], axis=0)
        uw = _mm_hilo(inv, rhs)
        u_ref[n, 0] = uw[:DC, :DN_HEAD_DIM]
        u_ref[n, 1] = uw[DC:, :DN_HEAD_DIM]
        wq_ref[n, 0] = jnp.concatenate([uw[:DC, DN_HEAD_DIM:], qn * eg_f], axis=0).astype(BF16)
        wq_ref[n, 1] = jnp.concatenate([uw[DC:, DN_HEAD_DIM:], qn * eg_b], axis=0).astype(BF16)
        qk_ref[n] = (qk2 * gam).astype(BF16)
        kdt_ref[n, 0] = jnp.transpose(kn * jnp.exp(tot_f - g_f)).astype(BF16)
        kdt_ref[n, 1] = jnp.transpose(kn * jnp.exp(tot_b - g_b)).astype(BF16)
        e_ref[n, 0] = jnp.exp(tot_f[0:8])
        e_ref[n, 1] = jnp.exp(tot_b[0:8])
        return carry

    lax.fori_loop(0, DN_NCHUNK, prep, 0)

    s_ref[...] = jnp.zeros(s_ref.shape, F32)
    oacc_ref[...] = jnp.zeros(oacc_ref.shape, F32)

    def scan_step(s, carry):
        nb = jnp.where(s < DN_CTX_CHUNKS, DN_CTX_CHUNKS - 1 - s, DN_NCHUNK - 1 + DN_CTX_CHUNKS - s)
        for d, n in ((0, s), (1, nb)):
            state = s_ref[d]
            sw = _dot(wq_ref[n, d], state.astype(BF16))
            v_new = (u_ref[n, d] - sw[:DC]).astype(BF16)
            vnew_ref[n, d] = v_new
            rows = pl.ds(pl.multiple_of(n * DC, DC), DC)
            oacc_ref[rows, :] += sw[DC:]
            s_ref[d] = state * e_ref[n, d][0:1, :] + _dot(kdt_ref[n, d], v_new)
        return carry

    lax.fori_loop(0, DN_NCHUNK, scan_step, 0)

    def finish(n, carry):
        rows = pl.ds(pl.multiple_of(n * DC, DC), DC)
        vcat = jnp.concatenate([vnew_ref[n, 0], vnew_ref[n, 1]], axis=0)
        o = oacc_ref[rows, :] + _dot(qk_ref[n], vcat)
        y = o * lax.rsqrt(jnp.mean(o * o, axis=-1, keepdims=True) + EPS) * nw_ref[...]
        o_ref[0, rows, :] = (y * _silu(z_ref[0, rows, :])).astype(BF16)
        return carry

    lax.fori_loop(0, DN_NCHUNK, finish, 0)


def _dn_mixer(p, conv_w, a_log, dt_bias, norm_w):
    bsz = p.shape[0]
    prm = jnp.concatenate([a_log, dt_bias], axis=0).T.reshape(DN_HEADS, 4, 1)
    gate_blk = AB_MAIN_COLS // AB_GATE_COLS
    hd = DN_HEAD_DIM
    seq = lambda off: pl.BlockSpec((1, LT, hd), lambda b, h: (b, 0, off + h))
    cw = lambda off: pl.BlockSpec((CONV_WIDTH, hd), lambda b, h: (0, off + h))
    return pl.pallas_call(
        _dn_kernel,
        grid=(bsz, DN_HEADS),
        in_specs=[seq(0), seq(DN_HEADS), seq(2 * DN_HEADS), seq(3 * DN_HEADS),
                  pl.BlockSpec((1, LT, AB_GATE_COLS), lambda b, h: (b, 0, gate_blk)),
                  cw(0), cw(DN_HEADS), cw(2 * DN_HEADS),
                  pl.BlockSpec((1, 4, 1), lambda b, h: (h, 0, 0)),
                  pl.BlockSpec((1, hd), lambda b, h: (0, 0))],
        out_specs=pl.BlockSpec((1, LT, hd), lambda b, h: (b, 0, h)),
        out_shape=jax.ShapeDtypeStruct((bsz, LT, DN_WIDTH), BF16),
        scratch_shapes=[pltpu.VMEM((LT, hd), F32), pltpu.VMEM((LT, hd), F32), pltpu.VMEM((LT, hd), F32),
                        pltpu.VMEM((2, LT, hd), F32), pltpu.VMEM((2, LT, hd), F32),
                        pltpu.VMEM((DN_NCHUNK, 2, 2 * DC, hd), BF16),
                        pltpu.VMEM((DN_NCHUNK, 2, DC, hd), F32),
                        pltpu.VMEM((DN_NCHUNK, DC, 2 * DC), BF16),
                        pltpu.VMEM((DN_NCHUNK, 2, hd, DC), BF16),
                        pltpu.VMEM((DN_NCHUNK, 2, 8, hd), F32),
                        pltpu.VMEM((DN_NCHUNK, 2, DC, hd), BF16),
                        pltpu.VMEM((LT, hd), F32),
                        pltpu.VMEM((2, hd, hd), F32)],
        compiler_params=_cparams(("parallel", "parallel")),
        name="gated_deltanet",
    )(p, p, p, p, p, conv_w, conv_w, conv_w, prm, norm_w.reshape(1, hd))


def _rms_norm(x, w=None):
    y = x * lax.rsqrt(jnp.mean(x * x, axis=-1, keepdims=True) + EPS)
    return y if w is None else y * w


def _l2norm(t):
    return t * lax.rsqrt(jnp.sum(t * t, axis=-1, keepdims=True) + EPS)


def _to_heads(t, n):
    B, L, _ = t.shape
    return t.reshape(B, L, n, -1).transpose(0, 2, 1, 3)


def _rotate_half(x):
    x1, x2 = jnp.split(x, 2, axis=-1)
    return jnp.concatenate([-x2, x1], axis=-1)


def _rope(x, ang):
    cos = jnp.concatenate([jnp.cos(ang)] * 2, axis=-1)
    sin = jnp.concatenate([jnp.sin(ang)] * 2, axis=-1)
    return x * cos + _rotate_half(x) * sin


def _axial_angles(L):
    rows = L // GRID_W
    r, cidx = jnp.meshgrid(jnp.arange(rows), jnp.arange(GRID_W), indexing='ij')
    axis_dim = DIFF_QK_DIM // 2
    inv = ROPE_BASE ** (-jnp.arange(0, axis_dim, 2, dtype=F32) / axis_dim)
    return (r.reshape(-1).astype(F32)[:, None] * inv, cidx.reshape(-1).astype(F32)[:, None] * inv)


def _axial_rope(x, ang_r, ang_c):
    xr, xc = jnp.split(x, 2, axis=-1)
    return jnp.concatenate([_rope(xr, ang_r), _rope(xc, ang_c)], axis=-1)


def _short_conv(u, w):
    k = w.shape[0]
    return lax.conv_general_dilated(u, w.astype(u.dtype)[:, None, :], window_strides=(1,),
                                    padding=[(k // 2, k // 2)],
                                    dimension_numbers=('NWC', 'WIO', 'NWC'),
                                    feature_group_count=u.shape[-1])


def _two_pass(fn, ctx_seqs, lat_seqs, S0):
    outs_c, outs_l = [], []
    for d in range(2):
        f = (lambda t: jnp.flip(t, 2)) if d == 1 else (lambda t: t)
        oc, S = fn(d, [f(t) for t in ctx_seqs], S0)
        ol, _ = fn(d, [f(t) for t in lat_seqs], S)
        outs_c.append(f(oc))
        outs_l.append(f(ol))
    return outs_c[0] + outs_c[1], outs_l[0] + outs_l[1]


def _gated_delta_chunked(q, k, v, log_a, beta, S0):
    B, H, L, _ = q.shape
    dv = v.shape[-1]
    C = DN_CHUNK
    n = L // C
    q, k, v = (t.reshape(B, H, n, C, t.shape[-1]) for t in (q, k, v))
    log_a, beta = (t.reshape(B, H, n, C) for t in (log_a, beta))
    g = jnp.cumsum(log_a, axis=-1)
    tril = jnp.tril(jnp.ones((C, C), bool))
    strict = jnp.tril(jnp.ones((C, C), bool), -1)
    gamma = jnp.exp(jnp.where(tril, g[..., :, None] - g[..., None, :], -jnp.inf))
    kb = k * beta[..., None]
    lower = jnp.where(strict, jnp.einsum('bhnid,bhnjd->bhnij', kb, k) * gamma, 0.0)
    tmat = lower + jnp.eye(C, dtype=lower.dtype)
    solve = lambda rhs: lax.linalg.triangular_solve(tmat, rhs, left_side=True, lower=True,
                                                    unit_diagonal=True)
    u = solve(v * beta[..., None])
    w = solve(kb * jnp.exp(g)[..., None])
    qk = jnp.einsum('bhnid,bhnjd->bhnij', q, k) * gamma

    def step(S, inp):
        q_i, k_i, u_i, w_i, g_i, qk_i = inp
        v_new = u_i - jnp.einsum('bhcd,bhde->bhce', w_i, S)
        o = (jnp.einsum('bhcd,bhde->bhce', q_i * jnp.exp(g_i)[..., None], S)
             + jnp.einsum('bhij,bhje->bhie', qk_i, v_new))
        g_last = g_i[..., -1:]
        S = (S * jnp.exp(g_last)[..., None]
             + jnp.einsum('bhcd,bhce->bhde', k_i * jnp.exp(g_last - g_i)[..., None], v_new))
        return S, o

    xs = tuple(jnp.moveaxis(t, 2, 0) for t in (q, k, u, w, g, qk))
    S, o = lax.scan(step, S0, xs)
    return jnp.moveaxis(o, 0, 2).reshape(B, H, L, dv), S


def _retention_chunked(q, k, v, log_gamma, S0):
    B, H, L, _ = q.shape
    dv = v.shape[-1]
    C = RET_CHUNK
    n = L // C
    q, k, v = (t.reshape(B, H, n, C, t.shape[-1]) for t in (q, k, v))
    pos = jnp.arange(C, dtype=F32)
    lg = log_gamma[:, None]
    tril = jnp.tril(jnp.ones((C, C), bool))
    dec = jnp.exp(jnp.where(tril, (pos[:, None] - pos[None, :]) * lg[:, :, None], -jnp.inf))
    o_inner = jnp.einsum('bhnij,bhnje->bhnie',
                         jnp.einsum('bhnid,bhnjd->bhnij', q, k) * dec[:, None], v)
    q_dec = jnp.exp(lg * (pos + 1))[:, :, None]
    k_dec = jnp.exp(lg * (C - 1 - pos))[:, :, None]
    c_dec = jnp.exp(log_gamma * C)[:, None, None]

    def step(S, inp):
        q_i, k_i, v_i, o_i = inp
        o = o_i + jnp.einsum('bhcd,bhde->bhce', q_i, S) * q_dec
        S = S * c_dec + jnp.einsum('bhcd,bhce->bhde', k_i * k_dec, v_i)
        return S, o

    xs = tuple(jnp.moveaxis(t, 2, 0) for t in (q, k, v, o_inner))
    S, o = lax.scan(step, S0, xs)
    return jnp.moveaxis(o, 0, 2).reshape(B, H, L, dv), S


def _diff_attend(q, k, v, lam_full):
    s = jnp.einsum('bhmqd,bhmkd->bhmqk', q, k) * DIFF_QK_DIM ** -0.5
    p = jax.nn.softmax(s, axis=-1)
    return jnp.einsum('bhqk,bhkd->bhqd', p[:, :, 0] - lam_full * p[:, :, 1], v)


def _ab_mixer_jax(pc, pl_, conv_w, A_log, dt_bias, dn_norm_w, lam, subln_w, layer_idx):
    B, L, _ = pl_[0].shape

    def dn_seqs(p):
        q, k, v, _, a, b = p[:6]
        qkv = jax.nn.silu(_short_conv(jnp.concatenate([q, k, v], axis=-1), conv_w))
        q, k, v = (_to_heads(t, DN_HEADS) for t in jnp.split(qkv, 3, axis=-1))
        q = _l2norm(q) * DN_HEAD_DIM ** -0.5
        k = _l2norm(k)
        Bs, Ls, _ = a.shape
        a = a.reshape(Bs, Ls, 2, DN_HEADS)
        b = b.reshape(Bs, Ls, 2, DN_HEADS)
        log_a = (-jnp.exp(A_log) * jax.nn.softplus(a + dt_bias)).transpose(0, 3, 1, 2)
        beta = jax.nn.sigmoid(b).transpose(0, 3, 1, 2)
        return [q, k, v, log_a, beta]

    S0 = jnp.zeros((B, DN_HEADS, DN_HEAD_DIM, DN_HEAD_DIM), F32)
    dn_fn = lambda d, s, S: _gated_delta_chunked(s[0], s[1], s[2], s[3][..., d], s[4][..., d], S)
    dn_c, dn_l = _two_pass(dn_fn, dn_seqs(pc), dn_seqs(pl_), S0)

    def dn_merge(o, z):
        Bs, H, Ls, dv = o.shape
        y = _rms_norm(o.transpose(0, 2, 1, 3), dn_norm_w) * jax.nn.silu(z).reshape(Bs, Ls, H, dv)
        return y.reshape(Bs, Ls, H * dv)

    lambda_init = 0.8 - 0.6 * math.exp(-0.3 * layer_idx)
    lam_full = jnp.exp(jnp.sum(lam[0] * lam[1])) - jnp.exp(jnp.sum(lam[2] * lam[3])) + lambda_init

    def diff_qkv(p):
        q, k, v = p[6:]
        Bs, Ls, _ = q.shape
        q, k = (t.reshape(Bs, Ls, DIFF_HEADS, 2, DIFF_QK_DIM).transpose(0, 2, 3, 1, 4) for t in (q, k))
        return q, k, _to_heads(v, DIFF_HEADS)

    def diff_merge(o):
        Bs, H, Ls, dv = o.shape
        return (_rms_norm(o, subln_w) * (1 - lambda_init)).transpose(0, 2, 1, 3).reshape(Bs, Ls, H * dv)

    qc, kc, vc = diff_qkv(pc)
    ql, kl, vl = diff_qkv(pl_)
    ang_r, ang_c = _axial_angles(L)
    ql, kl = _axial_rope(ql, ang_r, ang_c), _axial_rope(kl, ang_r, ang_c)
    k_all = jnp.concatenate([kc, kl], axis=3)
    v_all = jnp.concatenate([vc, vl], axis=2)
    nb = L // Q_BLOCK
    qb = jnp.moveaxis(ql.reshape(B, DIFF_HEADS, 2, nb, Q_BLOCK, DIFF_QK_DIM), 3, 0)
    ol = lax.map(lambda qi: _diff_attend(qi, k_all, v_all, lam_full), qb)
    ol = jnp.moveaxis(ol, 0, 2).reshape(B, DIFF_HEADS, L, DIFF_V_DIM)
    oc = _diff_attend(qc, kc, vc, lam_full)
    y_l = (dn_merge(dn_l, pl_[3]), diff_merge(ol))
    y_c = (dn_merge(dn_c, pc[3]), diff_merge(oc))
    return y_c, y_l


def _ret_mixer_jax(pc, pl_, decay_logit):
    def stream(p, rotate):
        q, k, v, g = p
        q, k, v = (_to_heads(t, RET_HEADS) for t in (q, k, v))
        if rotate:
            inv = ROPE_BASE ** (-jnp.linspace(0.0, 1.0, RET_KEY_DIM // 2, dtype=F32))
            ang = jnp.arange(q.shape[2], dtype=F32)[:, None] * inv
            q, k = _rope(q, ang), _rope(k, ang)
        return [q, k * RET_KEY_DIM ** -0.5, v], g

    sc, gc = stream(pc, False)
    sl, gl = stream(pl_, True)
    log_gamma = jax.nn.log_sigmoid(decay_logit)
    S0 = jnp.zeros((pl_[0].shape[0], RET_HEADS, RET_KEY_DIM, RET_VALUE_DIM), F32)
    ret_fn = lambda d, s, S: _retention_chunked(s[0], s[1], s[2], log_gamma[d], S)
    oc, ol = _two_pass(ret_fn, sc, sl, S0)

    def merge(o, g):
        Bs, H, Ls, dv = o.shape
        return _rms_norm(o.transpose(0, 2, 1, 3)).reshape(Bs, Ls, H * dv) * jax.nn.silu(g)

    return merge(oc, gc), merge(ol, gl)


AB_MAIN_COLS = 4 * DN_WIDTH + 3 * DIFF_WIDTH
AB_GATE_COLS = 256
AB_PROJ_TN = (AB_MAIN_COLS + AB_GATE_COLS) // 3
RET_PROJ_TN = RET_IN // 3


def _split_ab_weight(w):
    o_ab = 4 * DN_WIDTH
    n_ab = 4 * DN_HEADS
    main = jnp.concatenate([w[:, :o_ab], w[:, o_ab + n_ab:]], axis=1)
    gates = jnp.pad(w[:, o_ab:o_ab + n_ab], ((0, 0), (0, AB_GATE_COLS - n_ab)))
    return jnp.concatenate([main, gates], axis=1)


def kernel(x, c, ctx, c_ctx, ada_w, ada_b, norm_w, final_norm_w, ffn_w_in, ffn_w_out, ab_w_in,
           ab_conv_w, dn_A_log, dn_dt_bias, dn_norm_w, diff_lambda, diff_subln_w, ab_w_out,
           ret_w_in, ret_decay_logit, ret_w_out):
    B = x.shape[0]
    xs = jnp.concatenate([ctx, x], axis=1).reshape(B * LT, D_MODEL)
    cvec = jnp.concatenate([c, c_ctx[None], jnp.zeros((MOD_ROWS - B - 1, D_MODEL), F32)], axis=0)
    mods_all = _ada_mods(cvec, ada_w, ada_b)
    mods_all = mods_all.reshape(DEPTH, MOD_ROWS, N_MOD, D_MODEL).transpose(0, 2, 1, 3)
    ffn_w_in_b = ffn_w_in.astype(BF16)
    ffn_w_out_b = ffn_w_out.astype(BF16)
    for l in range(DEPTH):
        last = l == DEPTH - 1
        mods = mods_all[l]
        xs = _ffn(xs, norm_w[l, 0], mods, 0, ffn_w_in_b[l, 0], ffn_w_out_b[l, 0])
        i = l // 2
        if l % 2 == 0:
            w = _split_ab_weight(ab_w_in[i]).astype(BF16)
            p = _proj(xs, norm_w[l, 1], mods, w, AB_PROJ_TN).reshape(B, LT, -1)
            y_dn = _dn_mixer(p, ab_conv_w[i], dn_A_log[i], dn_dt_bias[i], dn_norm_w[i])
            y_df = _diff_mixer(p, diff_lambda[i], diff_subln_w[i], l)
            wo = ab_w_out[i].astype(BF16)
            xs = _out_proj(xs, mods, [y_dn.reshape(B * LT, DN_WIDTH), y_df.reshape(B * LT, DIFF_WIDTH)],
                           [wo[:DN_WIDTH], wo[DN_WIDTH:]])
        else:
            p = _proj(xs, norm_w[l, 1], mods, ret_w_in[i].astype(BF16), RET_PROJ_TN).reshape(B, LT, -1)
            y = _ret_mixer(p, ret_decay_logit[i])
            xs = _out_proj(xs, mods, [y.reshape(B * LT, RET_V_WIDTH)], [ret_w_out[i].astype(BF16)])
        xs = _ffn(xs, norm_w[l, 2], mods, 2, ffn_w_in_b[l, 1], ffn_w_out_b[l, 1],
                  final_w=final_norm_w if last else None)
    return xs.reshape(B, LT, D_MODEL)[:, CTX_LEN:]
```

```python
import functools
import math

import jax
import jax.numpy as jnp
import numpy as np
from jax import lax
from jax.experimental import pallas as pl
from jax.experimental.pallas import tpu as pltpu

D_MODEL = 1024
BATCH = 8
SEQ = 2048
DEPTH = 4
GRID_W = 64
CTX_LEN = 256
EPS = 1e-6
N_MOD = 9
DN_HEADS = 4
DN_HEAD_DIM = 128
DN_WIDTH = DN_HEADS * DN_HEAD_DIM
CONV_WIDTH = 3
DN_CHUNK = 64
DIFF_HEADS = 4
DIFF_QK_DIM = 64
DIFF_V_DIM = 2 * DIFF_QK_DIM
DIFF_WIDTH = DIFF_HEADS * DIFF_V_DIM
Q_BLOCK = 128
ROPE_BASE = 10000.0
RET_HEADS = 8
RET_KEY_DIM = D_MODEL // RET_HEADS
RET_VALUE_DIM = 2 * RET_KEY_DIM
RET_V_WIDTH = RET_HEADS * RET_VALUE_DIM
RET_CHUNK = 64
D_FF = 2816
AB_SIZES = [DN_WIDTH] * 4 + [2 * DN_HEADS] * 2 + [DIFF_WIDTH] * 3
AB_IN = sum(AB_SIZES)
RET_SIZES = [RET_HEADS * RET_KEY_DIM] * 2 + [RET_V_WIDTH] * 2
RET_IN = sum(RET_SIZES)

F32 = jnp.float32
BF16 = jnp.bfloat16

LT = CTX_LEN + SEQ
TM = LT // 2
TILES_PER_BATCH = LT // TM
MOD_ROWS = 16
CTX_ROW = BATCH
V7X_VMEM_LIMIT = 56 * 1024 * 1024
FF_CHUNK = 256


def _cparams(sem):
    return pltpu.CompilerParams(dimension_semantics=sem, vmem_limit_bytes=V7X_VMEM_LIMIT)


def _silu(x):
    return x * (1.0 / (1.0 + jnp.exp(-x)))


def _dot(a, b):
    return jnp.dot(a, b, preferred_element_type=F32)


def _ada_kernel(s_ref, w_ref, b_ref, o_ref):
    s = _silu(s_ref[...]).astype(BF16)
    o_ref[0] = _dot(s, w_ref[0].astype(BF16)) + b_ref[0]


def _ada_mods(cvec, ada_w, ada_b):
    tn = 1024
    n = N_MOD * D_MODEL
    return pl.pallas_call(
        _ada_kernel,
        grid=(DEPTH, n // tn),
        in_specs=[pl.BlockSpec((MOD_ROWS, D_MODEL), lambda l, j: (0, 0)),
                  pl.BlockSpec((1, D_MODEL, tn), lambda l, j: (l, 0, j)),
                  pl.BlockSpec((1, 1, tn), lambda l, j: (l, 0, j))],
        out_specs=pl.BlockSpec((1, MOD_ROWS, tn), lambda l, j: (l, 0, j)),
        out_shape=jax.ShapeDtypeStruct((DEPTH, MOD_ROWS, n), F32),
        compiler_params=_cparams(("parallel", "parallel")),
        name="ada_mods",
    )(cvec, ada_w, ada_b.reshape(DEPTH, 1, n))


def _ada_in_tile(x, g_ref, mod_ref, h_ref):
    i = pl.program_id(0)
    b = i // TILES_PER_BATCH
    y = x * lax.rsqrt(jnp.mean(x * x, axis=-1, keepdims=True) + EPS) * g_ref[...]
    h_ref[...] = (y * (1.0 + mod_ref[1, pl.ds(b, 1), :]) + mod_ref[0, pl.ds(b, 1), :]).astype(BF16)

    @pl.when(i % TILES_PER_BATCH == 0)
    def _():
        yc = y[:CTX_LEN]
        h_ref[:CTX_LEN, :] = (yc * (1.0 + mod_ref[1, CTX_ROW:CTX_ROW + 1, :])
                              + mod_ref[0, CTX_ROW:CTX_ROW + 1, :]).astype(BF16)


def _gated_residual_store(x_ref, upd, mod_ref, o_ref, gate_scale, final_w_ref):
    i = pl.program_id(0)
    b = i // TILES_PER_BATCH

    def fin(v):
        if final_w_ref is None:
            return v
        return v * lax.rsqrt(jnp.mean(v * v, axis=-1, keepdims=True) + EPS) * final_w_ref[...]

    o_ref[...] = fin(x_ref[...] + (gate_scale * mod_ref[2, pl.ds(b, 1), :]) * upd)

    @pl.when(i % TILES_PER_BATCH == 0)
    def _():
        o_ref[:CTX_LEN, :] = fin(x_ref[:CTX_LEN, :]
                                 + (gate_scale * mod_ref[2, CTX_ROW:CTX_ROW + 1, :]) * upd[:CTX_LEN])


def _mod_spec(sub):
    return pl.BlockSpec((3, MOD_ROWS, D_MODEL), lambda *idx: (sub, 0, 0))


def _ffn_kernel(*refs, with_final):
    if with_final:
        x_ref, g_ref, mod_ref, win_ref, wout_ref, fw_ref, o_ref, h_ref, acc_ref = refs
    else:
        x_ref, g_ref, mod_ref, win_ref, wout_ref, o_ref, h_ref, acc_ref = refs
        fw_ref = None
    _ada_in_tile(x_ref[...], g_ref, mod_ref, h_ref)
    for f in range(D_FF // FF_CHUNK):
        h = h_ref[...]
        gg = _dot(h, win_ref[:, f * FF_CHUNK:(f + 1) * FF_CHUNK])
        uu = _dot(h, win_ref[:, D_FF + f * FF_CHUNK:D_FF + (f + 1) * FF_CHUNK])
        a = (_silu(gg) * uu).astype(BF16)
        part = _dot(a, wout_ref[f * FF_CHUNK:(f + 1) * FF_CHUNK, :])
        if f == 0:
            acc_ref[...] = part
        else:
            acc_ref[...] += part
    _gated_residual_store(x_ref, acc_ref[...], mod_ref, o_ref, 0.5, fw_ref)


def _ffn(x, g, mods, sub, w_in, w_out, final_w=None):
    t = x.shape[0]
    single = pl.Buffered(1)
    in_specs = [pl.BlockSpec((TM, D_MODEL), lambda i: (i, 0)),
                pl.BlockSpec((1, D_MODEL), lambda i: (0, 0)),
                _mod_spec(sub),
                pl.BlockSpec((D_MODEL, 2 * D_FF), lambda i: (0, 0), pipeline_mode=single),
                pl.BlockSpec((D_FF, D_MODEL), lambda i: (0, 0), pipeline_mode=single)]
    args = [x, g.reshape(1, D_MODEL), mods, w_in, w_out]
    if final_w is not None:
        in_specs.append(pl.BlockSpec((1, D_MODEL), lambda i: (0, 0)))
        args.append(final_w.reshape(1, D_MODEL))
    return pl.pallas_call(
        functools.partial(_ffn_kernel, with_final=final_w is not None),
        grid=(t // TM,),
        in_specs=in_specs,
        out_specs=pl.BlockSpec((TM, D_MODEL), lambda i: (i, 0)),
        out_shape=jax.ShapeDtypeStruct((t, D_MODEL), F32),
        scratch_shapes=[pltpu.VMEM((TM, D_MODEL), BF16), pltpu.VMEM((TM, D_MODEL), F32)],
        compiler_params=_cparams(("parallel",)),
        name="ffn",
    )(*args)


def _proj_kernel(x_ref, g_ref, mod_ref, w_ref, o_ref, h_ref):
    @pl.when(pl.program_id(1) == 0)
    def _():
        _ada_in_tile(x_ref[...], g_ref, mod_ref, h_ref)

    o_ref[...] = _dot(h_ref[...], w_ref[...])


def _proj(x, g, mods, w, tn):
    t = x.shape[0]
    n = w.shape[1]
    return pl.pallas_call(
        _proj_kernel,
        grid=(t // TM, n // tn),
        in_specs=[pl.BlockSpec((TM, D_MODEL), lambda i, j: (i, 0)),
                  pl.BlockSpec((1, D_MODEL), lambda i, j: (0, 0)),
                  _mod_spec(1),
                  pl.BlockSpec((D_MODEL, tn), lambda i, j: (0, j))],
        out_specs=pl.BlockSpec((TM, tn), lambda i, j: (i, j)),
        out_shape=jax.ShapeDtypeStruct((t, n), F32),
        scratch_shapes=[pltpu.VMEM((TM, D_MODEL), BF16)],
        compiler_params=_cparams(("parallel", "arbitrary")),
        name="mixer_in_proj",
    )(x, g.reshape(1, D_MODEL), mods, w)


def _out_proj_kernel(*refs, n_parts):
    x_ref, mod_ref = refs[0], refs[1]
    y_refs = refs[2:2 + n_parts]
    w_refs = refs[2 + n_parts:2 + 2 * n_parts]
    o_ref = refs[2 + 2 * n_parts]
    upd = None
    for y_ref, w_ref in zip(y_refs, w_refs):
        part = _dot(y_ref[...].astype(BF16), w_ref[...])
        upd = part if upd is None else upd + part
    _gated_residual_store(x_ref, upd, mod_ref, o_ref, 1.0, None)


def _out_proj(x, mods, ys, ws):
    t = x.shape[0]
    n_parts = len(ys)
    in_specs = [pl.BlockSpec((TM, D_MODEL), lambda i: (i, 0)), _mod_spec(1)]
    in_specs += [pl.BlockSpec((TM, y.shape[1]), lambda i: (i, 0)) for y in ys]
    in_specs += [pl.BlockSpec(w.shape, lambda i: (0, 0)) for w in ws]
    return pl.pallas_call(
        functools.partial(_out_proj_kernel, n_parts=n_parts),
        grid=(t // TM,),
        in_specs=in_specs,
        out_specs=pl.BlockSpec((TM, D_MODEL), lambda i: (i, 0)),
        out_shape=jax.ShapeDtypeStruct((t, D_MODEL), F32),
        compiler_params=_cparams(("parallel",)),
        name="mixer_out_proj",
    )(x, mods, *ys, *ws)


RC = CTX_LEN
RET_NCHUNK = LT // RC


def _log_sigmoid(x):
    return jnp.minimum(x, 0.0) - jnp.log1p(jnp.exp(-jnp.abs(x)))


def _ret_kernel(q_ref, k_ref, v_ref, g_ref, cos_ref, sin_ref, dl_ref, o_ref,
                qb_ref, kb_ref, kv_ref, s_ref):
    lg = _log_sigmoid(dl_ref[0])
    lg_f, lg_b = lg[0:1, :], lg[1:2, :]
    cos, sin = cos_ref[...], sin_ref[...]
    q = q_ref[0]
    k = k_ref[0] * (RET_KEY_DIM ** -0.5)
    half = RET_KEY_DIM // 2
    qb_ref[...] = (q * cos + pltpu.roll(q, half, 1) * sin).astype(BF16)
    kr = k * cos + pltpu.roll(k, half, 1) * sin
    kb_ref[...] = kr.astype(BF16)

    pos = lax.broadcasted_iota(jnp.int32, (RC, 1), 0).astype(F32)
    kdec_f = jnp.exp(lg_f * (RC - 1.0 - pos))
    kdec_b = jnp.exp(lg_b * pos)
    qdec_f = jnp.exp(lg_f * (pos + 1.0))
    qdec_b = jnp.exp(lg_b * (RC - pos))
    c_f = jnp.exp(lg_f * RC)
    c_b = jnp.exp(lg_b * RC)
    ii = lax.broadcasted_iota(jnp.int32, (RC, RC), 0)
    jj = lax.broadcasted_iota(jnp.int32, (RC, RC), 1)
    dist = (ii - jj).astype(F32)
    dmat = (jnp.where(ii >= jj, jnp.exp(lg_f * dist), 0.0)
            + jnp.where(ii <= jj, jnp.exp(-lg_b * dist), 0.0))

    tn_dims = (((0,), (0,)), ((), ()))
    for n in range(RET_NCHUNK):
        rows = slice(n * RC, (n + 1) * RC)
        kn = kr[rows]
        vn = v_ref[0, rows, :].astype(BF16)
        kv_ref[0, n] = lax.dot_general((kn * kdec_f).astype(BF16), vn, tn_dims,
                                       preferred_element_type=F32)
        kv_ref[1, n] = lax.dot_general((kn * kdec_b).astype(BF16), vn, tn_dims,
                                       preferred_element_type=F32)

    sf = jnp.zeros((RET_KEY_DIM, RET_VALUE_DIM), F32)
    for n in range(RET_NCHUNK):
        s_ref[n, :, :RET_VALUE_DIM] = sf.astype(BF16)
        sf = sf * c_f + kv_ref[0, n]
    s_ref[0, :, RET_VALUE_DIM:] = jnp.zeros((RET_KEY_DIM, RET_VALUE_DIM), BF16)
    sb = kv_ref[1, 0]
    for n in range(RET_NCHUNK - 1, 0, -1):
        s_ref[n, :, RET_VALUE_DIM:] = sb.astype(BF16)
        sb = sb * c_b + kv_ref[1, n]

    nt_dims = (((1,), (1,)), ((), ()))
    for n in range(RET_NCHUNK):
        rows = slice(n * RC, (n + 1) * RC)
        qn = qb_ref[rows, :]
        s = lax.dot_general(qn, kb_ref[rows, :], nt_dims, preferred_element_type=F32)
        o = _dot((s * dmat).astype(BF16), v_ref[0, rows, :].astype(BF16))
        cross = _dot(qn, s_ref[n])
        o = o + cross[:, :RET_VALUE_DIM] * qdec_f + cross[:, RET_VALUE_DIM:] * qdec_b
        y = o * lax.rsqrt(jnp.mean(o * o, axis=-1, keepdims=True) + EPS)
        o_ref[0, rows, :] = (y * _silu(g_ref[0, rows, :])).astype(BF16)


def _ret_rope_tables():
    inv = ROPE_BASE ** (-jnp.linspace(0.0, 1.0, RET_KEY_DIM // 2, dtype=F32))
    ang = jnp.arange(SEQ, dtype=F32)[:, None] * inv
    ang = jnp.concatenate([jnp.zeros((CTX_LEN, RET_KEY_DIM // 2), F32), ang], axis=0)
    cos = jnp.concatenate([jnp.cos(ang)] * 2, axis=-1)
    sin = jnp.concatenate([-jnp.sin(ang), jnp.sin(ang)], axis=-1)
    return cos, sin


def _ret_mixer(p, decay_logit):
    bsz = p.shape[0]
    cos, sin = _ret_rope_tables()
    dl = decay_logit.T.reshape(RET_HEADS, 2, 1)
    kq = RET_HEADS
    kv = (2 * RET_HEADS * RET_KEY_DIM) // RET_VALUE_DIM
    return pl.pallas_call(
        _ret_kernel,
        grid=(bsz, RET_HEADS),
        in_specs=[pl.BlockSpec((1, LT, RET_KEY_DIM), lambda b, h: (b, 0, h)),
                  pl.BlockSpec((1, LT, RET_KEY_DIM), lambda b, h: (b, 0, kq + h)),
                  pl.BlockSpec((1, LT, RET_VALUE_DIM), lambda b, h: (b, 0, kv + h)),
                  pl.BlockSpec((1, LT, RET_VALUE_DIM), lambda b, h: (b, 0, kv + RET_HEADS + h)),
                  pl.BlockSpec((LT, RET_KEY_DIM), lambda b, h: (0, 0)),
                  pl.BlockSpec((LT, RET_KEY_DIM), lambda b, h: (0, 0)),
                  pl.BlockSpec((1, 2, 1), lambda b, h: (h, 0, 0))],
        out_specs=pl.BlockSpec((1, LT, RET_VALUE_DIM), lambda b, h: (b, 0, h)),
        out_shape=jax.ShapeDtypeStruct((bsz, LT, RET_V_WIDTH), BF16),
        scratch_shapes=[pltpu.VMEM((LT, RET_KEY_DIM), BF16),
                        pltpu.VMEM((LT, RET_KEY_DIM), BF16),
                        pltpu.VMEM((2, RET_NCHUNK, RET_KEY_DIM, RET_VALUE_DIM), F32),
                        pltpu.VMEM((RET_NCHUNK, RET_KEY_DIM, 2 * RET_VALUE_DIM), BF16)],
        compiler_params=_cparams(("parallel", "parallel")),
        name="retention",
    )(p, p, p, p, cos, sin, dl)


TQ = CTX_LEN
DIFF_NQ = LT // TQ
DIFF_Q_SPLIT = 4


def _run_in_lockstep(gens):
    while gens:
        alive = []
        for g in gens:
            try:
                next(g)
                alive.append(g)
            except StopIteration:
                pass
        gens = alive


def _axial_rope_tile(x, cos, sin_lo, sin_hi):
    q = DIFF_QK_DIM // 4
    return x * cos + pltpu.roll(x, 128 - q, 1) * sin_lo + pltpu.roll(x, q, 1) * sin_hi


def _diff_kernel(q_ref, k_ref, v_ref, cos_ref, slo_ref, shi_ref, lam_ref, w_ref, o_ref,
                 kb_ref, vb_ref, *, lambda_init):
    i = pl.program_id(2)

    @pl.when(i == 0)
    def _():
        kb_ref[...] = _axial_rope_tile(k_ref[0], cos_ref[...], slo_ref[...], shi_ref[...]).astype(BF16)
        ones_col = (lax.broadcasted_iota(jnp.int32, (LT, DIFF_V_DIM), 1) == 0).astype(BF16)
        vb_ref[...] = jnp.concatenate([v_ref[0].astype(BF16), ones_col], axis=1)

    lam = lam_ref[...]
    lam_full = (jnp.exp(jnp.sum(lam[0:1] * lam[1:2], axis=-1, keepdims=True))
                - jnp.exp(jnp.sum(lam[2:3] * lam[3:4], axis=-1, keepdims=True)) + lambda_init)
    r0 = pl.multiple_of(i * TQ, TQ)
    q = q_ref[0] * (DIFF_QK_DIM ** -0.5 * math.log2(math.e))
    qr = _axial_rope_tile(q, cos_ref[pl.ds(r0, TQ), :], slo_ref[pl.ds(r0, TQ), :], shi_ref[pl.ds(r0, TQ), :])
    first_map = lax.broadcasted_iota(jnp.int32, (TQ, DIFF_V_DIM), 1) < DIFF_QK_DIM
    q_maps = (jnp.where(first_map, qr, 0.0).astype(BF16), jnp.where(first_map, 0.0, qr).astype(BF16))
    out_scale = w_ref[...] * (1.0 - lambda_init)

    def attend_rows(r, n_keys):
        qs = jnp.concatenate([q_maps[0][r], q_maps[1][r]], axis=0)
        nr = qs.shape[0] // 2
        s = lax.dot_general(qs, kb_ref[:n_keys, :], (((1,), (1,)), ((), ())),
                            preferred_element_type=F32)
        yield
        e = jnp.exp2(s - jnp.max(s, axis=-1, keepdims=True)).astype(BF16)
        ob = _dot(e, vb_ref[:n_keys, :])
        yield
        ob = ob[:, :DIFF_V_DIM] / ob[:, DIFF_V_DIM:DIFF_V_DIM + 1]
        o = ob[:nr] - lam_full * ob[nr:]
        y = o * lax.rsqrt(jnp.mean(o * o, axis=-1, keepdims=True) + EPS) * out_scale
        o_ref[0, r, :] = y.astype(BF16)

    def attend(n_keys, n_split):
        step = TQ // n_split
        _run_in_lockstep([attend_rows(slice(j * step, (j + 1) * step), n_keys) for j in range(n_split)])

    @pl.when(i == 0)
    def _():
        attend(CTX_LEN, 1)

    @pl.when(i > 0)
    def _():
        attend(LT, DIFF_Q_SPLIT)


def _diff_rope_tables():
    rows = SEQ // GRID_W
    r, cidx = jnp.meshgrid(jnp.arange(rows), jnp.arange(GRID_W), indexing='ij')
    axis_dim = DIFF_QK_DIM // 2
    inv = ROPE_BASE ** (-jnp.arange(0, axis_dim, 2, dtype=F32) / axis_dim)
    ang_r = r.reshape(-1).astype(F32)[:, None] * inv
    ang_c = cidx.reshape(-1).astype(F32)[:, None] * inv
    zero = jnp.zeros_like(ang_r)

    def table(fn_lo, fn_hi):
        one = jnp.concatenate([fn_lo(ang_r), fn_hi(ang_r), fn_lo(ang_c), fn_hi(ang_c)], axis=-1)
        return jnp.concatenate([one, one], axis=-1)

    cos = table(jnp.cos, jnp.cos)
    sin_lo = table(lambda a: -jnp.sin(a), lambda a: zero)
    sin_hi = table(lambda a: zero, jnp.sin)
    ident = [jnp.ones((CTX_LEN, DIFF_V_DIM), F32), jnp.zeros((CTX_LEN, DIFF_V_DIM), F32),
             jnp.zeros((CTX_LEN, DIFF_V_DIM), F32)]
    return [jnp.concatenate([c, t], axis=0) for c, t in zip(ident, (cos, sin_lo, sin_hi))]


def _diff_mixer(p, lam, subln_w, layer_idx):
    bsz = p.shape[0]
    lambda_init = 0.8 - 0.6 * math.exp(-0.3 * layer_idx)
    cos, sin_lo, sin_hi = _diff_rope_tables()
    q0 = 4 * DN_WIDTH // DIFF_V_DIM
    k0 = q0 + DIFF_HEADS
    v0 = k0 + DIFF_HEADS
    tab = pl.BlockSpec((LT, DIFF_V_DIM), lambda b, h, i: (0, 0))
    return pl.pallas_call(
        functools.partial(_diff_kernel, lambda_init=lambda_init),
        grid=(bsz, DIFF_HEADS, DIFF_NQ),
        in_specs=[pl.BlockSpec((1, TQ, DIFF_V_DIM), lambda b, h, i: (b, i, q0 + h)),
                  pl.BlockSpec((1, LT, DIFF_V_DIM), lambda b, h, i: (b, 0, k0 + h)),
                  pl.BlockSpec((1, LT, DIFF_V_DIM), lambda b, h, i: (b, 0, v0 + h)),
                  tab, tab, tab,
                  pl.BlockSpec((4, DIFF_QK_DIM), lambda b, h, i: (0, 0)),
                  pl.BlockSpec((1, DIFF_V_DIM), lambda b, h, i: (0, 0))],
        out_specs=pl.BlockSpec((1, TQ, DIFF_V_DIM), lambda b, h, i: (b, i, h)),
        out_shape=jax.ShapeDtypeStruct((bsz, LT, DIFF_WIDTH), BF16),
        scratch_shapes=[pltpu.VMEM((LT, DIFF_V_DIM), BF16), pltpu.VMEM((LT, 2 * DIFF_V_DIM), BF16)],
        compiler_params=_cparams(("parallel", "parallel", "arbitrary")),
        name="diff_attention",
    )(p, p, p, cos, sin_lo, sin_hi, lam, subln_w.reshape(1, DIFF_V_DIM))


DC = DN_CHUNK
DN_NCHUNK = LT // DC
DN_CTX_CHUNKS = CTX_LEN // DC


DN_PREP_GROUP = 6


def _split2(x):
    hi = x.astype(BF16)
    return hi, (x - hi.astype(F32)).astype(BF16)


def _mm_hilo(a, b):
    ah, al = _split2(a)
    bh, bl = _split2(b)
    return _dot(ah, bh) + (_dot(ah, bl) + _dot(al, bh))


def _mm_exact_lhs(a_bf16, x):
    h1 = x.astype(BF16)
    r1 = x - h1.astype(F32)
    h2 = r1.astype(BF16)
    h3 = (r1 - h2.astype(F32)).astype(BF16)
    return _dot(a_bf16, h1) + (_dot(a_bf16, h2) + _dot(a_bf16, h3))


def _softplus(x):
    return jnp.maximum(x, 0.0) + jnp.log1p(jnp.exp(-jnp.abs(x)))


def _dn_kernel(q_ref, k_ref, v_ref, z_ref, gt_ref, cq_ref, ck_ref, cv_ref, prm_ref, nw_ref, o_ref,
               qn_ref, kn_ref, vn_ref, la_ref, be_ref, wq_ref, u_ref, qk_ref, b_ref, p_ref, e_ref,
               sst_ref, s_ref):
    h = pl.program_id(1)
    tpos = lax.broadcasted_iota(jnp.int32, (LT, 1), 0)
    has_prev = jnp.logical_and(tpos != 0, tpos != CTX_LEN)
    has_next = jnp.logical_and(tpos != CTX_LEN - 1, tpos != LT - 1)

    def conv_silu(x_ref, c_ref):
        x = x_ref[0]
        w = c_ref[...]
        prev = jnp.where(has_prev, pltpu.roll(x, 1, 0), 0.0)
        nxt = jnp.where(has_next, pltpu.roll(x, LT - 1, 0), 0.0)
        return _silu(prev * w[0:1] + x * w[1:2] + nxt * w[2:3])

    def l2n(t):
        return t * lax.rsqrt(jnp.sum(t * t, axis=-1, keepdims=True) + EPS)

    qn_ref[...] = l2n(conv_silu(q_ref, cq_ref)) * (DN_HEAD_DIM ** -0.5)
    kn_ref[...] = l2n(conv_silu(k_ref, ck_ref))
    vn_ref[...] = conv_silu(v_ref, cv_ref)

    gates = gt_ref[0]
    glane = lax.broadcasted_iota(jnp.int32, gates.shape, 1)
    log_a_all = -jnp.exp(prm_ref[0:1, :]) * _softplus(gates + prm_ref[1:2, :])
    beta_all = 1.0 / (1.0 + jnp.exp(-gates))
    for d in range(2):
        log_a = jnp.sum(jnp.where(glane == d * DN_HEADS + h, log_a_all, 0.0), axis=-1, keepdims=True)
        beta = jnp.sum(jnp.where(glane == (2 + d) * DN_HEADS + h, beta_all, 0.0), axis=-1, keepdims=True)
        la_ref[d] = jnp.broadcast_to(log_a, (LT, DN_HEAD_DIM))
        be_ref[d] = jnp.broadcast_to(beta, (LT, DN_HEAD_DIM))

    row = lax.broadcasted_iota(jnp.int32, (DC, 2 * DC), 0)
    lane = lax.broadcasted_iota(jnp.int32, (DC, 2 * DC), 1)
    col = lane & (DC - 1)
    isf = lane < DC
    eye2 = row == col
    isb = jnp.logical_not(isf)
    incl = jnp.logical_or(jnp.logical_and(isf, row >= col), jnp.logical_and(isb, row <= col))
    strict = jnp.logical_or(jnp.logical_and(isf, row > col), jnp.logical_and(isb, row < col))
    r64 = lax.broadcasted_iota(jnp.int32, (DC, DC), 0)
    c64 = lax.broadcasted_iota(jnp.int32, (DC, DC), 1)
    cum_lhs = jnp.concatenate([(r64 >= c64).astype(BF16), jnp.ones((DC, DC), BF16)], axis=0)
    ones_lhs = jnp.ones((DC, DC), BF16)
    rr = lax.broadcasted_iota(jnp.int32, (2 * DC, 2 * DC), 0)
    cc = lax.broadcasted_iota(jnp.int32, (2 * DC, 2 * DC), 1)
    eye_bd = (rr == cc).astype(F32)
    same16 = (rr >> 4) == (cc >> 4)
    same32 = (rr >> 5) == (cc >> 5)
    same64 = (rr >> 6) == (cc >> 6)
    nt_dims = (((1,), (1,)), ((), ()))

    def prep_chunk(n):
        r0 = pl.multiple_of(n * DC, DC)
        rows = pl.ds(r0, DC)
        qn, kn, vn = qn_ref[rows, :], kn_ref[rows, :], vn_ref[rows, :]
        la_f, la_b = la_ref[0, rows, :], la_ref[1, rows, :]
        be_f, be_b = be_ref[0, rows, :], be_ref[1, rows, :]
        pt_f = _mm_exact_lhs(cum_lhs, la_f)
        pt_b = _mm_exact_lhs(cum_lhs, la_b)
        kb16 = kn.astype(BF16)
        kcat = jnp.concatenate([kb16, kb16], axis=0)
        kk2 = lax.dot_general(kb16, kcat, nt_dims, preferred_element_type=F32)
        qk2 = lax.dot_general(qn.astype(BF16), kcat, nt_dims, preferred_element_type=F32)
        yield
        g_f, tot_f = pt_f[:DC], pt_f[DC:]
        tot_b = pt_b[DC:]
        g_b = tot_b - pt_b[:DC] + la_b
        gg = jnp.where(isf, g_f, g_b)
        g_row = _mm_exact_lhs(ones_lhs, jnp.where(eye2, gg, 0.0))
        yield
        gam = jnp.where(incl, jnp.exp(gg - g_row), 0.0)
        a2 = jnp.where(strict, kk2 * jnp.where(isf, be_f, be_b) * gam, 0.0)
        abd = jnp.concatenate([jnp.where(isf, a2, 0.0), jnp.where(isf, 0.0, a2)], axis=0)
        dmat = jnp.where(same16, abd, 0.0)
        d2 = _mm_hilo(dmat, dmat)
        yield
        d4 = _mm_hilo(d2, d2)
        inv = _mm_hilo(eye_bd - dmat, eye_bd + d2)
        yield
        d8 = _mm_hilo(d4, d4)
        inv = _mm_hilo(inv, eye_bd + d4)
        yield
        inv = _mm_hilo(inv, eye_bd + d8)
        yield
        e16 = jnp.where(jnp.logical_and(same32, jnp.logical_not(same16)), abd, 0.0)
        t = _mm_hilo(inv, e16)
        yield
        inv = inv - _mm_hilo(t, inv)
        yield
        e32 = jnp.where(jnp.logical_and(same64, jnp.logical_not(same32)), abd, 0.0)
        t = _mm_hilo(inv, e32)
        yield
        inv = inv - _mm_hilo(t, inv)
        yield
        eg_f, eg_b = jnp.exp(g_f), jnp.exp(g_b)
        rhs = jnp.concatenate([jnp.concatenate([vn * be_f, kn * be_f * eg_f], axis=1),
                               jnp.concatenate([vn * be_b, kn * be_b * eg_b], axis=1)], axis=0)
        uw = _mm_hilo(inv, rhs)
        yield
        uw16 = uw.astype(BF16)
        kdt_f = jnp.transpose(kn * jnp.exp(tot_f - g_f)).astype(BF16)
        kdt_b = jnp.transpose(kn * jnp.exp(tot_b - g_b)).astype(BF16)
        bp_f = _dot(kdt_f, uw16[:DC])
        bp_b = _dot(kdt_b, uw16[DC:])
        yield
        u_ref[n, 0] = uw[:DC, :DN_HEAD_DIM]
        u_ref[n, 1] = uw[DC:, :DN_HEAD_DIM]
        wq_ref[n, 0] = jnp.concatenate([uw[:DC, DN_HEAD_DIM:], qn * eg_f], axis=0).astype(BF16)
        wq_ref[n, 1] = jnp.concatenate([uw[DC:, DN_HEAD_DIM:], qn * eg_b], axis=0).astype(BF16)
        qk_ref[n] = (qk2 * gam).astype(BF16)
        b_ref[n, 0] = bp_f[:, :DN_HEAD_DIM]
        b_ref[n, 1] = bp_b[:, :DN_HEAD_DIM]
        p_ref[n, 0] = bp_f[:, DN_HEAD_DIM:].astype(BF16)
        p_ref[n, 1] = bp_b[:, DN_HEAD_DIM:].astype(BF16)
        e_ref[n, 0] = jnp.exp(tot_f[0:8])
        e_ref[n, 1] = jnp.exp(tot_b[0:8])

    def prep(i, carry):
        _run_in_lockstep([prep_chunk(i * DN_PREP_GROUP + j) for j in range(DN_PREP_GROUP)])
        return carry

    lax.fori_loop(0, DN_NCHUNK // DN_PREP_GROUP, prep, 0)

    s_ref[...] = jnp.zeros(s_ref.shape, F32)

    def scan_dir(d, n):
        state = s_ref[d]
        s16 = state.astype(BF16)
        sst_ref[n, d] = s16
        ps = _dot(p_ref[n, d], s16)
        yield
        s_ref[d] = state * e_ref[n, d][0:1, :] - ps + b_ref[n, d]

    def scan_step(s, carry):
        nb = jnp.where(s < DN_CTX_CHUNKS, DN_CTX_CHUNKS - 1 - s, DN_NCHUNK - 1 + DN_CTX_CHUNKS - s)
        _run_in_lockstep([scan_dir(0, s), scan_dir(1, nb)])
        return carry

    lax.fori_loop(0, DN_NCHUNK, scan_step, 0)

    def finish_chunk(n):
        rows = pl.ds(pl.multiple_of(n * DC, DC), DC)
        sw_f = _dot(wq_ref[n, 0], sst_ref[n, 0])
        sw_b = _dot(wq_ref[n, 1], sst_ref[n, 1])
        yield
        vcat = jnp.concatenate([u_ref[n, 0] - sw_f[:DC], u_ref[n, 1] - sw_b[:DC]], axis=0).astype(BF16)
        intra = _dot(qk_ref[n], vcat)
        yield
        o = sw_f[DC:] + sw_b[DC:] + intra
        y = o * lax.rsqrt(jnp.mean(o * o, axis=-1, keepdims=True) + EPS) * nw_ref[...]
        o_ref[0, rows, :] = (y * _silu(z_ref[0, rows, :])).astype(BF16)

    def finish(i, carry):
        _run_in_lockstep([finish_chunk(i * DN_PREP_GROUP + j) for j in range(DN_PREP_GROUP)])
        return carry

    lax.fori_loop(0, DN_NCHUNK // DN_PREP_GROUP, finish, 0)


def _dn_mixer(p, conv_w, a_log, dt_bias, norm_w):
    bsz = p.shape[0]
    hd = DN_HEAD_DIM
    prm = jnp.pad(jnp.stack([a_log.reshape(-1), dt_bias.reshape(-1)]), ((0, 0), (0, hd - 2 * DN_HEADS)))
    gate_blk = AB_MAIN_COLS // hd
    seq = lambda off: pl.BlockSpec((1, LT, hd), lambda b, h: (b, 0, off + h))
    cw = lambda off: pl.BlockSpec((CONV_WIDTH, hd), lambda b, h: (0, off + h))
    return pl.pallas_call(
        _dn_kernel,
        grid=(bsz, DN_HEADS),
        in_specs=[seq(0), seq(DN_HEADS), seq(2 * DN_HEADS), seq(3 * DN_HEADS),
                  pl.BlockSpec((1, LT, hd), lambda b, h: (b, 0, gate_blk)),
                  cw(0), cw(DN_HEADS), cw(2 * DN_HEADS),
                  pl.BlockSpec((2, hd), lambda b, h: (0, 0)),
                  pl.BlockSpec((1, hd), lambda b, h: (0, 0))],
        out_specs=pl.BlockSpec((1, LT, hd), lambda b, h: (b, 0, h)),
        out_shape=jax.ShapeDtypeStruct((bsz, LT, DN_WIDTH), BF16),
        scratch_shapes=[pltpu.VMEM((LT, hd), F32), pltpu.VMEM((LT, hd), F32), pltpu.VMEM((LT, hd), F32),
                        pltpu.VMEM((2, LT, hd), F32), pltpu.VMEM((2, LT, hd), F32),
                        pltpu.VMEM((DN_NCHUNK, 2, 2 * DC, hd), BF16),
                        pltpu.VMEM((DN_NCHUNK, 2, DC, hd), F32),
                        pltpu.VMEM((DN_NCHUNK, DC, 2 * DC), BF16),
                        pltpu.VMEM((DN_NCHUNK, 2, hd, hd), F32),
                        pltpu.VMEM((DN_NCHUNK, 2, hd, hd), BF16),
                        pltpu.VMEM((DN_NCHUNK, 2, 8, hd), F32),
                        pltpu.VMEM((DN_NCHUNK, 2, hd, hd), BF16),
                        pltpu.VMEM((2, hd, hd), F32)],
        compiler_params=_cparams(("parallel", "parallel")),
        name="gated_deltanet",
    )(p, p, p, p, p, conv_w, conv_w, conv_w, prm, norm_w.reshape(1, hd))


def _rms_norm(x, w=None):
    y = x * lax.rsqrt(jnp.mean(x * x, axis=-1, keepdims=True) + EPS)
    return y if w is None else y * w


def _l2norm(t):
    return t * lax.rsqrt(jnp.sum(t * t, axis=-1, keepdims=True) + EPS)


def _to_heads(t, n):
    B, L, _ = t.shape
    return t.reshape(B, L, n, -1).transpose(0, 2, 1, 3)


def _rotate_half(x):
    x1, x2 = jnp.split(x, 2, axis=-1)
    return jnp.concatenate([-x2, x1], axis=-1)


def _rope(x, ang):
    cos = jnp.concatenate([jnp.cos(ang)] * 2, axis=-1)
    sin = jnp.concatenate([jnp.sin(ang)] * 2, axis=-1)
    return x * cos + _rotate_half(x) * sin


def _axial_angles(L):
    rows = L // GRID_W
    r, cidx = jnp.meshgrid(jnp.arange(rows), jnp.arange(GRID_W), indexing='ij')
    axis_dim = DIFF_QK_DIM // 2
    inv = ROPE_BASE ** (-jnp.arange(0, axis_dim, 2, dtype=F32) / axis_dim)
    return (r.reshape(-1).astype(F32)[:, None] * inv, cidx.reshape(-1).astype(F32)[:, None] * inv)


def _axial_rope(x, ang_r, ang_c):
    xr, xc = jnp.split(x, 2, axis=-1)
    return jnp.concatenate([_rope(xr, ang_r), _rope(xc, ang_c)], axis=-1)


def _short_conv(u, w):
    k = w.shape[0]
    return lax.conv_general_dilated(u, w.astype(u.dtype)[:, None, :], window_strides=(1,),
                                    padding=[(k // 2, k // 2)],
                                    dimension_numbers=('NWC', 'WIO', 'NWC'),
                                    feature_group_count=u.shape[-1])


def _two_pass(fn, ctx_seqs, lat_seqs, S0):
    outs_c, outs_l = [], []
    for d in range(2):
        f = (lambda t: jnp.flip(t, 2)) if d == 1 else (lambda t: t)
        oc, S = fn(d, [f(t) for t in ctx_seqs], S0)
        ol, _ = fn(d, [f(t) for t in lat_seqs], S)
        outs_c.append(f(oc))
        outs_l.append(f(ol))
    return outs_c[0] + outs_c[1], outs_l[0] + outs_l[1]


def _gated_delta_chunked(q, k, v, log_a, beta, S0):
    B, H, L, _ = q.shape
    dv = v.shape[-1]
    C = DN_CHUNK
    n = L // C
    q, k, v = (t.reshape(B, H, n, C, t.shape[-1]) for t in (q, k, v))
    log_a, beta = (t.reshape(B, H, n, C) for t in (log_a, beta))
    g = jnp.cumsum(log_a, axis=-1)
    tril = jnp.tril(jnp.ones((C, C), bool))
    strict = jnp.tril(jnp.ones((C, C), bool), -1)
    gamma = jnp.exp(jnp.where(tril, g[..., :, None] - g[..., None, :], -jnp.inf))
    kb = k * beta[..., None]
    lower = jnp.where(strict, jnp.einsum('bhnid,bhnjd->bhnij', kb, k) * gamma, 0.0)
    tmat = lower + jnp.eye(C, dtype=lower.dtype)
    solve = lambda rhs: lax.linalg.triangular_solve(tmat, rhs, left_side=True, lower=True,
                                                    unit_diagonal=True)
    u = solve(v * beta[..., None])
    w = solve(kb * jnp.exp(g)[..., None])
    qk = jnp.einsum('bhnid,bhnjd->bhnij', q, k) * gamma

    def step(S, inp):
        q_i, k_i, u_i, w_i, g_i, qk_i = inp
        v_new = u_i - jnp.einsum('bhcd,bhde->bhce', w_i, S)
        o = (jnp.einsum('bhcd,bhde->bhce', q_i * jnp.exp(g_i)[..., None], S)
             + jnp.einsum('bhij,bhje->bhie', qk_i, v_new))
        g_last = g_i[..., -1:]
        S = (S * jnp.exp(g_last)[..., None]
             + jnp.einsum('bhcd,bhce->bhde', k_i * jnp.exp(g_last - g_i)[..., None], v_new))
        return S, o

    xs = tuple(jnp.moveaxis(t, 2, 0) for t in (q, k, u, w, g, qk))
    S, o = lax.scan(step, S0, xs)
    return jnp.moveaxis(o, 0, 2).reshape(B, H, L, dv), S


def _retention_chunked(q, k, v, log_gamma, S0):
    B, H, L, _ = q.shape
    dv = v.shape[-1]
    C = RET_CHUNK
    n = L // C
    q, k, v = (t.reshape(B, H, n, C, t.shape[-1]) for t in (q, k, v))
    pos = jnp.arange(C, dtype=F32)
    lg = log_gamma[:, None]
    tril = jnp.tril(jnp.ones((C, C), bool))
    dec = jnp.exp(jnp.where(tril, (pos[:, None] - pos[None, :]) * lg[:, :, None], -jnp.inf))
    o_inner = jnp.einsum('bhnij,bhnje->bhnie',
                         jnp.einsum('bhnid,bhnjd->bhnij', q, k) * dec[:, None], v)
    q_dec = jnp.exp(lg * (pos + 1))[:, :, None]
    k_dec = jnp.exp(lg * (C - 1 - pos))[:, :, None]
    c_dec = jnp.exp(log_gamma * C)[:, None, None]

    def step(S, inp):
        q_i, k_i, v_i, o_i = inp
        o = o_i + jnp.einsum('bhcd,bhde->bhce', q_i, S) * q_dec
        S = S * c_dec + jnp.einsum('bhcd,bhce->bhde', k_i * k_dec, v_i)
        return S, o

    xs = tuple(jnp.moveaxis(t, 2, 0) for t in (q, k, v, o_inner))
    S, o = lax.scan(step, S0, xs)
    return jnp.moveaxis(o, 0, 2).reshape(B, H, L, dv), S


def _diff_attend(q, k, v, lam_full):
    s = jnp.einsum('bhmqd,bhmkd->bhmqk', q, k) * DIFF_QK_DIM ** -0.5
    p = jax.nn.softmax(s, axis=-1)
    return jnp.einsum('bhqk,bhkd->bhqd', p[:, :, 0] - lam_full * p[:, :, 1], v)


def _ab_mixer_jax(pc, pl_, conv_w, A_log, dt_bias, dn_norm_w, lam, subln_w, layer_idx):
    B, L, _ = pl_[0].shape

    def dn_seqs(p):
        q, k, v, _, a, b = p[:6]
        qkv = jax.nn.silu(_short_conv(jnp.concatenate([q, k, v], axis=-1), conv_w))
        q, k, v = (_to_heads(t, DN_HEADS) for t in jnp.split(qkv, 3, axis=-1))
        q = _l2norm(q) * DN_HEAD_DIM ** -0.5
        k = _l2norm(k)
        Bs, Ls, _ = a.shape
        a = a.reshape(Bs, Ls, 2, DN_HEADS)
        b = b.reshape(Bs, Ls, 2, DN_HEADS)
        log_a = (-jnp.exp(A_log) * jax.nn.softplus(a + dt_bias)).transpose(0, 3, 1, 2)
        beta = jax.nn.sigmoid(b).transpose(0, 3, 1, 2)
        return [q, k, v, log_a, beta]

    S0 = jnp.zeros((B, DN_HEADS, DN_HEAD_DIM, DN_HEAD_DIM), F32)
    dn_fn = lambda d, s, S: _gated_delta_chunked(s[0], s[1], s[2], s[3][..., d], s[4][..., d], S)
    dn_c, dn_l = _two_pass(dn_fn, dn_seqs(pc), dn_seqs(pl_), S0)

    def dn_merge(o, z):
        Bs, H, Ls, dv = o.shape
        y = _rms_norm(o.transpose(0, 2, 1, 3), dn_norm_w) * jax.nn.silu(z).reshape(Bs, Ls, H, dv)
        return y.reshape(Bs, Ls, H * dv)

    lambda_init = 0.8 - 0.6 * math.exp(-0.3 * layer_idx)
    lam_full = jnp.exp(jnp.sum(lam[0] * lam[1])) - jnp.exp(jnp.sum(lam[2] * lam[3])) + lambda_init

    def diff_qkv(p):
        q, k, v = p[6:]
        Bs, Ls, _ = q.shape
        q, k = (t.reshape(Bs, Ls, DIFF_HEADS, 2, DIFF_QK_DIM).transpose(0, 2, 3, 1, 4) for t in (q, k))
        return q, k, _to_heads(v, DIFF_HEADS)

    def diff_merge(o):
        Bs, H, Ls, dv = o.shape
        return (_rms_norm(o, subln_w) * (1 - lambda_init)).transpose(0, 2, 1, 3).reshape(Bs, Ls, H * dv)

    qc, kc, vc = diff_qkv(pc)
    ql, kl, vl = diff_qkv(pl_)
    ang_r, ang_c = _axial_angles(L)
    ql, kl = _axial_rope(ql, ang_r, ang_c), _axial_rope(kl, ang_r, ang_c)
    k_all = jnp.concatenate([kc, kl], axis=3)
    v_all = jnp.concatenate([vc, vl], axis=2)
    nb = L // Q_BLOCK
    qb = jnp.moveaxis(ql.reshape(B, DIFF_HEADS, 2, nb, Q_BLOCK, DIFF_QK_DIM), 3, 0)
    ol = lax.map(lambda qi: _diff_attend(qi, k_all, v_all, lam_full), qb)
    ol = jnp.moveaxis(ol, 0, 2).reshape(B, DIFF_HEADS, L, DIFF_V_DIM)
    oc = _diff_attend(qc, kc, vc, lam_full)
    y_l = (dn_merge(dn_l, pl_[3]), diff_merge(ol))
    y_c = (dn_merge(dn_c, pc[3]), diff_merge(oc))
    return y_c, y_l


def _ret_mixer_jax(pc, pl_, decay_logit):
    def stream(p, rotate):
        q, k, v, g = p
        q, k, v = (_to_heads(t, RET_HEADS) for t in (q, k, v))
        if rotate:
            inv = ROPE_BASE ** (-jnp.linspace(0.0, 1.0, RET_KEY_DIM // 2, dtype=F32))
            ang = jnp.arange(q.shape[2], dtype=F32)[:, None] * inv
            q, k = _rope(q, ang), _rope(k, ang)
        return [q, k * RET_KEY_DIM ** -0.5, v], g

    sc, gc = stream(pc, False)
    sl, gl = stream(pl_, True)
    log_gamma = jax.nn.log_sigmoid(decay_logit)
    S0 = jnp.zeros((pl_[0].shape[0], RET_HEADS, RET_KEY_DIM, RET_VALUE_DIM), F32)
    ret_fn = lambda d, s, S: _retention_chunked(s[0], s[1], s[2], log_gamma[d], S)
    oc, ol = _two_pass(ret_fn, sc, sl, S0)

    def merge(o, g):
        Bs, H, Ls, dv = o.shape
        return _rms_norm(o.transpose(0, 2, 1, 3)).reshape(Bs, Ls, H * dv) * jax.nn.silu(g)

    return merge(oc, gc), merge(ol, gl)


AB_MAIN_COLS = 4 * DN_WIDTH + 3 * DIFF_WIDTH
AB_GATE_COLS = 256
AB_PROJ_TN = (AB_MAIN_COLS + AB_GATE_COLS) // 3
RET_PROJ_TN = RET_IN // 3


def _split_ab_weight(w):
    o_ab = 4 * DN_WIDTH
    n_ab = 4 * DN_HEADS
    main = jnp.concatenate([w[:, :o_ab], w[:, o_ab + n_ab:]], axis=1)
    gates = jnp.pad(w[:, o_ab:o_ab + n_ab], ((0, 0), (0, AB_GATE_COLS - n_ab)))
    return jnp.concatenate([main, gates], axis=1)


def kernel(x, c, ctx, c_ctx, ada_w, ada_b, norm_w, final_norm_w, ffn_w_in, ffn_w_out, ab_w_in,
           ab_conv_w, dn_A_log, dn_dt_bias, dn_norm_w, diff_lambda, diff_subln_w, ab_w_out,
           ret_w_in, ret_decay_logit, ret_w_out):
    B = x.shape[0]
    xs = jnp.concatenate([ctx, x], axis=1).reshape(B * LT, D_MODEL)
    cvec = jnp.concatenate([c, c_ctx[None], jnp.zeros((MOD_ROWS - B - 1, D_MODEL), F32)], axis=0)
    mods_all = _ada_mods(cvec, ada_w, ada_b)
    mods_all = mods_all.reshape(DEPTH, MOD_ROWS, N_MOD, D_MODEL).transpose(0, 2, 1, 3)
    ffn_w_in_b = ffn_w_in.astype(BF16)
    ffn_w_out_b = ffn_w_out.astype(BF16)
    for l in range(DEPTH):
        last = l == DEPTH - 1
        mods = mods_all[l]
        xs = _ffn(xs, norm_w[l, 0], mods, 0, ffn_w_in_b[l, 0], ffn_w_out_b[l, 0])
        i = l // 2
        if l % 2 == 0:
            w = _split_ab_weight(ab_w_in[i]).astype(BF16)
            p = _proj(xs, norm_w[l, 1], mods, w, AB_PROJ_TN).reshape(B, LT, -1)
            y_dn = _dn_mixer(p, ab_conv_w[i], dn_A_log[i], dn_dt_bias[i], dn_norm_w[i])
            y_df = _diff_mixer(p, diff_lambda[i], diff_subln_w[i], l)
            wo = ab_w_out[i].astype(BF16)
            xs = _out_proj(xs, mods, [y_dn.reshape(B * LT, DN_WIDTH), y_df.reshape(B * LT, DIFF_WIDTH)],
                           [wo[:DN_WIDTH], wo[DN_WIDTH:]])
        else:
            p = _proj(xs, norm_w[l, 1], mods, ret_w_in[i].astype(BF16), RET_PROJ_TN).reshape(B, LT, -1)
            y = _ret_mixer(p, ret_decay_logit[i])
            xs = _out_proj(xs, mods, [y.reshape(B * LT, RET_V_WIDTH)], [ret_w_out[i].astype(BF16)])
        xs = _ffn(xs, norm_w[l, 2], mods, 2, ffn_w_in_b[l, 1], ffn_w_out_b[l, 1],
                  final_w=final_norm_w if last else None)
    return xs.reshape(B, LT, D_MODEL)[:, CTX_LEN:]
```

```python
import functools
import math

import jax
import jax.numpy as jnp
import numpy as np
from jax import lax
from jax.experimental import pallas as pl
from jax.experimental.pallas import tpu as pltpu

D_MODEL = 1024
BATCH = 8
SEQ = 2048
DEPTH = 4
GRID_W = 64
CTX_LEN = 256
EPS = 1e-6
N_MOD = 9
DN_HEADS = 4
DN_HEAD_DIM = 128
DN_WIDTH = DN_HEADS * DN_HEAD_DIM
CONV_WIDTH = 3
DN_CHUNK = 64
DIFF_HEADS = 4
DIFF_QK_DIM = 64
DIFF_V_DIM = 2 * DIFF_QK_DIM
DIFF_WIDTH = DIFF_HEADS * DIFF_V_DIM
Q_BLOCK = 128
ROPE_BASE = 10000.0
RET_HEADS = 8
RET_KEY_DIM = D_MODEL // RET_HEADS
RET_VALUE_DIM = 2 * RET_KEY_DIM
RET_V_WIDTH = RET_HEADS * RET_VALUE_DIM
RET_CHUNK = 64
D_FF = 2816
AB_SIZES = [DN_WIDTH] * 4 + [2 * DN_HEADS] * 2 + [DIFF_WIDTH] * 3
AB_IN = sum(AB_SIZES)
RET_SIZES = [RET_HEADS * RET_KEY_DIM] * 2 + [RET_V_WIDTH] * 2
RET_IN = sum(RET_SIZES)

F32 = jnp.float32
BF16 = jnp.bfloat16

LT = CTX_LEN + SEQ
TM = LT // 2
TILES_PER_BATCH = LT // TM
MOD_ROWS = 16
CTX_ROW = BATCH
V7X_VMEM_LIMIT = 56 * 1024 * 1024
FF_CHUNK = 256


def _cparams(sem):
    return pltpu.CompilerParams(dimension_semantics=sem, vmem_limit_bytes=V7X_VMEM_LIMIT)


def _silu(x):
    return x * (1.0 / (1.0 + jnp.exp(-x)))


def _dot(a, b):
    return jnp.dot(a, b, preferred_element_type=F32)


def _run_in_lockstep(gens):
    while gens:
        alive = []
        for g in gens:
            try:
                next(g)
                alive.append(g)
            except StopIteration:
                pass
        gens = alive


def _ada_kernel(s_ref, w_ref, b_ref, o_ref):
    s = _silu(s_ref[...]).astype(BF16)
    o_ref[0] = _dot(s, w_ref[0].astype(BF16)) + b_ref[0]


def _ada_mods(cvec, ada_w, ada_b):
    tn = 1024
    n = N_MOD * D_MODEL
    return pl.pallas_call(
        _ada_kernel,
        grid=(DEPTH, n // tn),
        in_specs=[pl.BlockSpec((MOD_ROWS, D_MODEL), lambda l, j: (0, 0)),
                  pl.BlockSpec((1, D_MODEL, tn), lambda l, j: (l, 0, j)),
                  pl.BlockSpec((1, 1, tn), lambda l, j: (l, 0, j))],
        out_specs=pl.BlockSpec((1, MOD_ROWS, tn), lambda l, j: (l, 0, j)),
        out_shape=jax.ShapeDtypeStruct((DEPTH, MOD_ROWS, n), F32),
        compiler_params=_cparams(("parallel", "parallel")),
        name="ada_mods",
    )(cvec, ada_w, ada_b.reshape(DEPTH, 1, n))


def _ada_in_tile(x, g_ref, mod_ref, h_ref):
    i = pl.program_id(0)
    b = i // TILES_PER_BATCH
    y = x * lax.rsqrt(jnp.mean(x * x, axis=-1, keepdims=True) + EPS) * g_ref[...]
    h_ref[...] = (y * (1.0 + mod_ref[1, pl.ds(b, 1), :]) + mod_ref[0, pl.ds(b, 1), :]).astype(BF16)

    @pl.when(i % TILES_PER_BATCH == 0)
    def _():
        yc = y[:CTX_LEN]
        h_ref[:CTX_LEN, :] = (yc * (1.0 + mod_ref[1, CTX_ROW:CTX_ROW + 1, :])
                              + mod_ref[0, CTX_ROW:CTX_ROW + 1, :]).astype(BF16)


def _gated_residual_store(x_ref, upd, mod_ref, o_ref, gate_scale, final_w_ref):
    i = pl.program_id(0)
    b = i // TILES_PER_BATCH

    def fin(v):
        if final_w_ref is None:
            return v
        return v * lax.rsqrt(jnp.mean(v * v, axis=-1, keepdims=True) + EPS) * final_w_ref[...]

    o_ref[...] = fin(x_ref[...] + (gate_scale * mod_ref[2, pl.ds(b, 1), :]) * upd)

    @pl.when(i % TILES_PER_BATCH == 0)
    def _():
        o_ref[:CTX_LEN, :] = fin(x_ref[:CTX_LEN, :]
                                 + (gate_scale * mod_ref[2, CTX_ROW:CTX_ROW + 1, :]) * upd[:CTX_LEN])


def _mod_spec(layer, sub):
    return pl.BlockSpec((None, 3, MOD_ROWS, D_MODEL), lambda *idx: (layer, sub, 0, 0))


def _norm_spec(layer, sub):
    return pl.BlockSpec((None, None, 1, D_MODEL), lambda *idx: (layer, sub, 0, 0))


def _ffn_kernel(*refs, with_final):
    if with_final:
        x_ref, g_ref, mod_ref, win_ref, wout_ref, fw_ref, o_ref, h_ref, acc_ref = refs
    else:
        x_ref, g_ref, mod_ref, win_ref, wout_ref, o_ref, h_ref, acc_ref = refs
        fw_ref = None
    _ada_in_tile(x_ref[...], g_ref, mod_ref, h_ref)
    for f in range(D_FF // FF_CHUNK):
        h = h_ref[...]
        gg = _dot(h, win_ref[:, f * FF_CHUNK:(f + 1) * FF_CHUNK])
        uu = _dot(h, win_ref[:, D_FF + f * FF_CHUNK:D_FF + (f + 1) * FF_CHUNK])
        a = (_silu(gg) * uu).astype(BF16)
        part = _dot(a, wout_ref[f * FF_CHUNK:(f + 1) * FF_CHUNK, :])
        if f == 0:
            acc_ref[...] = part
        else:
            acc_ref[...] += part
    _gated_residual_store(x_ref, acc_ref[...], mod_ref, o_ref, 0.5, fw_ref)


def _ffn(x, norms, mods, layer, sub, w_in, w_out, which, final_w=None):
    t = x.shape[0]
    single = pl.Buffered(1)
    in_specs = [pl.BlockSpec((TM, D_MODEL), lambda i: (i, 0)),
                _norm_spec(layer, sub),
                _mod_spec(layer, sub),
                pl.BlockSpec((None, None, D_MODEL, 2 * D_FF), lambda i: (layer, which, 0, 0),
                             pipeline_mode=single),
                pl.BlockSpec((None, None, D_FF, D_MODEL), lambda i: (layer, which, 0, 0),
                             pipeline_mode=single)]
    args = [x, norms, mods, w_in, w_out]
    if final_w is not None:
        in_specs.append(pl.BlockSpec((1, D_MODEL), lambda i: (0, 0)))
        args.append(final_w.reshape(1, D_MODEL))
    return pl.pallas_call(
        functools.partial(_ffn_kernel, with_final=final_w is not None),
        grid=(t // TM,),
        in_specs=in_specs,
        out_specs=pl.BlockSpec((TM, D_MODEL), lambda i: (i, 0)),
        out_shape=jax.ShapeDtypeStruct((t, D_MODEL), F32),
        scratch_shapes=[pltpu.VMEM((TM, D_MODEL), BF16), pltpu.VMEM((TM, D_MODEL), F32)],
        compiler_params=_cparams(("parallel",)),
        name="ffn",
    )(*args)


def _proj_kernel(x_ref, g_ref, mod_ref, w_ref, o_ref, h_ref):
    @pl.when(pl.program_id(1) == 0)
    def _():
        _ada_in_tile(x_ref[...], g_ref, mod_ref, h_ref)

    o_ref[...] = _dot(h_ref[...], w_ref[...])


def _proj(x, norms, mods, layer, w, which, tn):
    t = x.shape[0]
    n = w.shape[2]
    return pl.pallas_call(
        _proj_kernel,
        grid=(t // TM, n // tn),
        in_specs=[pl.BlockSpec((TM, D_MODEL), lambda i, j: (i, 0)),
                  _norm_spec(layer, 1),
                  _mod_spec(layer, 1),
                  pl.BlockSpec((None, D_MODEL, tn), lambda i, j: (which, 0, j))],
        out_specs=pl.BlockSpec((TM, tn), lambda i, j: (i, j)),
        out_shape=jax.ShapeDtypeStruct((t, n), F32),
        scratch_shapes=[pltpu.VMEM((TM, D_MODEL), BF16)],
        compiler_params=_cparams(("parallel", "arbitrary")),
        name="mixer_in_proj",
    )(x, norms, mods, w)


def _out_proj_kernel(*refs, n_parts):
    x_ref, mod_ref = refs[0], refs[1]
    y_refs = refs[2:2 + n_parts]
    w_refs = refs[2 + n_parts:2 + 2 * n_parts]
    o_ref = refs[2 + 2 * n_parts]
    upd = None
    for y_ref, w_ref in zip(y_refs, w_refs):
        part = _dot(y_ref[...].astype(BF16), w_ref[...])
        upd = part if upd is None else upd + part
    _gated_residual_store(x_ref, upd, mod_ref, o_ref, 1.0, None)


def _out_proj(x, mods, layer, ys, w, which):
    t = x.shape[0]
    n_parts = len(ys)
    kp = ys[0].shape[1]
    in_specs = [pl.BlockSpec((TM, D_MODEL), lambda i: (i, 0)), _mod_spec(layer, 1)]
    in_specs += [pl.BlockSpec((TM, kp), lambda i: (i, 0)) for _ in ys]
    in_specs += [pl.BlockSpec((None, kp, D_MODEL), lambda i, p=p: (which, p, 0)) for p in range(n_parts)]
    return pl.pallas_call(
        functools.partial(_out_proj_kernel, n_parts=n_parts),
        grid=(t // TM,),
        in_specs=in_specs,
        out_specs=pl.BlockSpec((TM, D_MODEL), lambda i: (i, 0)),
        out_shape=jax.ShapeDtypeStruct((t, D_MODEL), F32),
        compiler_params=_cparams(("parallel",)),
        name="mixer_out_proj",
    )(x, mods, *ys, *([w] * n_parts))


RC = CTX_LEN
RET_NCHUNK = LT // RC
RET_OUT_GROUP = 1


def _log_sigmoid(x):
    return jnp.minimum(x, 0.0) - jnp.log1p(jnp.exp(-jnp.abs(x)))


def _ret_kernel(q_ref, k_ref, v_ref, g_ref, cos_ref, sin_ref, dl_ref, o_ref,
                qb_ref, kb_ref, kv_ref, s_ref):
    lg = _log_sigmoid(dl_ref[0])
    lg_f, lg_b = lg[0:1, :], lg[1:2, :]
    cos, sin = cos_ref[...], sin_ref[...]
    q = q_ref[0]
    k = k_ref[0] * (RET_KEY_DIM ** -0.5)
    half = RET_KEY_DIM // 2
    qb_ref[...] = (q * cos + pltpu.roll(q, half, 1) * sin).astype(BF16)
    kr = k * cos + pltpu.roll(k, half, 1) * sin
    kb_ref[...] = kr.astype(BF16)

    pos = lax.broadcasted_iota(jnp.int32, (RC, 1), 0).astype(F32)
    kdec_f = jnp.exp(lg_f * (RC - 1.0 - pos))
    kdec_b = jnp.exp(lg_b * pos)
    qdec_f = jnp.exp(lg_f * (pos + 1.0))
    qdec_b = jnp.exp(lg_b * (RC - pos))
    c_f = jnp.exp(lg_f * RC)
    c_b = jnp.exp(lg_b * RC)
    ii = lax.broadcasted_iota(jnp.int32, (RC, RC), 0)
    jj = lax.broadcasted_iota(jnp.int32, (RC, RC), 1)
    dist = (ii - jj).astype(F32)
    dmat = (jnp.where(ii >= jj, jnp.exp(lg_f * dist), 0.0)
            + jnp.where(ii <= jj, jnp.exp(-lg_b * dist), 0.0))

    tn_dims = (((0,), (0,)), ((), ()))
    for n in range(RET_NCHUNK):
        rows = slice(n * RC, (n + 1) * RC)
        kn = kr[rows]
        vn = v_ref[0, rows, :].astype(BF16)
        kv_ref[0, n] = lax.dot_general((kn * kdec_f).astype(BF16), vn, tn_dims,
                                       preferred_element_type=F32)
        kv_ref[1, n] = lax.dot_general((kn * kdec_b).astype(BF16), vn, tn_dims,
                                       preferred_element_type=F32)

    sf = jnp.zeros((RET_KEY_DIM, RET_VALUE_DIM), F32)
    for n in range(RET_NCHUNK):
        s_ref[n, :, :RET_VALUE_DIM] = sf.astype(BF16)
        sf = sf * c_f + kv_ref[0, n]
    s_ref[0, :, RET_VALUE_DIM:] = jnp.zeros((RET_KEY_DIM, RET_VALUE_DIM), BF16)
    sb = kv_ref[1, 0]
    for n in range(RET_NCHUNK - 1, 0, -1):
        s_ref[n, :, RET_VALUE_DIM:] = sb.astype(BF16)
        sb = sb * c_b + kv_ref[1, n]

    nt_dims = (((1,), (1,)), ((), ()))

    def out_chunk(n):
        rows = slice(n * RC, (n + 1) * RC)
        qn = qb_ref[rows, :]
        s = lax.dot_general(qn, kb_ref[rows, :], nt_dims, preferred_element_type=F32)
        cross = _dot(qn, s_ref[n])
        yield
        o = _dot((s * dmat).astype(BF16), v_ref[0, rows, :].astype(BF16))
        yield
        o = o + cross[:, :RET_VALUE_DIM] * qdec_f + cross[:, RET_VALUE_DIM:] * qdec_b
        y = o * lax.rsqrt(jnp.mean(o * o, axis=-1, keepdims=True) + EPS)
        o_ref[0, rows, :] = (y * _silu(g_ref[0, rows, :])).astype(BF16)

    for n0 in range(0, RET_NCHUNK, RET_OUT_GROUP):
        _run_in_lockstep([out_chunk(n) for n in range(n0, n0 + RET_OUT_GROUP)])


def _ret_rope_tables():
    inv = ROPE_BASE ** (-jnp.linspace(0.0, 1.0, RET_KEY_DIM // 2, dtype=F32))
    ang = jnp.arange(SEQ, dtype=F32)[:, None] * inv
    ang = jnp.concatenate([jnp.zeros((CTX_LEN, RET_KEY_DIM // 2), F32), ang], axis=0)
    cos = jnp.concatenate([jnp.cos(ang)] * 2, axis=-1)
    sin = jnp.concatenate([-jnp.sin(ang), jnp.sin(ang)], axis=-1)
    return cos, sin


def _ret_mixer(p, decay_logit):
    bsz = p.shape[0]
    cos, sin = _ret_rope_tables()
    dl = decay_logit.T.reshape(RET_HEADS, 2, 1)
    kq = RET_HEADS
    kv = (2 * RET_HEADS * RET_KEY_DIM) // RET_VALUE_DIM
    return pl.pallas_call(
        _ret_kernel,
        grid=(bsz, RET_HEADS),
        in_specs=[pl.BlockSpec((1, LT, RET_KEY_DIM), lambda b, h: (b, 0, h)),
                  pl.BlockSpec((1, LT, RET_KEY_DIM), lambda b, h: (b, 0, kq + h)),
                  pl.BlockSpec((1, LT, RET_VALUE_DIM), lambda b, h: (b, 0, kv + h)),
                  pl.BlockSpec((1, LT, RET_VALUE_DIM), lambda b, h: (b, 0, kv + RET_HEADS + h)),
                  pl.BlockSpec((LT, RET_KEY_DIM), lambda b, h: (0, 0)),
                  pl.BlockSpec((LT, RET_KEY_DIM), lambda b, h: (0, 0)),
                  pl.BlockSpec((1, 2, 1), lambda b, h: (h, 0, 0))],
        out_specs=pl.BlockSpec((1, LT, RET_VALUE_DIM), lambda b, h: (b, 0, h)),
        out_shape=jax.ShapeDtypeStruct((bsz, LT, RET_V_WIDTH), BF16),
        scratch_shapes=[pltpu.VMEM((LT, RET_KEY_DIM), BF16),
                        pltpu.VMEM((LT, RET_KEY_DIM), BF16),
                        pltpu.VMEM((2, RET_NCHUNK, RET_KEY_DIM, RET_VALUE_DIM), F32),
                        pltpu.VMEM((RET_NCHUNK, RET_KEY_DIM, 2 * RET_VALUE_DIM), BF16)],
        compiler_params=_cparams(("parallel", "parallel")),
        name="retention",
    )(p, p, p, p, cos, sin, dl)


TQ = CTX_LEN
DIFF_NQ = LT // TQ
DIFF_Q_SPLIT = 4


def _axial_rope_tile(x, cos, sin_lo, sin_hi):
    q = DIFF_QK_DIM // 4
    return x * cos + pltpu.roll(x, 128 - q, 1) * sin_lo + pltpu.roll(x, q, 1) * sin_hi


def _diff_kernel(q_ref, k_ref, v_ref, cos_ref, slo_ref, shi_ref, lam_ref, w_ref, o_ref,
                 kb_ref, vb_ref, *, lambda_init):
    i = pl.program_id(2)

    @pl.when(i == 0)
    def _():
        kb_ref[...] = _axial_rope_tile(k_ref[0], cos_ref[...], slo_ref[...], shi_ref[...]).astype(BF16)
        ones_col = (lax.broadcasted_iota(jnp.int32, (LT, DIFF_V_DIM), 1) == 0).astype(BF16)
        vb_ref[...] = jnp.concatenate([v_ref[0].astype(BF16), ones_col], axis=1)

    lam = lam_ref[...]
    lam_full = (jnp.exp(jnp.sum(lam[0:1] * lam[1:2], axis=-1, keepdims=True))
                - jnp.exp(jnp.sum(lam[2:3] * lam[3:4], axis=-1, keepdims=True)) + lambda_init)
    r0 = pl.multiple_of(i * TQ, TQ)
    q = q_ref[0] * (DIFF_QK_DIM ** -0.5 * math.log2(math.e))
    qr = _axial_rope_tile(q, cos_ref[pl.ds(r0, TQ), :], slo_ref[pl.ds(r0, TQ), :], shi_ref[pl.ds(r0, TQ), :])
    first_map = lax.broadcasted_iota(jnp.int32, (TQ, DIFF_V_DIM), 1) < DIFF_QK_DIM
    q_maps = (jnp.where(first_map, qr, 0.0).astype(BF16), jnp.where(first_map, 0.0, qr).astype(BF16))
    out_scale = w_ref[...] * (1.0 - lambda_init)

    def attend_rows(r, n_keys):
        qs = jnp.concatenate([q_maps[0][r], q_maps[1][r]], axis=0)
        nr = qs.shape[0] // 2
        s = lax.dot_general(qs, kb_ref[:n_keys, :], (((1,), (1,)), ((), ())),
                            preferred_element_type=F32)
        yield
        e = jnp.exp2(s - jnp.max(s, axis=-1, keepdims=True)).astype(BF16)
        ob = _dot(e, vb_ref[:n_keys, :])
        yield
        ob = ob[:, :DIFF_V_DIM] / ob[:, DIFF_V_DIM:DIFF_V_DIM + 1]
        o = ob[:nr] - lam_full * ob[nr:]
        y = o * lax.rsqrt(jnp.mean(o * o, axis=-1, keepdims=True) + EPS) * out_scale
        o_ref[0, r, :] = y.astype(BF16)

    def attend(n_keys, n_split):
        step = TQ // n_split
        _run_in_lockstep([attend_rows(slice(j * step, (j + 1) * step), n_keys) for j in range(n_split)])

    @pl.when(i == 0)
    def _():
        attend(CTX_LEN, 1)

    @pl.when(i > 0)
    def _():
        attend(LT, DIFF_Q_SPLIT)


def _diff_rope_tables():
    rows = SEQ // GRID_W
    r, cidx = jnp.meshgrid(jnp.arange(rows), jnp.arange(GRID_W), indexing='ij')
    axis_dim = DIFF_QK_DIM // 2
    inv = ROPE_BASE ** (-jnp.arange(0, axis_dim, 2, dtype=F32) / axis_dim)
    ang_r = r.reshape(-1).astype(F32)[:, None] * inv
    ang_c = cidx.reshape(-1).astype(F32)[:, None] * inv
    zero = jnp.zeros_like(ang_r)

    def table(fn_lo, fn_hi):
        one = jnp.concatenate([fn_lo(ang_r), fn_hi(ang_r), fn_lo(ang_c), fn_hi(ang_c)], axis=-1)
        return jnp.concatenate([one, one], axis=-1)

    cos = table(jnp.cos, jnp.cos)
    sin_lo = table(lambda a: -jnp.sin(a), lambda a: zero)
    sin_hi = table(lambda a: zero, jnp.sin)
    ident = [jnp.ones((CTX_LEN, DIFF_V_DIM), F32), jnp.zeros((CTX_LEN, DIFF_V_DIM), F32),
             jnp.zeros((CTX_LEN, DIFF_V_DIM), F32)]
    return [jnp.concatenate([c, t], axis=0) for c, t in zip(ident, (cos, sin_lo, sin_hi))]


def _diff_mixer(p, lam, subln_w, layer_idx):
    bsz = p.shape[0]
    lambda_init = 0.8 - 0.6 * math.exp(-0.3 * layer_idx)
    cos, sin_lo, sin_hi = _diff_rope_tables()
    q0 = 4 * DN_WIDTH // DIFF_V_DIM
    k0 = q0 + DIFF_HEADS
    v0 = k0 + DIFF_HEADS
    tab = pl.BlockSpec((LT, DIFF_V_DIM), lambda b, h, i: (0, 0))
    return pl.pallas_call(
        functools.partial(_diff_kernel, lambda_init=lambda_init),
        grid=(bsz, DIFF_HEADS, DIFF_NQ),
        in_specs=[pl.BlockSpec((1, TQ, DIFF_V_DIM), lambda b, h, i: (b, i, q0 + h)),
                  pl.BlockSpec((1, LT, DIFF_V_DIM), lambda b, h, i: (b, 0, k0 + h)),
                  pl.BlockSpec((1, LT, DIFF_V_DIM), lambda b, h, i: (b, 0, v0 + h)),
                  tab, tab, tab,
                  pl.BlockSpec((4, DIFF_QK_DIM), lambda b, h, i: (0, 0)),
                  pl.BlockSpec((1, DIFF_V_DIM), lambda b, h, i: (0, 0))],
        out_specs=pl.BlockSpec((1, TQ, DIFF_V_DIM), lambda b, h, i: (b, i, h)),
        out_shape=jax.ShapeDtypeStruct((bsz, LT, DIFF_WIDTH), BF16),
        scratch_shapes=[pltpu.VMEM((LT, DIFF_V_DIM), BF16), pltpu.VMEM((LT, 2 * DIFF_V_DIM), BF16)],
        compiler_params=_cparams(("parallel", "parallel", "arbitrary")),
        name="diff_attention",
    )(p, p, p, cos, sin_lo, sin_hi, lam, subln_w.reshape(1, DIFF_V_DIM))


DC = DN_CHUNK
DN_NCHUNK = LT // DC
DN_CTX_CHUNKS = CTX_LEN // DC


DN_PREP_GROUP = 9


def _split2(x):
    hi = x.astype(BF16)
    return hi, (x - hi.astype(F32)).astype(BF16)


def _mm_hilo(a, b):
    ah, al = _split2(a)
    bh, bl = _split2(b)
    n = b.shape[1]
    lhs = jnp.concatenate([ah, al], axis=1)
    rhs = jnp.concatenate([jnp.concatenate([bh, bl], axis=1),
                           jnp.concatenate([bh, jnp.zeros_like(bl)], axis=1)], axis=0)
    out = _dot(lhs, rhs)
    return out[:, :n] + out[:, n:]


def _stack3(a_bf16):
    return jnp.concatenate([a_bf16, a_bf16, a_bf16], axis=1)


def _mm_exact_lhs(a3_bf16, x):
    h1 = x.astype(BF16)
    r1 = x - h1.astype(F32)
    h2 = r1.astype(BF16)
    h3 = (r1 - h2.astype(F32)).astype(BF16)
    return _dot(a3_bf16, jnp.concatenate([h1, h2, h3], axis=0))


def _softplus(x):
    return jnp.maximum(x, 0.0) + jnp.log1p(jnp.exp(-jnp.abs(x)))


def _dn_kernel(q_ref, k_ref, v_ref, z_ref, gt_ref, cq_ref, ck_ref, cv_ref, prm_ref, nw_ref, o_ref,
               qn_ref, kn_ref, vn_ref, la_ref, be_ref, wq_ref, u_ref, qk_ref, b_ref, p_ref, e_ref,
               sst_ref, s_ref):
    h = pl.program_id(1)
    tpos = lax.broadcasted_iota(jnp.int32, (LT, 1), 0)
    has_prev = jnp.logical_and(tpos != 0, tpos != CTX_LEN)
    has_next = jnp.logical_and(tpos != CTX_LEN - 1, tpos != LT - 1)

    def conv_silu(x_ref, c_ref):
        x = x_ref[0]
        w = c_ref[...]
        prev = jnp.where(has_prev, pltpu.roll(x, 1, 0), 0.0)
        nxt = jnp.where(has_next, pltpu.roll(x, LT - 1, 0), 0.0)
        return _silu(prev * w[0:1] + x * w[1:2] + nxt * w[2:3])

    def l2n(t):
        return t * lax.rsqrt(jnp.sum(t * t, axis=-1, keepdims=True) + EPS)

    qn_ref[...] = l2n(conv_silu(q_ref, cq_ref)) * (DN_HEAD_DIM ** -0.5)
    kn_ref[...] = l2n(conv_silu(k_ref, ck_ref))
    vn_ref[...] = conv_silu(v_ref, cv_ref)

    gates = gt_ref[0]
    glane = lax.broadcasted_iota(jnp.int32, gates.shape, 1)
    log_a_all = -jnp.exp(prm_ref[0:1, :]) * _softplus(gates + prm_ref[1:2, :])
    beta_all = 1.0 / (1.0 + jnp.exp(-gates))
    for d in range(2):
        log_a = jnp.sum(jnp.where(glane == d * DN_HEADS + h, log_a_all, 0.0), axis=-1, keepdims=True)
        beta = jnp.sum(jnp.where(glane == (2 + d) * DN_HEADS + h, beta_all, 0.0), axis=-1, keepdims=True)
        la_ref[d] = jnp.broadcast_to(log_a, (LT, DN_HEAD_DIM))
        be_ref[d] = jnp.broadcast_to(beta, (LT, DN_HEAD_DIM))

    row = lax.broadcasted_iota(jnp.int32, (DC, 2 * DC), 0)
    lane = lax.broadcasted_iota(jnp.int32, (DC, 2 * DC), 1)
    col = lane & (DC - 1)
    isf = lane < DC
    eye2 = row == col
    isb = jnp.logical_not(isf)
    incl = jnp.logical_or(jnp.logical_and(isf, row >= col), jnp.logical_and(isb, row <= col))
    strict = jnp.logical_or(jnp.logical_and(isf, row > col), jnp.logical_and(isb, row < col))
    r64 = lax.broadcasted_iota(jnp.int32, (DC, DC), 0)
    c64 = lax.broadcasted_iota(jnp.int32, (DC, DC), 1)
    cum_lhs = _stack3(jnp.concatenate([(r64 >= c64).astype(BF16), jnp.ones((DC, DC), BF16)], axis=0))
    ones_lhs = _stack3(jnp.ones((DC, DC), BF16))
    eye_f = eye2.astype(F32)
    same16 = (row >> 4) == (col >> 4)
    in32_not16 = jnp.logical_and((row >> 5) == (col >> 5), jnp.logical_not(same16))
    not32 = (row >> 5) != (col >> 5)
    nt_dims = (((1,), (1,)), ((), ()))

    def block_diag(x2):
        return jnp.concatenate([jnp.where(isf[:x2.shape[0]], x2, 0.0), jnp.where(isb[:x2.shape[0]], x2, 0.0)],
                               axis=0)

    def mm_pair(x2, y2):
        xh, xl = _split2(x2)
        yh, yl = _split2(block_diag(y2))
        rhs = jnp.concatenate([jnp.concatenate([yh, yl], axis=1),
                               jnp.concatenate([yh, jnp.zeros_like(yl)], axis=1)], axis=0)
        out = _dot(jnp.concatenate([xh, xl], axis=1), rhs)
        return out[:, :2 * DC] + out[:, 2 * DC:]

    def prep_chunk(n):
        r0 = pl.multiple_of(n * DC, DC)
        rows = pl.ds(r0, DC)
        qn, kn, vn = qn_ref[rows, :], kn_ref[rows, :], vn_ref[rows, :]
        la_f, la_b = la_ref[0, rows, :], la_ref[1, rows, :]
        be_f, be_b = be_ref[0, rows, :], be_ref[1, rows, :]
        pt_f = _mm_exact_lhs(cum_lhs, la_f)
        pt_b = _mm_exact_lhs(cum_lhs, la_b)
        kb16 = kn.astype(BF16)
        kcat = jnp.concatenate([kb16, kb16], axis=0)
        kk2 = lax.dot_general(kb16, kcat, nt_dims, preferred_element_type=F32)
        qk2 = lax.dot_general(qn.astype(BF16), kcat, nt_dims, preferred_element_type=F32)
        yield
        g_f, tot_f = pt_f[:DC], pt_f[DC:]
        tot_b = pt_b[DC:]
        g_b = tot_b - pt_b[:DC] + la_b
        gg = jnp.where(isf, g_f, g_b)
        g_row = _mm_exact_lhs(ones_lhs, jnp.where(eye2, gg, 0.0))
        yield
        gam = jnp.where(incl, jnp.exp(gg - g_row), 0.0)
        a2 = jnp.where(strict, kk2 * jnp.where(isf, be_f, be_b) * gam, 0.0)
        dmat = jnp.where(same16, a2, 0.0)
        d2 = mm_pair(dmat, dmat)
        yield
        d4 = mm_pair(d2, d2)
        inv = mm_pair(eye_f - dmat, eye_f + d2)
        yield
        d8 = mm_pair(d4, d4)
        inv = mm_pair(inv, eye_f + d4)
        yield
        inv = mm_pair(inv, eye_f + d8)
        yield
        t = mm_pair(inv, jnp.where(in32_not16, a2, 0.0))
        yield
        inv = inv - mm_pair(t, inv)
        yield
        t = mm_pair(inv, jnp.where(not32, a2, 0.0))
        yield
        inv = inv - mm_pair(t, inv)
        yield
        eg_f, eg_b = jnp.exp(g_f), jnp.exp(g_b)
        zeros_rhs = jnp.zeros((DC, 2 * DN_HEAD_DIM), BF16)
        rhs_f = jnp.concatenate([vn * be_f, kn * be_f * eg_f], axis=1).astype(BF16)
        rhs_b = jnp.concatenate([vn * be_b, kn * be_b * eg_b], axis=1).astype(BF16)
        rhs_bd = jnp.concatenate([jnp.concatenate([rhs_f, zeros_rhs], axis=1),
                                  jnp.concatenate([zeros_rhs, rhs_b], axis=1)], axis=0)
        ih, il = _split2(inv)
        uw = _dot(jnp.concatenate([ih, il], axis=1), jnp.concatenate([rhs_bd, rhs_bd], axis=0))
        uw_f, uw_b = uw[:, :2 * DN_HEAD_DIM], uw[:, 2 * DN_HEAD_DIM:]
        yield
        kdt_f = jnp.transpose(kn * jnp.exp(tot_f - g_f)).astype(BF16)
        kdt_b = jnp.transpose(kn * jnp.exp(tot_b - g_b)).astype(BF16)
        bp_f = _dot(kdt_f, uw_f.astype(BF16))
        bp_b = _dot(kdt_b, uw_b.astype(BF16))
        yield
        u_ref[n, 0] = uw_f[:, :DN_HEAD_DIM]
        u_ref[n, 1] = uw_b[:, :DN_HEAD_DIM]
        wq_ref[n, 0] = jnp.concatenate([uw_f[:, DN_HEAD_DIM:], qn * eg_f], axis=0).astype(BF16)
        wq_ref[n, 1] = jnp.concatenate([uw_b[:, DN_HEAD_DIM:], qn * eg_b], axis=0).astype(BF16)
        qk_ref[n] = (qk2 * gam).astype(BF16)
        b_ref[n, 0] = bp_f[:, :DN_HEAD_DIM]
        b_ref[n, 1] = bp_b[:, :DN_HEAD_DIM]
        p_ref[n, 0] = bp_f[:, DN_HEAD_DIM:].astype(BF16)
        p_ref[n, 1] = bp_b[:, DN_HEAD_DIM:].astype(BF16)
        e_ref[n, 0] = jnp.exp(tot_f[0:8])
        e_ref[n, 1] = jnp.exp(tot_b[0:8])

    def prep(i, carry):
        _run_in_lockstep([prep_chunk(i * DN_PREP_GROUP + j) for j in range(DN_PREP_GROUP)])
        return carry

    lax.fori_loop(0, DN_NCHUNK // DN_PREP_GROUP, prep, 0)

    s_ref[...] = jnp.zeros(s_ref.shape, F32)

    def scan_dir(d, n):
        state = s_ref[d]
        s16 = state.astype(BF16)
        sst_ref[n, d] = s16
        ps = _dot(p_ref[n, d], s16)
        yield
        s_ref[d] = state * e_ref[n, d][0:1, :] - ps + b_ref[n, d]

    def scan_step(s, carry):
        nb = jnp.where(s < DN_CTX_CHUNKS, DN_CTX_CHUNKS - 1 - s, DN_NCHUNK - 1 + DN_CTX_CHUNKS - s)
        _run_in_lockstep([scan_dir(0, s), scan_dir(1, nb)])
        return carry

    lax.fori_loop(0, DN_NCHUNK, scan_step, 0)

    def finish_chunk(n):
        rows = pl.ds(pl.multiple_of(n * DC, DC), DC)
        sw_f = _dot(wq_ref[n, 0], sst_ref[n, 0])
        sw_b = _dot(wq_ref[n, 1], sst_ref[n, 1])
        yield
        vcat = jnp.concatenate([u_ref[n, 0] - sw_f[:DC], u_ref[n, 1] - sw_b[:DC]], axis=0).astype(BF16)
        intra = _dot(qk_ref[n], vcat)
        yield
        o = sw_f[DC:] + sw_b[DC:] + intra
        y = o * lax.rsqrt(jnp.mean(o * o, axis=-1, keepdims=True) + EPS) * nw_ref[...]
        o_ref[0, rows, :] = (y * _silu(z_ref[0, rows, :])).astype(BF16)

    def finish(i, carry):
        _run_in_lockstep([finish_chunk(i * DN_PREP_GROUP + j) for j in range(DN_PREP_GROUP)])
        return carry

    lax.fori_loop(0, DN_NCHUNK // DN_PREP_GROUP, finish, 0)


def _dn_mixer(p, conv_w, a_log, dt_bias, norm_w):
    bsz = p.shape[0]
    hd = DN_HEAD_DIM
    prm = jnp.pad(jnp.stack([a_log.reshape(-1), dt_bias.reshape(-1)]), ((0, 0), (0, hd - 2 * DN_HEADS)))
    gate_blk = AB_MAIN_COLS // hd
    seq = lambda off: pl.BlockSpec((1, LT, hd), lambda b, h: (b, 0, off + h))
    cw = lambda off: pl.BlockSpec((CONV_WIDTH, hd), lambda b, h: (0, off + h))
    return pl.pallas_call(
        _dn_kernel,
        grid=(bsz, DN_HEADS),
        in_specs=[seq(0), seq(DN_HEADS), seq(2 * DN_HEADS), seq(3 * DN_HEADS),
                  pl.BlockSpec((1, LT, hd), lambda b, h: (b, 0, gate_blk)),
                  cw(0), cw(DN_HEADS), cw(2 * DN_HEADS),
                  pl.BlockSpec((2, hd), lambda b, h: (0, 0)),
                  pl.BlockSpec((1, hd), lambda b, h: (0, 0))],
        out_specs=pl.BlockSpec((1, LT, hd), lambda b, h: (b, 0, h)),
        out_shape=jax.ShapeDtypeStruct((bsz, LT, DN_WIDTH), BF16),
        scratch_shapes=[pltpu.VMEM((LT, hd), F32), pltpu.VMEM((LT, hd), F32), pltpu.VMEM((LT, hd), F32),
                        pltpu.VMEM((2, LT, hd), F32), pltpu.VMEM((2, LT, hd), F32),
                        pltpu.VMEM((DN_NCHUNK, 2, 2 * DC, hd), BF16),
                        pltpu.VMEM((DN_NCHUNK, 2, DC, hd), F32),
                        pltpu.VMEM((DN_NCHUNK, DC, 2 * DC), BF16),
                        pltpu.VMEM((DN_NCHUNK, 2, hd, hd), F32),
                        pltpu.VMEM((DN_NCHUNK, 2, hd, hd), BF16),
                        pltpu.VMEM((DN_NCHUNK, 2, 8, hd), F32),
                        pltpu.VMEM((DN_NCHUNK, 2, hd, hd), BF16),
                        pltpu.VMEM((2, hd, hd), F32)],
        compiler_params=_cparams(("parallel", "parallel")),
        name="gated_deltanet",
    )(p, p, p, p, p, conv_w, conv_w, conv_w, prm, norm_w.reshape(1, hd))


def _rms_norm(x, w=None):
    y = x * lax.rsqrt(jnp.mean(x * x, axis=-1, keepdims=True) + EPS)
    return y if w is None else y * w


def _l2norm(t):
    return t * lax.rsqrt(jnp.sum(t * t, axis=-1, keepdims=True) + EPS)


def _to_heads(t, n):
    B, L, _ = t.shape
    return t.reshape(B, L, n, -1).transpose(0, 2, 1, 3)


def _rotate_half(x):
    x1, x2 = jnp.split(x, 2, axis=-1)
    return jnp.concatenate([-x2, x1], axis=-1)


def _rope(x, ang):
    cos = jnp.concatenate([jnp.cos(ang)] * 2, axis=-1)
    sin = jnp.concatenate([jnp.sin(ang)] * 2, axis=-1)
    return x * cos + _rotate_half(x) * sin


def _axial_angles(L):
    rows = L // GRID_W
    r, cidx = jnp.meshgrid(jnp.arange(rows), jnp.arange(GRID_W), indexing='ij')
    axis_dim = DIFF_QK_DIM // 2
    inv = ROPE_BASE ** (-jnp.arange(0, axis_dim, 2, dtype=F32) / axis_dim)
    return (r.reshape(-1).astype(F32)[:, None] * inv, cidx.reshape(-1).astype(F32)[:, None] * inv)


def _axial_rope(x, ang_r, ang_c):
    xr, xc = jnp.split(x, 2, axis=-1)
    return jnp.concatenate([_rope(xr, ang_r), _rope(xc, ang_c)], axis=-1)


def _short_conv(u, w):
    k = w.shape[0]
    return lax.conv_general_dilated(u, w.astype(u.dtype)[:, None, :], window_strides=(1,),
                                    padding=[(k // 2, k // 2)],
                                    dimension_numbers=('NWC', 'WIO', 'NWC'),
                                    feature_group_count=u.shape[-1])


def _two_pass(fn, ctx_seqs, lat_seqs, S0):
    outs_c, outs_l = [], []
    for d in range(2):
        f = (lambda t: jnp.flip(t, 2)) if d == 1 else (lambda t: t)
        oc, S = fn(d, [f(t) for t in ctx_seqs], S0)
        ol, _ = fn(d, [f(t) for t in lat_seqs], S)
        outs_c.append(f(oc))
        outs_l.append(f(ol))
    return outs_c[0] + outs_c[1], outs_l[0] + outs_l[1]


def _gated_delta_chunked(q, k, v, log_a, beta, S0):
    B, H, L, _ = q.shape
    dv = v.shape[-1]
    C = DN_CHUNK
    n = L // C
    q, k, v = (t.reshape(B, H, n, C, t.shape[-1]) for t in (q, k, v))
    log_a, beta = (t.reshape(B, H, n, C) for t in (log_a, beta))
    g = jnp.cumsum(log_a, axis=-1)
    tril = jnp.tril(jnp.ones((C, C), bool))
    strict = jnp.tril(jnp.ones((C, C), bool), -1)
    gamma = jnp.exp(jnp.where(tril, g[..., :, None] - g[..., None, :], -jnp.inf))
    kb = k * beta[..., None]
    lower = jnp.where(strict, jnp.einsum('bhnid,bhnjd->bhnij', kb, k) * gamma, 0.0)
    tmat = lower + jnp.eye(C, dtype=lower.dtype)
    solve = lambda rhs: lax.linalg.triangular_solve(tmat, rhs, left_side=True, lower=True,
                                                    unit_diagonal=True)
    u = solve(v * beta[..., None])
    w = solve(kb * jnp.exp(g)[..., None])
    qk = jnp.einsum('bhnid,bhnjd->bhnij', q, k) * gamma

    def step(S, inp):
        q_i, k_i, u_i, w_i, g_i, qk_i = inp
        v_new = u_i - jnp.einsum('bhcd,bhde->bhce', w_i, S)
        o = (jnp.einsum('bhcd,bhde->bhce', q_i * jnp.exp(g_i)[..., None], S)
             + jnp.einsum('bhij,bhje->bhie', qk_i, v_new))
        g_last = g_i[..., -1:]
        S = (S * jnp.exp(g_last)[..., None]
             + jnp.einsum('bhcd,bhce->bhde', k_i * jnp.exp(g_last - g_i)[..., None], v_new))
        return S, o

    xs = tuple(jnp.moveaxis(t, 2, 0) for t in (q, k, u, w, g, qk))
    S, o = lax.scan(step, S0, xs)
    return jnp.moveaxis(o, 0, 2).reshape(B, H, L, dv), S


def _retention_chunked(q, k, v, log_gamma, S0):
    B, H, L, _ = q.shape
    dv = v.shape[-1]
    C = RET_CHUNK
    n = L // C
    q, k, v = (t.reshape(B, H, n, C, t.shape[-1]) for t in (q, k, v))
    pos = jnp.arange(C, dtype=F32)
    lg = log_gamma[:, None]
    tril = jnp.tril(jnp.ones((C, C), bool))
    dec = jnp.exp(jnp.where(tril, (pos[:, None] - pos[None, :]) * lg[:, :, None], -jnp.inf))
    o_inner = jnp.einsum('bhnij,bhnje->bhnie',
                         jnp.einsum('bhnid,bhnjd->bhnij', q, k) * dec[:, None], v)
    q_dec = jnp.exp(lg * (pos + 1))[:, :, None]
    k_dec = jnp.exp(lg * (C - 1 - pos))[:, :, None]
    c_dec = jnp.exp(log_gamma * C)[:, None, None]

    def step(S, inp):
        q_i, k_i, v_i, o_i = inp
        o = o_i + jnp.einsum('bhcd,bhde->bhce', q_i, S) * q_dec
        S = S * c_dec + jnp.einsum('bhcd,bhce->bhde', k_i * k_dec, v_i)
        return S, o

    xs = tuple(jnp.moveaxis(t, 2, 0) for t in (q, k, v, o_inner))
    S, o = lax.scan(step, S0, xs)
    return jnp.moveaxis(o, 0, 2).reshape(B, H, L, dv), S


def _diff_attend(q, k, v, lam_full):
    s = jnp.einsum('bhmqd,bhmkd->bhmqk', q, k) * DIFF_QK_DIM ** -0.5
    p = jax.nn.softmax(s, axis=-1)
    return jnp.einsum('bhqk,bhkd->bhqd', p[:, :, 0] - lam_full * p[:, :, 1], v)


def _ab_mixer_jax(pc, pl_, conv_w, A_log, dt_bias, dn_norm_w, lam, subln_w, layer_idx):
    B, L, _ = pl_[0].shape

    def dn_seqs(p):
        q, k, v, _, a, b = p[:6]
        qkv = jax.nn.silu(_short_conv(jnp.concatenate([q, k, v], axis=-1), conv_w))
        q, k, v = (_to_heads(t, DN_HEADS) for t in jnp.split(qkv, 3, axis=-1))
        q = _l2norm(q) * DN_HEAD_DIM ** -0.5
        k = _l2norm(k)
        Bs, Ls, _ = a.shape
        a = a.reshape(Bs, Ls, 2, DN_HEADS)
        b = b.reshape(Bs, Ls, 2, DN_HEADS)
        log_a = (-jnp.exp(A_log) * jax.nn.softplus(a + dt_bias)).transpose(0, 3, 1, 2)
        beta = jax.nn.sigmoid(b).transpose(0, 3, 1, 2)
        return [q, k, v, log_a, beta]

    S0 = jnp.zeros((B, DN_HEADS, DN_HEAD_DIM, DN_HEAD_DIM), F32)
    dn_fn = lambda d, s, S: _gated_delta_chunked(s[0], s[1], s[2], s[3][..., d], s[4][..., d], S)
    dn_c, dn_l = _two_pass(dn_fn, dn_seqs(pc), dn_seqs(pl_), S0)

    def dn_merge(o, z):
        Bs, H, Ls, dv = o.shape
        y = _rms_norm(o.transpose(0, 2, 1, 3), dn_norm_w) * jax.nn.silu(z).reshape(Bs, Ls, H, dv)
        return y.reshape(Bs, Ls, H * dv)

    lambda_init = 0.8 - 0.6 * math.exp(-0.3 * layer_idx)
    lam_full = jnp.exp(jnp.sum(lam[0] * lam[1])) - jnp.exp(jnp.sum(lam[2] * lam[3])) + lambda_init

    def diff_qkv(p):
        q, k, v = p[6:]
        Bs, Ls, _ = q.shape
        q, k = (t.reshape(Bs, Ls, DIFF_HEADS, 2, DIFF_QK_DIM).transpose(0, 2, 3, 1, 4) for t in (q, k))
        return q, k, _to_heads(v, DIFF_HEADS)

    def diff_merge(o):
        Bs, H, Ls, dv = o.shape
        return (_rms_norm(o, subln_w) * (1 - lambda_init)).transpose(0, 2, 1, 3).reshape(Bs, Ls, H * dv)

    qc, kc, vc = diff_qkv(pc)
    ql, kl, vl = diff_qkv(pl_)
    ang_r, ang_c = _axial_angles(L)
    ql, kl = _axial_rope(ql, ang_r, ang_c), _axial_rope(kl, ang_r, ang_c)
    k_all = jnp.concatenate([kc, kl], axis=3)
    v_all = jnp.concatenate([vc, vl], axis=2)
    nb = L // Q_BLOCK
    qb = jnp.moveaxis(ql.reshape(B, DIFF_HEADS, 2, nb, Q_BLOCK, DIFF_QK_DIM), 3, 0)
    ol = lax.map(lambda qi: _diff_attend(qi, k_all, v_all, lam_full), qb)
    ol = jnp.moveaxis(ol, 0, 2).reshape(B, DIFF_HEADS, L, DIFF_V_DIM)
    oc = _diff_attend(qc, kc, vc, lam_full)
    y_l = (dn_merge(dn_l, pl_[3]), diff_merge(ol))
    y_c = (dn_merge(dn_c, pc[3]), diff_merge(oc))
    return y_c, y_l


def _ret_mixer_jax(pc, pl_, decay_logit):
    def stream(p, rotate):
        q, k, v, g = p
        q, k, v = (_to_heads(t, RET_HEADS) for t in (q, k, v))
        if rotate:
            inv = ROPE_BASE ** (-jnp.linspace(0.0, 1.0, RET_KEY_DIM // 2, dtype=F32))
            ang = jnp.arange(q.shape[2], dtype=F32)[:, None] * inv
            q, k = _rope(q, ang), _rope(k, ang)
        return [q, k * RET_KEY_DIM ** -0.5, v], g

    sc, gc = stream(pc, False)
    sl, gl = stream(pl_, True)
    log_gamma = jax.nn.log_sigmoid(decay_logit)
    S0 = jnp.zeros((pl_[0].shape[0], RET_HEADS, RET_KEY_DIM, RET_VALUE_DIM), F32)
    ret_fn = lambda d, s, S: _retention_chunked(s[0], s[1], s[2], log_gamma[d], S)
    oc, ol = _two_pass(ret_fn, sc, sl, S0)

    def merge(o, g):
        Bs, H, Ls, dv = o.shape
        return _rms_norm(o.transpose(0, 2, 1, 3)).reshape(Bs, Ls, H * dv) * jax.nn.silu(g)

    return merge(oc, gc), merge(ol, gl)


AB_MAIN_COLS = 4 * DN_WIDTH + 3 * DIFF_WIDTH
AB_GATE_COLS = 256
AB_PROJ_TN = (AB_MAIN_COLS + AB_GATE_COLS) // 3
RET_PROJ_TN = RET_IN // 3


def _split_ab_weight(w):
    o_ab = 4 * DN_WIDTH
    n_ab = 4 * DN_HEADS
    main = jnp.concatenate([w[..., :o_ab], w[..., o_ab + n_ab:]], axis=-1)
    pad = [(0, 0)] * (w.ndim - 1) + [(0, AB_GATE_COLS - n_ab)]
    return jnp.concatenate([main, jnp.pad(w[..., o_ab:o_ab + n_ab], pad)], axis=-1)


def kernel(x, c, ctx, c_ctx, ada_w, ada_b, norm_w, final_norm_w, ffn_w_in, ffn_w_out, ab_w_in,
           ab_conv_w, dn_A_log, dn_dt_bias, dn_norm_w, diff_lambda, diff_subln_w, ab_w_out,
           ret_w_in, ret_decay_logit, ret_w_out):
    B = x.shape[0]
    xs = jnp.concatenate([ctx, x], axis=1).reshape(B * LT, D_MODEL)
    cvec = jnp.concatenate([c, c_ctx[None], jnp.zeros((MOD_ROWS - B - 1, D_MODEL), F32)], axis=0)
    mods_all = _ada_mods(cvec, ada_w, ada_b)
    mods_all = mods_all.reshape(DEPTH, MOD_ROWS, N_MOD, D_MODEL).transpose(0, 2, 1, 3)
    norms = norm_w.reshape(DEPTH, 3, 1, D_MODEL)
    ffn_w_in_b = ffn_w_in.astype(BF16)
    ffn_w_out_b = ffn_w_out.astype(BF16)
    ab_w_in_b = _split_ab_weight(ab_w_in).astype(BF16)
    ab_w_out_b = ab_w_out.astype(BF16)
    ret_w_in_b = ret_w_in.astype(BF16)
    ret_w_out_b = ret_w_out.astype(BF16)
    for l in range(DEPTH):
        last = l == DEPTH - 1
        xs = _ffn(xs, norms, mods_all, l, 0, ffn_w_in_b, ffn_w_out_b, 0)
        i = l // 2
        if l % 2 == 0:
            p = _proj(xs, norms, mods_all, l, ab_w_in_b, i, AB_PROJ_TN).reshape(B, LT, -1)
            y_dn = _dn_mixer(p, ab_conv_w[i], dn_A_log[i], dn_dt_bias[i], dn_norm_w[i])
            y_df = _diff_mixer(p, diff_lambda[i], diff_subln_w[i], l)
            xs = _out_proj(xs, mods_all, l, [y_dn.reshape(B * LT, DN_WIDTH), y_df.reshape(B * LT, DIFF_WIDTH)],
                           ab_w_out_b, i)
        else:
            p = _proj(xs, norms, mods_all, l, ret_w_in_b, i, RET_PROJ_TN).reshape(B, LT, -1)
            y = _ret_mixer(p, ret_decay_logit[i])
            xs = _out_proj(xs, mods_all, l, [y.reshape(B * LT, RET_V_WIDTH)], ret_w_out_b, i)
        xs = _ffn(xs, norms, mods_all, l, 2, ffn_w_in_b, ffn_w_out_b, 1,
                  final_w=final_norm_w if last else None)
    return xs.reshape(B, LT, D_MODEL)[:, CTX_LEN:]
```

```python
import functools
import math

import jax
import jax.numpy as jnp
import numpy as np
from jax import lax
from jax.experimental import pallas as pl
from jax.experimental.pallas import tpu as pltpu

D_MODEL = 1024
BATCH = 8
SEQ = 2048
DEPTH = 4
GRID_W = 64
CTX_LEN = 256
EPS = 1e-6
N_MOD = 9
DN_HEADS = 4
DN_HEAD_DIM = 128
DN_WIDTH = DN_HEADS * DN_HEAD_DIM
CONV_WIDTH = 3
DN_CHUNK = 64
DIFF_HEADS = 4
DIFF_QK_DIM = 64
DIFF_V_DIM = 2 * DIFF_QK_DIM
DIFF_WIDTH = DIFF_HEADS * DIFF_V_DIM
Q_BLOCK = 128
ROPE_BASE = 10000.0
RET_HEADS = 8
RET_KEY_DIM = D_MODEL // RET_HEADS
RET_VALUE_DIM = 2 * RET_KEY_DIM
RET_V_WIDTH = RET_HEADS * RET_VALUE_DIM
RET_CHUNK = 64
D_FF = 2816
AB_SIZES = [DN_WIDTH] * 4 + [2 * DN_HEADS] * 2 + [DIFF_WIDTH] * 3
AB_IN = sum(AB_SIZES)
RET_SIZES = [RET_HEADS * RET_KEY_DIM] * 2 + [RET_V_WIDTH] * 2
RET_IN = sum(RET_SIZES)

F32 = jnp.float32
BF16 = jnp.bfloat16

LT = CTX_LEN + SEQ
TM = LT // 2
TILES_PER_BATCH = LT // TM
MOD_ROWS = 16
CTX_ROW = BATCH
V7X_VMEM_LIMIT = 56 * 1024 * 1024
FF_CHUNK = 256
FFN_ROW_SPLIT = 2


def _cparams(sem):
    return pltpu.CompilerParams(dimension_semantics=sem, vmem_limit_bytes=V7X_VMEM_LIMIT)


def _silu(x):
    return x * (1.0 / (1.0 + jnp.exp(-x)))


def _dot(a, b):
    return jnp.dot(a, b, preferred_element_type=F32)


def _run_in_lockstep(gens):
    while gens:
        alive = []
        for g in gens:
            try:
                next(g)
                alive.append(g)
            except StopIteration:
                pass
        gens = alive


def _ada_kernel(s_ref, w_ref, b_ref, o_ref):
    s = _silu(s_ref[...]).astype(BF16)
    o_ref[0] = _dot(s, w_ref[0].astype(BF16)) + b_ref[0]


def _ada_mods(cvec, ada_w, ada_b):
    tn = 1024
    n = N_MOD * D_MODEL
    return pl.pallas_call(
        _ada_kernel,
        grid=(DEPTH, n // tn),
        in_specs=[pl.BlockSpec((MOD_ROWS, D_MODEL), lambda l, j: (0, 0)),
                  pl.BlockSpec((1, D_MODEL, tn), lambda l, j: (l, 0, j)),
                  pl.BlockSpec((1, 1, tn), lambda l, j: (l, 0, j))],
        out_specs=pl.BlockSpec((1, MOD_ROWS, tn), lambda l, j: (l, 0, j)),
        out_shape=jax.ShapeDtypeStruct((DEPTH, MOD_ROWS, n), F32),
        compiler_params=_cparams(("parallel", "parallel")),
        name="ada_mods",
    )(cvec, ada_w, ada_b.reshape(DEPTH, 1, n))


def _tile_mod_rows(mod_ref, k):
    i = pl.program_id(0)
    lat = mod_ref[k, pl.ds(i // TILES_PER_BATCH, 1), :]
    head = jnp.where(i % TILES_PER_BATCH == 0, mod_ref[k, CTX_ROW:CTX_ROW + 1, :], lat)
    return head, lat


def _mod_row_blocks(r0, r1):
    if r0 < CTX_LEN < r1:
        return [(r0, CTX_LEN, 0), (CTX_LEN, r1, 1)]
    return [(r0, r1, 0 if r1 <= CTX_LEN else 1)]


def _ada_in_rows(x_ref, g_ref, mod_ref, h_ref, r0, r1):
    shift, scale = _tile_mod_rows(mod_ref, 0), _tile_mod_rows(mod_ref, 1)
    for a, b, k in _mod_row_blocks(r0, r1):
        x = x_ref[a:b, :]
        y = x * lax.rsqrt(jnp.mean(x * x, axis=-1, keepdims=True) + EPS) * g_ref[...]
        h_ref[a:b, :] = (y * (1.0 + scale[k]) + shift[k]).astype(BF16)


def _gated_residual_rows(x_ref, upd_ref, mod_ref, o_ref, gate_scale, final_w_ref, r0, r1):
    gate = _tile_mod_rows(mod_ref, 2)
    for a, b, k in _mod_row_blocks(r0, r1):
        v = x_ref[a:b, :] + (gate_scale * gate[k]) * upd_ref[a:b, :]
        if final_w_ref is not None:
            v = v * lax.rsqrt(jnp.mean(v * v, axis=-1, keepdims=True) + EPS) * final_w_ref[...]
        o_ref[a:b, :] = v


def _mod_spec(layer, sub):
    return pl.BlockSpec((None, 3, MOD_ROWS, D_MODEL), lambda *idx: (layer, sub, 0, 0))


def _norm_spec(layer, sub):
    return pl.BlockSpec((None, None, 1, D_MODEL), lambda *idx: (layer, sub, 0, 0))


def _ffn_kernel(*refs, with_final):
    if with_final:
        x_ref, g_ref, mod_ref, win_ref, wout_ref, fw_ref, o_ref, h_ref, acc_ref = refs
    else:
        x_ref, g_ref, mod_ref, win_ref, wout_ref, o_ref, h_ref, acc_ref = refs
        fw_ref = None
    step = TM // FFN_ROW_SPLIT
    for r0 in range(0, TM, step):
        r1 = r0 + step
        _ada_in_rows(x_ref, g_ref, mod_ref, h_ref, r0, r1)
        for f in range(D_FF // FF_CHUNK):
            h = h_ref[r0:r1, :]
            gg = _dot(h, win_ref[:, f * FF_CHUNK:(f + 1) * FF_CHUNK])
            uu = _dot(h, win_ref[:, D_FF + f * FF_CHUNK:D_FF + (f + 1) * FF_CHUNK])
            a = (_silu(gg) * uu).astype(BF16)
            part = _dot(a, wout_ref[f * FF_CHUNK:(f + 1) * FF_CHUNK, :])
            if f == 0:
                acc_ref[r0:r1, :] = part
            else:
                acc_ref[r0:r1, :] += part
        _gated_residual_rows(x_ref, acc_ref, mod_ref, o_ref, 0.5, fw_ref, r0, r1)


def _ffn(x, norms, mods, layer, sub, w_in, w_out, which, final_w=None):
    t = x.shape[0]
    single = pl.Buffered(1)
    in_specs = [pl.BlockSpec((TM, D_MODEL), lambda i: (i, 0)),
                _norm_spec(layer, sub),
                _mod_spec(layer, sub),
                pl.BlockSpec((None, None, D_MODEL, 2 * D_FF), lambda i: (layer, which, 0, 0),
                             pipeline_mode=single),
                pl.BlockSpec((None, None, D_FF, D_MODEL), lambda i: (layer, which, 0, 0),
                             pipeline_mode=single)]
    args = [x, norms, mods, w_in, w_out]
    if final_w is not None:
        in_specs.append(pl.BlockSpec((1, D_MODEL), lambda i: (0, 0)))
        args.append(final_w.reshape(1, D_MODEL))
    return pl.pallas_call(
        functools.partial(_ffn_kernel, with_final=final_w is not None),
        grid=(t // TM,),
        in_specs=in_specs,
        out_specs=pl.BlockSpec((TM, D_MODEL), lambda i: (i, 0)),
        out_shape=jax.ShapeDtypeStruct((t, D_MODEL), F32),
        scratch_shapes=[pltpu.VMEM((TM, D_MODEL), BF16), pltpu.VMEM((TM, D_MODEL), F32)],
        compiler_params=_cparams(("parallel",)),
        name="ffn",
    )(*args)


def _proj_kernel(x_ref, g_ref, mod_ref, w_ref, o_ref, h_ref):
    @pl.when(pl.program_id(1) == 0)
    def _():
        step = TM // FFN_ROW_SPLIT
        for r0 in range(0, TM, step):
            _ada_in_rows(x_ref, g_ref, mod_ref, h_ref, r0, r0 + step)
            o_ref[r0:r0 + step, :] = _dot(h_ref[r0:r0 + step, :], w_ref[...])

    @pl.when(pl.program_id(1) != 0)
    def _():
        o_ref[...] = _dot(h_ref[...], w_ref[...])


def _proj(x, norms, mods, layer, w, which, tn):
    t = x.shape[0]
    n = w.shape[2]
    return pl.pallas_call(
        _proj_kernel,
        grid=(t // TM, n // tn),
        in_specs=[pl.BlockSpec((TM, D_MODEL), lambda i, j: (i, 0)),
                  _norm_spec(layer, 1),
                  _mod_spec(layer, 1),
                  pl.BlockSpec((None, D_MODEL, tn), lambda i, j: (which, 0, j))],
        out_specs=pl.BlockSpec((TM, tn), lambda i, j: (i, j)),
        out_shape=jax.ShapeDtypeStruct((t, n), F32),
        scratch_shapes=[pltpu.VMEM((TM, D_MODEL), BF16)],
        compiler_params=_cparams(("parallel", "arbitrary")),
        name="mixer_in_proj",
    )(x, norms, mods, w)


def _out_proj_kernel(*refs, n_parts):
    x_ref, mod_ref = refs[0], refs[1]
    y_refs = refs[2:2 + n_parts]
    w_refs = refs[2 + n_parts:2 + 2 * n_parts]
    o_ref = refs[2 + 2 * n_parts]
    upd = None
    for y_ref, w_ref in zip(y_refs, w_refs):
        part = _dot(y_ref[...].astype(BF16), w_ref[...])
        upd = part if upd is None else upd + part
    _gated_residual_rows(x_ref, upd, mod_ref, o_ref, 1.0, None, 0, TM)


def _out_proj(x, mods, layer, ys, w, which):
    t = x.shape[0]
    n_parts = len(ys)
    kp = ys[0].shape[1]
    in_specs = [pl.BlockSpec((TM, D_MODEL), lambda i: (i, 0)), _mod_spec(layer, 1)]
    in_specs += [pl.BlockSpec((TM, kp), lambda i: (i, 0)) for _ in ys]
    in_specs += [pl.BlockSpec((None, kp, D_MODEL), lambda i, p=p: (which, p, 0)) for p in range(n_parts)]
    return pl.pallas_call(
        functools.partial(_out_proj_kernel, n_parts=n_parts),
        grid=(t // TM,),
        in_specs=in_specs,
        out_specs=pl.BlockSpec((TM, D_MODEL), lambda i: (i, 0)),
        out_shape=jax.ShapeDtypeStruct((t, D_MODEL), F32),
        compiler_params=_cparams(("parallel",)),
        name="mixer_out_proj",
    )(x, mods, *ys, *([w] * n_parts))


RC = CTX_LEN
RET_NCHUNK = LT // RC
RET_OUT_GROUP = 1


def _log_sigmoid(x):
    return jnp.minimum(x, 0.0) - jnp.log1p(jnp.exp(-jnp.abs(x)))


def _ret_kernel(q_ref, k_ref, v_ref, g_ref, cos_ref, sin_ref, dl_ref, o_ref,
                qb_ref, kb_ref, kv_ref, s_ref):
    lg = _log_sigmoid(dl_ref[0])
    lg_f, lg_b = lg[0:1, :], lg[1:2, :]
    cos, sin = cos_ref[...], sin_ref[...]
    q = q_ref[0]
    k = k_ref[0] * (RET_KEY_DIM ** -0.5)
    half = RET_KEY_DIM // 2
    qb_ref[...] = (q * cos + pltpu.roll(q, half, 1) * sin).astype(BF16)
    kr = k * cos + pltpu.roll(k, half, 1) * sin
    kb_ref[...] = kr.astype(BF16)

    pos = lax.broadcasted_iota(jnp.int32, (RC, 1), 0).astype(F32)
    kdec_f = jnp.exp(lg_f * (RC - 1.0 - pos))
    kdec_b = jnp.exp(lg_b * pos)
    qdec_f = jnp.exp(lg_f * (pos + 1.0))
    qdec_b = jnp.exp(lg_b * (RC - pos))
    c_f = jnp.exp(lg_f * RC)
    c_b = jnp.exp(lg_b * RC)
    ii = lax.broadcasted_iota(jnp.int32, (RC, RC), 0)
    jj = lax.broadcasted_iota(jnp.int32, (RC, RC), 1)
    dist = (ii - jj).astype(F32)
    dmat = (jnp.where(ii >= jj, jnp.exp(lg_f * dist), 0.0)
            + jnp.where(ii <= jj, jnp.exp(-lg_b * dist), 0.0))

    tn_dims = (((0,), (0,)), ((), ()))
    for n in range(RET_NCHUNK):
        rows = slice(n * RC, (n + 1) * RC)
        kn = kr[rows]
        vn = v_ref[0, rows, :].astype(BF16)
        kv_ref[0, n] = lax.dot_general((kn * kdec_f).astype(BF16), vn, tn_dims,
                                       preferred_element_type=F32)
        kv_ref[1, n] = lax.dot_general((kn * kdec_b).astype(BF16), vn, tn_dims,
                                       preferred_element_type=F32)

    sf = jnp.zeros((RET_KEY_DIM, RET_VALUE_DIM), F32)
    for n in range(RET_NCHUNK):
        s_ref[n, :, :RET_VALUE_DIM] = sf.astype(BF16)
        sf = sf * c_f + kv_ref[0, n]
    s_ref[0, :, RET_VALUE_DIM:] = jnp.zeros((RET_KEY_DIM, RET_VALUE_DIM), BF16)
    sb = kv_ref[1, 0]
    for n in range(RET_NCHUNK - 1, 0, -1):
        s_ref[n, :, RET_VALUE_DIM:] = sb.astype(BF16)
        sb = sb * c_b + kv_ref[1, n]

    nt_dims = (((1,), (1,)), ((), ()))

    def out_chunk(n):
        rows = slice(n * RC, (n + 1) * RC)
        qn = qb_ref[rows, :]
        s = lax.dot_general(qn, kb_ref[rows, :], nt_dims, preferred_element_type=F32)
        cross = _dot(qn, s_ref[n])
        yield
        o = _dot((s * dmat).astype(BF16), v_ref[0, rows, :].astype(BF16))
        yield
        o = o + cross[:, :RET_VALUE_DIM] * qdec_f + cross[:, RET_VALUE_DIM:] * qdec_b
        y = o * lax.rsqrt(jnp.mean(o * o, axis=-1, keepdims=True) + EPS)
        o_ref[0, rows, :] = (y * _silu(g_ref[0, rows, :])).astype(BF16)

    for n0 in range(0, RET_NCHUNK, RET_OUT_GROUP):
        _run_in_lockstep([out_chunk(n) for n in range(n0, n0 + RET_OUT_GROUP)])


def _ret_rope_tables():
    inv = ROPE_BASE ** (-jnp.linspace(0.0, 1.0, RET_KEY_DIM // 2, dtype=F32))
    ang = jnp.arange(SEQ, dtype=F32)[:, None] * inv
    ang = jnp.concatenate([jnp.zeros((CTX_LEN, RET_KEY_DIM // 2), F32), ang], axis=0)
    cos = jnp.concatenate([jnp.cos(ang)] * 2, axis=-1)
    sin = jnp.concatenate([-jnp.sin(ang), jnp.sin(ang)], axis=-1)
    return cos, sin


def _ret_mixer(p, decay_logit):
    bsz = p.shape[0]
    cos, sin = _ret_rope_tables()
    dl = decay_logit.T.reshape(RET_HEADS, 2, 1)
    kq = RET_HEADS
    kv = (2 * RET_HEADS * RET_KEY_DIM) // RET_VALUE_DIM
    return pl.pallas_call(
        _ret_kernel,
        grid=(bsz, RET_HEADS),
        in_specs=[pl.BlockSpec((1, LT, RET_KEY_DIM), lambda b, h: (b, 0, h)),
                  pl.BlockSpec((1, LT, RET_KEY_DIM), lambda b, h: (b, 0, kq + h)),
                  pl.BlockSpec((1, LT, RET_VALUE_DIM), lambda b, h: (b, 0, kv + h)),
                  pl.BlockSpec((1, LT, RET_VALUE_DIM), lambda b, h: (b, 0, kv + RET_HEADS + h)),
                  pl.BlockSpec((LT, RET_KEY_DIM), lambda b, h: (0, 0)),
                  pl.BlockSpec((LT, RET_KEY_DIM), lambda b, h: (0, 0)),
                  pl.BlockSpec((1, 2, 1), lambda b, h: (h, 0, 0))],
        out_specs=pl.BlockSpec((1, LT, RET_VALUE_DIM), lambda b, h: (b, 0, h)),
        out_shape=jax.ShapeDtypeStruct((bsz, LT, RET_V_WIDTH), BF16),
        scratch_shapes=[pltpu.VMEM((LT, RET_KEY_DIM), BF16),
                        pltpu.VMEM((LT, RET_KEY_DIM), BF16),
                        pltpu.VMEM((2, RET_NCHUNK, RET_KEY_DIM, RET_VALUE_DIM), F32),
                        pltpu.VMEM((RET_NCHUNK, RET_KEY_DIM, 2 * RET_VALUE_DIM), BF16)],
        compiler_params=_cparams(("parallel", "parallel")),
        name="retention",
    )(p, p, p, p, cos, sin, dl)


TQ = CTX_LEN
DIFF_NQ = LT // TQ
DIFF_Q_SPLIT = 4


def _axial_rope_tile(x, cos, sin_lo, sin_hi):
    q = DIFF_QK_DIM // 4
    return x * cos + pltpu.roll(x, 128 - q, 1) * sin_lo + pltpu.roll(x, q, 1) * sin_hi


def _diff_kernel(q_ref, k_ref, v_ref, cos_ref, slo_ref, shi_ref, lam_ref, w_ref, o_ref,
                 kb_ref, vb_ref, *, lambda_init):
    i = pl.program_id(2)

    @pl.when(i == 0)
    def _():
        kb_ref[...] = _axial_rope_tile(k_ref[0], cos_ref[...], slo_ref[...], shi_ref[...]).astype(BF16)
        ones_col = (lax.broadcasted_iota(jnp.int32, (LT, DIFF_V_DIM), 1) == 0).astype(BF16)
        vb_ref[...] = jnp.concatenate([v_ref[0].astype(BF16), ones_col], axis=1)

    lam = lam_ref[...]
    lam_full = (jnp.exp(jnp.sum(lam[0:1] * lam[1:2], axis=-1, keepdims=True))
                - jnp.exp(jnp.sum(lam[2:3] * lam[3:4], axis=-1, keepdims=True)) + lambda_init)
    r0 = pl.multiple_of(i * TQ, TQ)
    q = q_ref[0] * (DIFF_QK_DIM ** -0.5 * math.log2(math.e))
    qr = _axial_rope_tile(q, cos_ref[pl.ds(r0, TQ), :], slo_ref[pl.ds(r0, TQ), :], shi_ref[pl.ds(r0, TQ), :])
    first_map = lax.broadcasted_iota(jnp.int32, (TQ, DIFF_V_DIM), 1) < DIFF_QK_DIM
    q_maps = (jnp.where(first_map, qr, 0.0).astype(BF16), jnp.where(first_map, 0.0, qr).astype(BF16))
    out_scale = w_ref[...] * (1.0 - lambda_init)

    def attend_rows(r, n_keys):
        qs = jnp.concatenate([q_maps[0][r], q_maps[1][r]], axis=0)
        nr = qs.shape[0] // 2
        s = lax.dot_general(qs, kb_ref[:n_keys, :], (((1,), (1,)), ((), ())),
                            preferred_element_type=F32)
        yield
        e = jnp.exp2(s - jnp.max(s, axis=-1, keepdims=True)).astype(BF16)
        ob = _dot(e, vb_ref[:n_keys, :])
        yield
        ob = ob[:, :DIFF_V_DIM] / ob[:, DIFF_V_DIM:DIFF_V_DIM + 1]
        o = ob[:nr] - lam_full * ob[nr:]
        y = o * lax.rsqrt(jnp.mean(o * o, axis=-1, keepdims=True) + EPS) * out_scale
        o_ref[0, r, :] = y.astype(BF16)

    def attend(n_keys, n_split):
        step = TQ // n_split
        _run_in_lockstep([attend_rows(slice(j * step, (j + 1) * step), n_keys) for j in range(n_split)])

    @pl.when(i == 0)
    def _():
        attend(CTX_LEN, 1)

    @pl.when(i > 0)
    def _():
        attend(LT, DIFF_Q_SPLIT)


def _diff_rope_tables():
    rows = SEQ // GRID_W
    r, cidx = jnp.meshgrid(jnp.arange(rows), jnp.arange(GRID_W), indexing='ij')
    axis_dim = DIFF_QK_DIM // 2
    inv = ROPE_BASE ** (-jnp.arange(0, axis_dim, 2, dtype=F32) / axis_dim)
    ang_r = r.reshape(-1).astype(F32)[:, None] * inv
    ang_c = cidx.reshape(-1).astype(F32)[:, None] * inv
    zero = jnp.zeros_like(ang_r)

    def table(fn_lo, fn_hi):
        one = jnp.concatenate([fn_lo(ang_r), fn_hi(ang_r), fn_lo(ang_c), fn_hi(ang_c)], axis=-1)
        return jnp.concatenate([one, one], axis=-1)

    cos = table(jnp.cos, jnp.cos)
    sin_lo = table(lambda a: -jnp.sin(a), lambda a: zero)
    sin_hi = table(lambda a: zero, jnp.sin)
    ident = [jnp.ones((CTX_LEN, DIFF_V_DIM), F32), jnp.zeros((CTX_LEN, DIFF_V_DIM), F32),
             jnp.zeros((CTX_LEN, DIFF_V_DIM), F32)]
    return [jnp.concatenate([c, t], axis=0) for c, t in zip(ident, (cos, sin_lo, sin_hi))]


def _diff_mixer(p, lam, subln_w, layer_idx):
    bsz = p.shape[0]
    lambda_init = 0.8 - 0.6 * math.exp(-0.3 * layer_idx)
    cos, sin_lo, sin_hi = _diff_rope_tables()
    q0 = 4 * DN_WIDTH // DIFF_V_DIM
    k0 = q0 + DIFF_HEADS
    v0 = k0 + DIFF_HEADS
    tab = pl.BlockSpec((LT, DIFF_V_DIM), lambda b, h, i: (0, 0))
    return pl.pallas_call(
        functools.partial(_diff_kernel, lambda_init=lambda_init),
        grid=(bsz, DIFF_HEADS, DIFF_NQ),
        in_specs=[pl.BlockSpec((1, TQ, DIFF_V_DIM), lambda b, h, i: (b, i, q0 + h)),
                  pl.BlockSpec((1, LT, DIFF_V_DIM), lambda b, h, i: (b, 0, k0 + h)),
                  pl.BlockSpec((1, LT, DIFF_V_DIM), lambda b, h, i: (b, 0, v0 + h)),
                  tab, tab, tab,
                  pl.BlockSpec((4, DIFF_QK_DIM), lambda b, h, i: (0, 0)),
                  pl.BlockSpec((1, DIFF_V_DIM), lambda b, h, i: (0, 0))],
        out_specs=pl.BlockSpec((1, TQ, DIFF_V_DIM), lambda b, h, i: (b, i, h)),
        out_shape=jax.ShapeDtypeStruct((bsz, LT, DIFF_WIDTH), BF16),
        scratch_shapes=[pltpu.VMEM((LT, DIFF_V_DIM), BF16), pltpu.VMEM((LT, 2 * DIFF_V_DIM), BF16)],
        compiler_params=_cparams(("parallel", "parallel", "arbitrary")),
        name="diff_attention",
    )(p, p, p, cos, sin_lo, sin_hi, lam, subln_w.reshape(1, DIFF_V_DIM))


DC = DN_CHUNK
DN_NCHUNK = LT // DC
DN_CTX_CHUNKS = CTX_LEN // DC


DN_PREP_GROUP = 9


def _split2(x):
    hi = x.astype(BF16)
    return hi, (x - hi.astype(F32)).astype(BF16)


def _mm_hilo(a, b):
    ah, al = _split2(a)
    bh, bl = _split2(b)
    n = b.shape[1]
    lhs = jnp.concatenate([ah, al], axis=1)
    rhs = jnp.concatenate([jnp.concatenate([bh, bl], axis=1),
                           jnp.concatenate([bh, jnp.zeros_like(bl)], axis=1)], axis=0)
    out = _dot(lhs, rhs)
    return out[:, :n] + out[:, n:]


def _stack3(a_bf16):
    return jnp.concatenate([a_bf16, a_bf16, a_bf16], axis=1)


def _mm_exact_lhs(a3_bf16, x):
    h1 = x.astype(BF16)
    r1 = x - h1.astype(F32)
    h2 = r1.astype(BF16)
    h3 = (r1 - h2.astype(F32)).astype(BF16)
    return _dot(a3_bf16, jnp.concatenate([h1, h2, h3], axis=0))


def _softplus(x):
    return jnp.maximum(x, 0.0) + jnp.log1p(jnp.exp(-jnp.abs(x)))


def _dn_kernel(q_ref, k_ref, v_ref, z_ref, gt_ref, cq_ref, ck_ref, cv_ref, prm_ref, nw_ref, o_ref,
               qn_ref, kn_ref, vn_ref, la_ref, be_ref, wq_ref, u_ref, qk_ref, b_ref, p_ref, e_ref,
               sst_ref, s_ref):
    h = pl.program_id(1)
    tpos = lax.broadcasted_iota(jnp.int32, (LT, 1), 0)
    has_prev = jnp.logical_and(tpos != 0, tpos != CTX_LEN)
    has_next = jnp.logical_and(tpos != CTX_LEN - 1, tpos != LT - 1)

    def conv_silu(x_ref, c_ref):
        x = x_ref[0]
        w = c_ref[...]
        prev = jnp.where(has_prev, pltpu.roll(x, 1, 0), 0.0)
        nxt = jnp.where(has_next, pltpu.roll(x, LT - 1, 0), 0.0)
        return _silu(prev * w[0:1] + x * w[1:2] + nxt * w[2:3])

    def l2n(t):
        return t * lax.rsqrt(jnp.sum(t * t, axis=-1, keepdims=True) + EPS)

    qn_ref[...] = l2n(conv_silu(q_ref, cq_ref)) * (DN_HEAD_DIM ** -0.5)
    kn_ref[...] = l2n(conv_silu(k_ref, ck_ref))
    vn_ref[...] = conv_silu(v_ref, cv_ref)

    gates = gt_ref[0]
    glane = lax.broadcasted_iota(jnp.int32, gates.shape, 1)
    log_a_all = -jnp.exp(prm_ref[0:1, :]) * _softplus(gates + prm_ref[1:2, :])
    beta_all = 1.0 / (1.0 + jnp.exp(-gates))
    for d in range(2):
        log_a = jnp.sum(jnp.where(glane == d * DN_HEADS + h, log_a_all, 0.0), axis=-1, keepdims=True)
        beta = jnp.sum(jnp.where(glane == (2 + d) * DN_HEADS + h, beta_all, 0.0), axis=-1, keepdims=True)
        la_ref[d] = jnp.broadcast_to(log_a, (LT, DN_HEAD_DIM))
        be_ref[d] = jnp.broadcast_to(beta, (LT, DN_HEAD_DIM))

    row = lax.broadcasted_iota(jnp.int32, (DC, 2 * DC), 0)
    lane = lax.broadcasted_iota(jnp.int32, (DC, 2 * DC), 1)
    col = lane & (DC - 1)
    isf = lane < DC
    eye2 = row == col
    isb = jnp.logical_not(isf)
    incl = jnp.logical_or(jnp.logical_and(isf, row >= col), jnp.logical_and(isb, row <= col))
    strict = jnp.logical_or(jnp.logical_and(isf, row > col), jnp.logical_and(isb, row < col))
    r64 = lax.broadcasted_iota(jnp.int32, (DC, DC), 0)
    c64 = lax.broadcasted_iota(jnp.int32, (DC, DC), 1)
    cum_lhs = _stack3(jnp.concatenate([(r64 >= c64).astype(BF16), jnp.ones((DC, DC), BF16)], axis=0))
    ones_lhs = _stack3(jnp.ones((DC, DC), BF16))
    eye_f = eye2.astype(F32)
    same16 = (row >> 4) == (col >> 4)
    in32_not16 = jnp.logical_and((row >> 5) == (col >> 5), jnp.logical_not(same16))
    not32 = (row >> 5) != (col >> 5)
    nt_dims = (((1,), (1,)), ((), ()))

    def block_diag(x2):
        return jnp.concatenate([jnp.where(isf[:x2.shape[0]], x2, 0.0), jnp.where(isb[:x2.shape[0]], x2, 0.0)],
                               axis=0)

    def mm_pair(x2, y2):
        xh, xl = _split2(x2)
        yh, yl = _split2(block_diag(y2))
        rhs = jnp.concatenate([jnp.concatenate([yh, yl], axis=1),
                               jnp.concatenate([yh, jnp.zeros_like(yl)], axis=1)], axis=0)
        out = _dot(jnp.concatenate([xh, xl], axis=1), rhs)
        return out[:, :2 * DC] + out[:, 2 * DC:]

    def prep_chunk(n):
        r0 = pl.multiple_of(n * DC, DC)
        rows = pl.ds(r0, DC)
        qn, kn, vn = qn_ref[rows, :], kn_ref[rows, :], vn_ref[rows, :]
        la_f, la_b = la_ref[0, rows, :], la_ref[1, rows, :]
        be_f, be_b = be_ref[0, rows, :], be_ref[1, rows, :]
        pt_f = _mm_exact_lhs(cum_lhs, la_f)
        pt_b = _mm_exact_lhs(cum_lhs, la_b)
        kb16 = kn.astype(BF16)
        kcat = jnp.concatenate([kb16, kb16], axis=0)
        kk2 = lax.dot_general(kb16, kcat, nt_dims, preferred_element_type=F32)
        qk2 = lax.dot_general(qn.astype(BF16), kcat, nt_dims, preferred_element_type=F32)
        yield
        g_f, tot_f = pt_f[:DC], pt_f[DC:]
        tot_b = pt_b[DC:]
        g_b = tot_b - pt_b[:DC] + la_b
        gg = jnp.where(isf, g_f, g_b)
        g_row = _mm_exact_lhs(ones_lhs, jnp.where(eye2, gg, 0.0))
        yield
        gam = jnp.where(incl, jnp.exp(gg - g_row), 0.0)
        a2 = jnp.where(strict, kk2 * jnp.where(isf, be_f, be_b) * gam, 0.0)
        dmat = jnp.where(same16, a2, 0.0)
        d2 = mm_pair(dmat, dmat)
        yield
        d4 = mm_pair(d2, d2)
        inv = mm_pair(eye_f - dmat, eye_f + d2)
        yield
        d8 = mm_pair(d4, d4)
        inv = mm_pair(inv, eye_f + d4)
        yield
        inv = mm_pair(inv, eye_f + d8)
        yield
        t = mm_pair(inv, jnp.where(in32_not16, a2, 0.0))
        yield
        inv = inv - mm_pair(t, inv)
        yield
        t = mm_pair(inv, jnp.where(not32, a2, 0.0))
        yield
        inv = inv - mm_pair(t, inv)
        yield
        eg_f, eg_b = jnp.exp(g_f), jnp.exp(g_b)
        zeros_rhs = jnp.zeros((DC, 2 * DN_HEAD_DIM), BF16)
        rhs_f = jnp.concatenate([vn * be_f, kn * be_f * eg_f], axis=1).astype(BF16)
        rhs_b = jnp.concatenate([vn * be_b, kn * be_b * eg_b], axis=1).astype(BF16)
        rhs_bd = jnp.concatenate([jnp.concatenate([rhs_f, zeros_rhs], axis=1),
                                  jnp.concatenate([zeros_rhs, rhs_b], axis=1)], axis=0)
        ih, il = _split2(inv)
        uw = _dot(jnp.concatenate([ih, il], axis=1), jnp.concatenate([rhs_bd, rhs_bd], axis=0))
        uw_f, uw_b = uw[:, :2 * DN_HEAD_DIM], uw[:, 2 * DN_HEAD_DIM:]
        yield
        kdt_f = jnp.transpose(kn * jnp.exp(tot_f - g_f)).astype(BF16)
        kdt_b = jnp.transpose(kn * jnp.exp(tot_b - g_b)).astype(BF16)
        bp_f = _dot(kdt_f, uw_f.astype(BF16))
        bp_b = _dot(kdt_b, uw_b.astype(BF16))
        yield
        u_ref[n, 0] = uw_f[:, :DN_HEAD_DIM]
        u_ref[n, 1] = uw_b[:, :DN_HEAD_DIM]
        wq_ref[n, 0] = jnp.concatenate([uw_f[:, DN_HEAD_DIM:], qn * eg_f], axis=0).astype(BF16)
        wq_ref[n, 1] = jnp.concatenate([uw_b[:, DN_HEAD_DIM:], qn * eg_b], axis=0).astype(BF16)
        qk_ref[n] = (qk2 * gam).astype(BF16)
        b_ref[n, 0] = bp_f[:, :DN_HEAD_DIM]
        b_ref[n, 1] = bp_b[:, :DN_HEAD_DIM]
        p_ref[n, 0] = bp_f[:, DN_HEAD_DIM:].astype(BF16)
        p_ref[n, 1] = bp_b[:, DN_HEAD_DIM:].astype(BF16)
        e_ref[n, 0] = jnp.exp(tot_f[0:8])
        e_ref[n, 1] = jnp.exp(tot_b[0:8])

    def prep(i, carry):
        _run_in_lockstep([prep_chunk(i * DN_PREP_GROUP + j) for j in range(DN_PREP_GROUP)])
        return carry

    lax.fori_loop(0, DN_NCHUNK // DN_PREP_GROUP, prep, 0)

    s_ref[...] = jnp.zeros(s_ref.shape, F32)

    def scan_dir(d, n):
        state = s_ref[d]
        s16 = state.astype(BF16)
        sst_ref[n, d] = s16
        ps = _dot(p_ref[n, d], s16)
        yield
        s_ref[d] = state * e_ref[n, d][0:1, :] - ps + b_ref[n, d]

    def scan_step(s, carry):
        nb = jnp.where(s < DN_CTX_CHUNKS, DN_CTX_CHUNKS - 1 - s, DN_NCHUNK - 1 + DN_CTX_CHUNKS - s)
        _run_in_lockstep([scan_dir(0, s), scan_dir(1, nb)])
        return carry

    lax.fori_loop(0, DN_NCHUNK, scan_step, 0)

    def finish_chunk(n):
        rows = pl.ds(pl.multiple_of(n * DC, DC), DC)
        sw_f = _dot(wq_ref[n, 0], sst_ref[n, 0])
        sw_b = _dot(wq_ref[n, 1], sst_ref[n, 1])
        yield
        vcat = jnp.concatenate([u_ref[n, 0] - sw_f[:DC], u_ref[n, 1] - sw_b[:DC]], axis=0).astype(BF16)
        intra = _dot(qk_ref[n], vcat)
        yield
        o = sw_f[DC:] + sw_b[DC:] + intra
        y = o * lax.rsqrt(jnp.mean(o * o, axis=-1, keepdims=True) + EPS) * nw_ref[...]
        o_ref[0, rows, :] = (y * _silu(z_ref[0, rows, :])).astype(BF16)

    def finish(i, carry):
        _run_in_lockstep([finish_chunk(i * DN_PREP_GROUP + j) for j in range(DN_PREP_GROUP)])
        return carry

    lax.fori_loop(0, DN_NCHUNK // DN_PREP_GROUP, finish, 0)


def _dn_mixer(p, conv_w, a_log, dt_bias, norm_w):
    bsz = p.shape[0]
    hd = DN_HEAD_DIM
    prm = jnp.pad(jnp.stack([a_log.reshape(-1), dt_bias.reshape(-1)]), ((0, 0), (0, hd - 2 * DN_HEADS)))
    gate_blk = AB_MAIN_COLS // hd
    seq = lambda off: pl.BlockSpec((1, LT, hd), lambda b, h: (b, 0, off + h))
    cw = lambda off: pl.BlockSpec((CONV_WIDTH, hd), lambda b, h: (0, off + h))
    return pl.pallas_call(
        _dn_kernel,
        grid=(bsz, DN_HEADS),
        in_specs=[seq(0), seq(DN_HEADS), seq(2 * DN_HEADS), seq(3 * DN_HEADS),
                  pl.BlockSpec((1, LT, hd), lambda b, h: (b, 0, gate_blk)),
                  cw(0), cw(DN_HEADS), cw(2 * DN_HEADS),
                  pl.BlockSpec((2, hd), lambda b, h: (0, 0)),
                  pl.BlockSpec((1, hd), lambda b, h: (0, 0))],
        out_specs=pl.BlockSpec((1, LT, hd), lambda b, h: (b, 0, h)),
        out_shape=jax.ShapeDtypeStruct((bsz, LT, DN_WIDTH), BF16),
        scratch_shapes=[pltpu.VMEM((LT, hd), F32), pltpu.VMEM((LT, hd), F32), pltpu.VMEM((LT, hd), F32),
                        pltpu.VMEM((2, LT, hd), F32), pltpu.VMEM((2, LT, hd), F32),
                        pltpu.VMEM((DN_NCHUNK, 2, 2 * DC, hd), BF16),
                        pltpu.VMEM((DN_NCHUNK, 2, DC, hd), F32),
                        pltpu.VMEM((DN_NCHUNK, DC, 2 * DC), BF16),
                        pltpu.VMEM((DN_NCHUNK, 2, hd, hd), F32),
                        pltpu.VMEM((DN_NCHUNK, 2, hd, hd), BF16),
                        pltpu.VMEM((DN_NCHUNK, 2, 8, hd), F32),
                        pltpu.VMEM((DN_NCHUNK, 2, hd, hd), BF16),
                        pltpu.VMEM((2, hd, hd), F32)],
        compiler_params=_cparams(("parallel", "parallel")),
        name="gated_deltanet",
    )(p, p, p, p, p, conv_w, conv_w, conv_w, prm, norm_w.reshape(1, hd))


def _rms_norm(x, w=None):
    y = x * lax.rsqrt(jnp.mean(x * x, axis=-1, keepdims=True) + EPS)
    return y if w is None else y * w


def _l2norm(t):
    return t * lax.rsqrt(jnp.sum(t * t, axis=-1, keepdims=True) + EPS)


def _to_heads(t, n):
    B, L, _ = t.shape
    return t.reshape(B, L, n, -1).transpose(0, 2, 1, 3)


def _rotate_half(x):
    x1, x2 = jnp.split(x, 2, axis=-1)
    return jnp.concatenate([-x2, x1], axis=-1)


def _rope(x, ang):
    cos = jnp.concatenate([jnp.cos(ang)] * 2, axis=-1)
    sin = jnp.concatenate([jnp.sin(ang)] * 2, axis=-1)
    return x * cos + _rotate_half(x) * sin


def _axial_angles(L):
    rows = L // GRID_W
    r, cidx = jnp.meshgrid(jnp.arange(rows), jnp.arange(GRID_W), indexing='ij')
    axis_dim = DIFF_QK_DIM // 2
    inv = ROPE_BASE ** (-jnp.arange(0, axis_dim, 2, dtype=F32) / axis_dim)
    return (r.reshape(-1).astype(F32)[:, None] * inv, cidx.reshape(-1).astype(F32)[:, None] * inv)


def _axial_rope(x, ang_r, ang_c):
    xr, xc = jnp.split(x, 2, axis=-1)
    return jnp.concatenate([_rope(xr, ang_r), _rope(xc, ang_c)], axis=-1)


def _short_conv(u, w):
    k = w.shape[0]
    return lax.conv_general_dilated(u, w.astype(u.dtype)[:, None, :], window_strides=(1,),
                                    padding=[(k // 2, k // 2)],
                                    dimension_numbers=('NWC', 'WIO', 'NWC'),
                                    feature_group_count=u.shape[-1])


def _two_pass(fn, ctx_seqs, lat_seqs, S0):
    outs_c, outs_l = [], []
    for d in range(2):
        f = (lambda t: jnp.flip(t, 2)) if d == 1 else (lambda t: t)
        oc, S = fn(d, [f(t) for t in ctx_seqs], S0)
        ol, _ = fn(d, [f(t) for t in lat_seqs], S)
        outs_c.append(f(oc))
        outs_l.append(f(ol))
    return outs_c[0] + outs_c[1], outs_l[0] + outs_l[1]


def _gated_delta_chunked(q, k, v, log_a, beta, S0):
    B, H, L, _ = q.shape
    dv = v.shape[-1]
    C = DN_CHUNK
    n = L // C
    q, k, v = (t.reshape(B, H, n, C, t.shape[-1]) for t in (q, k, v))
    log_a, beta = (t.reshape(B, H, n, C) for t in (log_a, beta))
    g = jnp.cumsum(log_a, axis=-1)
    tril = jnp.tril(jnp.ones((C, C), bool))
    strict = jnp.tril(jnp.ones((C, C), bool), -1)
    gamma = jnp.exp(jnp.where(tril, g[..., :, None] - g[..., None, :], -jnp.inf))
    kb = k * beta[..., None]
    lower = jnp.where(strict, jnp.einsum('bhnid,bhnjd->bhnij', kb, k) * gamma, 0.0)
    tmat = lower + jnp.eye(C, dtype=lower.dtype)
    solve = lambda rhs: lax.linalg.triangular_solve(tmat, rhs, left_side=True, lower=True,
                                                    unit_diagonal=True)
    u = solve(v * beta[..., None])
    w = solve(kb * jnp.exp(g)[..., None])
    qk = jnp.einsum('bhnid,bhnjd->bhnij', q, k) * gamma

    def step(S, inp):
        q_i, k_i, u_i, w_i, g_i, qk_i = inp
        v_new = u_i - jnp.einsum('bhcd,bhde->bhce', w_i, S)
        o = (jnp.einsum('bhcd,bhde->bhce', q_i * jnp.exp(g_i)[..., None], S)
             + jnp.einsum('bhij,bhje->bhie', qk_i, v_new))
        g_last = g_i[..., -1:]
        S = (S * jnp.exp(g_last)[..., None]
             + jnp.einsum('bhcd,bhce->bhde', k_i * jnp.exp(g_last - g_i)[..., None], v_new))
        return S, o

    xs = tuple(jnp.moveaxis(t, 2, 0) for t in (q, k, u, w, g, qk))
    S, o = lax.scan(step, S0, xs)
    return jnp.moveaxis(o, 0, 2).reshape(B, H, L, dv), S


def _retention_chunked(q, k, v, log_gamma, S0):
    B, H, L, _ = q.shape
    dv = v.shape[-1]
    C = RET_CHUNK
    n = L // C
    q, k, v = (t.reshape(B, H, n, C, t.shape[-1]) for t in (q, k, v))
    pos = jnp.arange(C, dtype=F32)
    lg = log_gamma[:, None]
    tril = jnp.tril(jnp.ones((C, C), bool))
    dec = jnp.exp(jnp.where(tril, (pos[:, None] - pos[None, :]) * lg[:, :, None], -jnp.inf))
    o_inner = jnp.einsum('bhnij,bhnje->bhnie',
                         jnp.einsum('bhnid,bhnjd->bhnij', q, k) * dec[:, None], v)
    q_dec = jnp.exp(lg * (pos + 1))[:, :, None]
    k_dec = jnp.exp(lg * (C - 1 - pos))[:, :, None]
    c_dec = jnp.exp(log_gamma * C)[:, None, None]

    def step(S, inp):
        q_i, k_i, v_i, o_i = inp
        o = o_i + jnp.einsum('bhcd,bhde->bhce', q_i, S) * q_dec
        S = S * c_dec + jnp.einsum('bhcd,bhce->bhde', k_i * k_dec, v_i)
        return S, o

    xs = tuple(jnp.moveaxis(t, 2, 0) for t in (q, k, v, o_inner))
    S, o = lax.scan(step, S0, xs)
    return jnp.moveaxis(o, 0, 2).reshape(B, H, L, dv), S


def _diff_attend(q, k, v, lam_full):
    s = jnp.einsum('bhmqd,bhmkd->bhmqk', q, k) * DIFF_QK_DIM ** -0.5
    p = jax.nn.softmax(s, axis=-1)
    return jnp.einsum('bhqk,bhkd->bhqd', p[:, :, 0] - lam_full * p[:, :, 1], v)


def _ab_mixer_jax(pc, pl_, conv_w, A_log, dt_bias, dn_norm_w, lam, subln_w, layer_idx):
    B, L, _ = pl_[0].shape

    def dn_seqs(p):
        q, k, v, _, a, b = p[:6]
        qkv = jax.nn.silu(_short_conv(jnp.concatenate([q, k, v], axis=-1), conv_w))
        q, k, v = (_to_heads(t, DN_HEADS) for t in jnp.split(qkv, 3, axis=-1))
        q = _l2norm(q) * DN_HEAD_DIM ** -0.5
        k = _l2norm(k)
        Bs, Ls, _ = a.shape
        a = a.reshape(Bs, Ls, 2, DN_HEADS)
        b = b.reshape(Bs, Ls, 2, DN_HEADS)
        log_a = (-jnp.exp(A_log) * jax.nn.softplus(a + dt_bias)).transpose(0, 3, 1, 2)
        beta = jax.nn.sigmoid(b).transpose(0, 3, 1, 2)
        return [q, k, v, log_a, beta]

    S0 = jnp.zeros((B, DN_HEADS, DN_HEAD_DIM, DN_HEAD_DIM), F32)
    dn_fn = lambda d, s, S: _gated_delta_chunked(s[0], s[1], s[2], s[3][..., d], s[4][..., d], S)
    dn_c, dn_l = _two_pass(dn_fn, dn_seqs(pc), dn_seqs(pl_), S0)

    def dn_merge(o, z):
        Bs, H, Ls, dv = o.shape
        y = _rms_norm(o.transpose(0, 2, 1, 3), dn_norm_w) * jax.nn.silu(z).reshape(Bs, Ls, H, dv)
        return y.reshape(Bs, Ls, H * dv)

    lambda_init = 0.8 - 0.6 * math.exp(-0.3 * layer_idx)
    lam_full = jnp.exp(jnp.sum(lam[0] * lam[1])) - jnp.exp(jnp.sum(lam[2] * lam[3])) + lambda_init

    def diff_qkv(p):
        q, k, v = p[6:]
        Bs, Ls, _ = q.shape
        q, k = (t.reshape(Bs, Ls, DIFF_HEADS, 2, DIFF_QK_DIM).transpose(0, 2, 3, 1, 4) for t in (q, k))
        return q, k, _to_heads(v, DIFF_HEADS)

    def diff_merge(o):
        Bs, H, Ls, dv = o.shape
        return (_rms_norm(o, subln_w) * (1 - lambda_init)).transpose(0, 2, 1, 3).reshape(Bs, Ls, H * dv)

    qc, kc, vc = diff_qkv(pc)
    ql, kl, vl = diff_qkv(pl_)
    ang_r, ang_c = _axial_angles(L)
    ql, kl = _axial_rope(ql, ang_r, ang_c), _axial_rope(kl, ang_r, ang_c)
    k_all = jnp.concatenate([kc, kl], axis=3)
    v_all = jnp.concatenate([vc, vl], axis=2)
    nb = L // Q_BLOCK
    qb = jnp.moveaxis(ql.reshape(B, DIFF_HEADS, 2, nb, Q_BLOCK, DIFF_QK_DIM), 3, 0)
    ol = lax.map(lambda qi: _diff_attend(qi, k_all, v_all, lam_full), qb)
    ol = jnp.moveaxis(ol, 0, 2).reshape(B, DIFF_HEADS, L, DIFF_V_DIM)
    oc = _diff_attend(qc, kc, vc, lam_full)
    y_l = (dn_merge(dn_l, pl_[3]), diff_merge(ol))
    y_c = (dn_merge(dn_c, pc[3]), diff_merge(oc))
    return y_c, y_l


def _ret_mixer_jax(pc, pl_, decay_logit):
    def stream(p, rotate):
        q, k, v, g = p
        q, k, v = (_to_heads(t, RET_HEADS) for t in (q, k, v))
        if rotate:
            inv = ROPE_BASE ** (-jnp.linspace(0.0, 1.0, RET_KEY_DIM // 2, dtype=F32))
            ang = jnp.arange(q.shape[2], dtype=F32)[:, None] * inv
            q, k = _rope(q, ang), _rope(k, ang)
        return [q, k * RET_KEY_DIM ** -0.5, v], g

    sc, gc = stream(pc, False)
    sl, gl = stream(pl_, True)
    log_gamma = jax.nn.log_sigmoid(decay_logit)
    S0 = jnp.zeros((pl_[0].shape[0], RET_HEADS, RET_KEY_DIM, RET_VALUE_DIM), F32)
    ret_fn = lambda d, s, S: _retention_chunked(s[0], s[1], s[2], log_gamma[d], S)
    oc, ol = _two_pass(ret_fn, sc, sl, S0)

    def merge(o, g):
        Bs, H, Ls, dv = o.shape
        return _rms_norm(o.transpose(0, 2, 1, 3)).reshape(Bs, Ls, H * dv) * jax.nn.silu(g)

    return merge(oc, gc), merge(ol, gl)


AB_MAIN_COLS = 4 * DN_WIDTH + 3 * DIFF_WIDTH
AB_GATE_COLS = 256
AB_PROJ_TN = (AB_MAIN_COLS + AB_GATE_COLS) // 3
RET_PROJ_TN = RET_IN // 3


def _split_ab_weight(w):
    o_ab = 4 * DN_WIDTH
    n_ab = 4 * DN_HEADS
    main = jnp.concatenate([w[..., :o_ab], w[..., o_ab + n_ab:]], axis=-1)
    pad = [(0, 0)] * (w.ndim - 1) + [(0, AB_GATE_COLS - n_ab)]
    return jnp.concatenate([main, jnp.pad(w[..., o_ab:o_ab + n_ab], pad)], axis=-1)


def kernel(x, c, ctx, c_ctx, ada_w, ada_b, norm_w, final_norm_w, ffn_w_in, ffn_w_out, ab_w_in,
           ab_conv_w, dn_A_log, dn_dt_bias, dn_norm_w, diff_lambda, diff_subln_w, ab_w_out,
           ret_w_in, ret_decay_logit, ret_w_out):
    B = x.shape[0]
    xs = jnp.concatenate([ctx, x], axis=1).reshape(B * LT, D_MODEL)
    cvec = jnp.concatenate([c, c_ctx[None], jnp.zeros((MOD_ROWS - B - 1, D_MODEL), F32)], axis=0)
    mods_all = _ada_mods(cvec, ada_w, ada_b)
    mods_all = mods_all.reshape(DEPTH, MOD_ROWS, N_MOD, D_MODEL).transpose(0, 2, 1, 3)
    norms = norm_w.reshape(DEPTH, 3, 1, D_MODEL)
    ffn_w_in_b = ffn_w_in.astype(BF16)
    ffn_w_out_b = ffn_w_out.astype(BF16)
    ab_w_in_b = _split_ab_weight(ab_w_in).astype(BF16)
    ab_w_out_b = ab_w_out.astype(BF16)
    ret_w_in_b = ret_w_in.astype(BF16)
    ret_w_out_b = ret_w_out.astype(BF16)
    for l in range(DEPTH):
        last = l == DEPTH - 1
        xs = _ffn(xs, norms, mods_all, l, 0, ffn_w_in_b, ffn_w_out_b, 0)
        i = l // 2
        if l % 2 == 0:
            p = _proj(xs, norms, mods_all, l, ab_w_in_b, i, AB_PROJ_TN).reshape(B, LT, -1)
            y_dn = _dn_mixer(p, ab_conv_w[i], dn_A_log[i], dn_dt_bias[i], dn_norm_w[i])
            y_df = _diff_mixer(p, diff_lambda[i], diff_subln_w[i], l)
            xs = _out_proj(xs, mods_all, l, [y_dn.reshape(B * LT, DN_WIDTH), y_df.reshape(B * LT, DIFF_WIDTH)],
                           ab_w_out_b, i)
        else:
            p = _proj(xs, norms, mods_all, l, ret_w_in_b, i, RET_PROJ_TN).reshape(B, LT, -1)
            y = _ret_mixer(p, ret_decay_logit[i])
            xs = _out_proj(xs, mods_all, l, [y.reshape(B * LT, RET_V_WIDTH)], ret_w_out_b, i)
        xs = _ffn(xs, norms, mods_all, l, 2, ffn_w_in_b, ffn_w_out_b, 1,
                  final_w=final_norm_w if last else None)
    return xs.reshape(B, LT, D_MODEL)[:, CTX_LEN:]
```

```python
import functools
import math

import jax
import jax.numpy as jnp
from jax import lax
from jax.experimental import pallas as pl
from jax.experimental.pallas import tpu as pltpu

D_MODEL = 1024
BATCH = 8
SEQ = 2048
DEPTH = 4
GRID_W = 64
CTX_LEN = 256
EPS = 1e-6
N_MOD = 9
DN_HEADS = 4
DN_HEAD_DIM = 128
DN_WIDTH = DN_HEADS * DN_HEAD_DIM
CONV_WIDTH = 3
DN_CHUNK = 64
DIFF_HEADS = 4
DIFF_QK_DIM = 64
DIFF_V_DIM = 2 * DIFF_QK_DIM
DIFF_WIDTH = DIFF_HEADS * DIFF_V_DIM
ROPE_BASE = 10000.0
RET_HEADS = 8
RET_KEY_DIM = D_MODEL // RET_HEADS
RET_VALUE_DIM = 2 * RET_KEY_DIM
RET_V_WIDTH = RET_HEADS * RET_VALUE_DIM
D_FF = 2816
RET_IN = 2 * RET_HEADS * RET_KEY_DIM + 2 * RET_V_WIDTH
AB_MAIN_COLS = 4 * DN_WIDTH + 3 * DIFF_WIDTH
AB_GATE_COLS = 256
AB_PROJ_TN = (AB_MAIN_COLS + AB_GATE_COLS) // 3
RET_PROJ_TN = RET_IN // 3

F32 = jnp.float32
BF16 = jnp.bfloat16

LT = CTX_LEN + SEQ
TM = LT // 2
TILES_PER_BATCH = LT // TM
MOD_ROWS = 16
CTX_ROW = BATCH
V7X_VMEM_LIMIT = 56 * 1024 * 1024
FF_CHUNK = 256
FFN_ROW_SPLIT = 2


def _cparams(sem):
    return pltpu.CompilerParams(dimension_semantics=sem, vmem_limit_bytes=V7X_VMEM_LIMIT)


def _silu(x):
    return x * (1.0 / (1.0 + jnp.exp(-x)))


def _dot(a, b):
    return jnp.dot(a, b, preferred_element_type=F32)


def _run_in_lockstep(gens):
    while gens:
        alive = []
        for g in gens:
            try:
                next(g)
                alive.append(g)
            except StopIteration:
                pass
        gens = alive


def _ada_kernel(s_ref, w_ref, b_ref, o_ref):
    s = _silu(s_ref[...]).astype(BF16)
    o_ref[0] = _dot(s, w_ref[0].astype(BF16)) + b_ref[0]


def _ada_mods(cvec, ada_w, ada_b):
    tn = 1024
    n = N_MOD * D_MODEL
    return pl.pallas_call(
        _ada_kernel,
        grid=(DEPTH, n // tn),
        in_specs=[pl.BlockSpec((MOD_ROWS, D_MODEL), lambda l, j: (0, 0)),
                  pl.BlockSpec((1, D_MODEL, tn), lambda l, j: (l, 0, j)),
                  pl.BlockSpec((1, 1, tn), lambda l, j: (l, 0, j))],
        out_specs=pl.BlockSpec((1, MOD_ROWS, tn), lambda l, j: (l, 0, j)),
        out_shape=jax.ShapeDtypeStruct((DEPTH, MOD_ROWS, n), F32),
        compiler_params=_cparams(("parallel", "parallel")),
        name="ada_mods",
    )(cvec, ada_w, ada_b.reshape(DEPTH, 1, n))


def _tile_mod_rows(mod_ref, k):
    i = pl.program_id(0)
    lat = mod_ref[k, pl.ds(i // TILES_PER_BATCH, 1), :]
    head = jnp.where(i % TILES_PER_BATCH == 0, mod_ref[k, CTX_ROW:CTX_ROW + 1, :], lat)
    return head, lat


def _mod_row_blocks(r0, r1):
    if r0 < CTX_LEN < r1:
        return [(r0, CTX_LEN, 0), (CTX_LEN, r1, 1)]
    return [(r0, r1, 0 if r1 <= CTX_LEN else 1)]


def _ada_in_rows(x_ref, g_ref, mod_ref, h_ref, r0, r1):
    shift, scale = _tile_mod_rows(mod_ref, 0), _tile_mod_rows(mod_ref, 1)
    for a, b, k in _mod_row_blocks(r0, r1):
        x = x_ref[a:b, :]
        y = x * lax.rsqrt(jnp.mean(x * x, axis=-1, keepdims=True) + EPS) * g_ref[...]
        h_ref[a:b, :] = (y * (1.0 + scale[k]) + shift[k]).astype(BF16)


def _gated_residual_rows(x_ref, upd_ref, mod_ref, o_ref, gate_scale, final_w_ref, r0, r1):
    gate = _tile_mod_rows(mod_ref, 2)
    for a, b, k in _mod_row_blocks(r0, r1):
        v = x_ref[a:b, :] + (gate_scale * gate[k]) * upd_ref[a:b, :]
        if final_w_ref is not None:
            v = v * lax.rsqrt(jnp.mean(v * v, axis=-1, keepdims=True) + EPS) * final_w_ref[...]
        o_ref[a:b, :] = v


def _mod_spec(layer, sub):
    return pl.BlockSpec((None, 3, MOD_ROWS, D_MODEL), lambda *idx: (layer, sub, 0, 0))


def _norm_spec(layer, sub):
    return pl.BlockSpec((None, None, 1, D_MODEL), lambda *idx: (layer, sub, 0, 0))


def _ffn_kernel(*refs, with_final):
    if with_final:
        x_ref, g_ref, mod_ref, win_ref, wout_ref, fw_ref, o_ref, h_ref, acc_ref = refs
    else:
        x_ref, g_ref, mod_ref, win_ref, wout_ref, o_ref, h_ref, acc_ref = refs
        fw_ref = None
    step = TM // FFN_ROW_SPLIT
    for r0 in range(0, TM, step):
        r1 = r0 + step
        _ada_in_rows(x_ref, g_ref, mod_ref, h_ref, r0, r1)
        for f in range(D_FF // FF_CHUNK):
            h = h_ref[r0:r1, :]
            gg = _dot(h, win_ref[:, f * FF_CHUNK:(f + 1) * FF_CHUNK])
            uu = _dot(h, win_ref[:, D_FF + f * FF_CHUNK:D_FF + (f + 1) * FF_CHUNK])
            a = (_silu(gg) * uu).astype(BF16)
            part = _dot(a, wout_ref[f * FF_CHUNK:(f + 1) * FF_CHUNK, :])
            if f == 0:
                acc_ref[r0:r1, :] = part
            else:
                acc_ref[r0:r1, :] += part
        _gated_residual_rows(x_ref, acc_ref, mod_ref, o_ref, 0.5, fw_ref, r0, r1)


def _ffn(x, norms, mods, layer, sub, w_in, w_out, which, final_w=None):
    t = x.shape[0]
    single = pl.Buffered(1)
    in_specs = [pl.BlockSpec((TM, D_MODEL), lambda i: (i, 0)),
                _norm_spec(layer, sub),
                _mod_spec(layer, sub),
                pl.BlockSpec((None, None, D_MODEL, 2 * D_FF), lambda i: (layer, which, 0, 0),
                             pipeline_mode=single),
                pl.BlockSpec((None, None, D_FF, D_MODEL), lambda i: (layer, which, 0, 0),
                             pipeline_mode=single)]
    args = [x, norms, mods, w_in, w_out]
    if final_w is not None:
        in_specs.append(pl.BlockSpec((1, D_MODEL), lambda i: (0, 0)))
        args.append(final_w.reshape(1, D_MODEL))
    return pl.pallas_call(
        functools.partial(_ffn_kernel, with_final=final_w is not None),
        grid=(t // TM,),
        in_specs=in_specs,
        out_specs=pl.BlockSpec((TM, D_MODEL), lambda i: (i, 0)),
        out_shape=jax.ShapeDtypeStruct((t, D_MODEL), F32),
        scratch_shapes=[pltpu.VMEM((TM, D_MODEL), BF16), pltpu.VMEM((TM, D_MODEL), F32)],
        compiler_params=_cparams(("parallel",)),
        name="ffn",
    )(*args)


def _proj_kernel(x_ref, g_ref, mod_ref, w_ref, o_ref, h_ref):
    @pl.when(pl.program_id(1) == 0)
    def _():
        step = TM // FFN_ROW_SPLIT
        for r0 in range(0, TM, step):
            _ada_in_rows(x_ref, g_ref, mod_ref, h_ref, r0, r0 + step)
            o_ref[r0:r0 + step, :] = _dot(h_ref[r0:r0 + step, :], w_ref[...])

    @pl.when(pl.program_id(1) != 0)
    def _():
        o_ref[...] = _dot(h_ref[...], w_ref[...])


def _proj(x, norms, mods, layer, w, which, tn):
    t = x.shape[0]
    n = w.shape[2]
    return pl.pallas_call(
        _proj_kernel,
        grid=(t // TM, n // tn),
        in_specs=[pl.BlockSpec((TM, D_MODEL), lambda i, j: (i, 0)),
                  _norm_spec(layer, 1),
                  _mod_spec(layer, 1),
                  pl.BlockSpec((None, D_MODEL, tn), lambda i, j: (which, 0, j))],
        out_specs=pl.BlockSpec((TM, tn), lambda i, j: (i, j)),
        out_shape=jax.ShapeDtypeStruct((t, n), F32),
        scratch_shapes=[pltpu.VMEM((TM, D_MODEL), BF16)],
        compiler_params=_cparams(("parallel", "arbitrary")),
        name="mixer_in_proj",
    )(x, norms, mods, w)


def _out_proj_kernel(*refs, n_parts):
    x_ref, mod_ref = refs[0], refs[1]
    y_refs = refs[2:2 + n_parts]
    w_refs = refs[2 + n_parts:2 + 2 * n_parts]
    o_ref = refs[2 + 2 * n_parts]
    upd = None
    for y_ref, w_ref in zip(y_refs, w_refs):
        part = _dot(y_ref[...].astype(BF16), w_ref[...])
        upd = part if upd is None else upd + part
    _gated_residual_rows(x_ref, upd, mod_ref, o_ref, 1.0, None, 0, TM)


def _out_proj(x, mods, layer, ys, w, which):
    t = x.shape[0]
    n_parts = len(ys)
    kp = ys[0].shape[1]
    in_specs = [pl.BlockSpec((TM, D_MODEL), lambda i: (i, 0)), _mod_spec(layer, 1)]
    in_specs += [pl.BlockSpec((TM, kp), lambda i: (i, 0)) for _ in ys]
    in_specs += [pl.BlockSpec((None, kp, D_MODEL), lambda i, p=p: (which, p, 0)) for p in range(n_parts)]
    return pl.pallas_call(
        functools.partial(_out_proj_kernel, n_parts=n_parts),
        grid=(t // TM,),
        in_specs=in_specs,
        out_specs=pl.BlockSpec((TM, D_MODEL), lambda i: (i, 0)),
        out_shape=jax.ShapeDtypeStruct((t, D_MODEL), F32),
        compiler_params=_cparams(("parallel",)),
        name="mixer_out_proj",
    )(x, mods, *ys, *([w] * n_parts))


RC = CTX_LEN
RET_NCHUNK = LT // RC


def _log_sigmoid(x):
    return jnp.minimum(x, 0.0) - jnp.log1p(jnp.exp(-jnp.abs(x)))


def _ret_kernel(q_ref, k_ref, v_ref, g_ref, cos_ref, sin_ref, dl_ref, o_ref,
                qb_ref, kb_ref, kv_ref, s_ref):
    lg = _log_sigmoid(dl_ref[0])
    lg_f, lg_b = lg[0:1, :], lg[1:2, :]
    cos, sin = cos_ref[...], sin_ref[...]
    q = q_ref[0]
    k = k_ref[0] * (RET_KEY_DIM ** -0.5)
    half = RET_KEY_DIM // 2
    qb_ref[...] = (q * cos + pltpu.roll(q, half, 1) * sin).astype(BF16)
    kr = k * cos + pltpu.roll(k, half, 1) * sin
    kb_ref[...] = kr.astype(BF16)

    pos = lax.broadcasted_iota(jnp.int32, (RC, 1), 0).astype(F32)
    kdec_f = jnp.exp(lg_f * (RC - 1.0 - pos))
    kdec_b = jnp.exp(lg_b * pos)
    qdec_f = jnp.exp(lg_f * (pos + 1.0))
    qdec_b = jnp.exp(lg_b * (RC - pos))
    c_f = jnp.exp(lg_f * RC)
    c_b = jnp.exp(lg_b * RC)
    ii = lax.broadcasted_iota(jnp.int32, (RC, RC), 0)
    jj = lax.broadcasted_iota(jnp.int32, (RC, RC), 1)
    dist = (ii - jj).astype(F32)
    dmat = (jnp.where(ii >= jj, jnp.exp(lg_f * dist), 0.0)
            + jnp.where(ii <= jj, jnp.exp(-lg_b * dist), 0.0))

    tn_dims = (((0,), (0,)), ((), ()))
    for n in range(RET_NCHUNK):
        rows = slice(n * RC, (n + 1) * RC)
        kn = kr[rows]
        vn = v_ref[0, rows, :].astype(BF16)
        kv_ref[0, n] = lax.dot_general((kn * kdec_f).astype(BF16), vn, tn_dims,
                                       preferred_element_type=F32)
        kv_ref[1, n] = lax.dot_general((kn * kdec_b).astype(BF16), vn, tn_dims,
                                       preferred_element_type=F32)

    sf = jnp.zeros((RET_KEY_DIM, RET_VALUE_DIM), F32)
    for n in range(RET_NCHUNK):
        s_ref[n, :, :RET_VALUE_DIM] = sf.astype(BF16)
        sf = sf * c_f + kv_ref[0, n]
    s_ref[0, :, RET_VALUE_DIM:] = jnp.zeros((RET_KEY_DIM, RET_VALUE_DIM), BF16)
    sb = kv_ref[1, 0]
    for n in range(RET_NCHUNK - 1, 0, -1):
        s_ref[n, :, RET_VALUE_DIM:] = sb.astype(BF16)
        sb = sb * c_b + kv_ref[1, n]

    nt_dims = (((1,), (1,)), ((), ()))

    for n in range(RET_NCHUNK):
        rows = slice(n * RC, (n + 1) * RC)
        qn = qb_ref[rows, :]
        s = lax.dot_general(qn, kb_ref[rows, :], nt_dims, preferred_element_type=F32)
        cross = _dot(qn, s_ref[n])
        o = _dot((s * dmat).astype(BF16), v_ref[0, rows, :].astype(BF16))
        o = o + cross[:, :RET_VALUE_DIM] * qdec_f + cross[:, RET_VALUE_DIM:] * qdec_b
        y = o * lax.rsqrt(jnp.mean(o * o, axis=-1, keepdims=True) + EPS)
        o_ref[0, rows, :] = (y * _silu(g_ref[0, rows, :])).astype(BF16)


def _ret_rope_tables():
    inv = ROPE_BASE ** (-jnp.linspace(0.0, 1.0, RET_KEY_DIM // 2, dtype=F32))
    ang = jnp.arange(SEQ, dtype=F32)[:, None] * inv
    ang = jnp.concatenate([jnp.zeros((CTX_LEN, RET_KEY_DIM // 2), F32), ang], axis=0)
    cos = jnp.concatenate([jnp.cos(ang)] * 2, axis=-1)
    sin = jnp.concatenate([-jnp.sin(ang), jnp.sin(ang)], axis=-1)
    return cos, sin


def _ret_mixer(p, decay_logit):
    bsz = p.shape[0]
    cos, sin = _ret_rope_tables()
    dl = decay_logit.T.reshape(RET_HEADS, 2, 1)
    kq = RET_HEADS
    kv = (2 * RET_HEADS * RET_KEY_DIM) // RET_VALUE_DIM
    return pl.pallas_call(
        _ret_kernel,
        grid=(bsz, RET_HEADS),
        in_specs=[pl.BlockSpec((1, LT, RET_KEY_DIM), lambda b, h: (b, 0, h)),
                  pl.BlockSpec((1, LT, RET_KEY_DIM), lambda b, h: (b, 0, kq + h)),
                  pl.BlockSpec((1, LT, RET_VALUE_DIM), lambda b, h: (b, 0, kv + h)),
                  pl.BlockSpec((1, LT, RET_VALUE_DIM), lambda b, h: (b, 0, kv + RET_HEADS + h)),
                  pl.BlockSpec((LT, RET_KEY_DIM), lambda b, h: (0, 0)),
                  pl.BlockSpec((LT, RET_KEY_DIM), lambda b, h: (0, 0)),
                  pl.BlockSpec((1, 2, 1), lambda b, h: (h, 0, 0))],
        out_specs=pl.BlockSpec((1, LT, RET_VALUE_DIM), lambda b, h: (b, 0, h)),
        out_shape=jax.ShapeDtypeStruct((bsz, LT, RET_V_WIDTH), BF16),
        scratch_shapes=[pltpu.VMEM((LT, RET_KEY_DIM), BF16),
                        pltpu.VMEM((LT, RET_KEY_DIM), BF16),
                        pltpu.VMEM((2, RET_NCHUNK, RET_KEY_DIM, RET_VALUE_DIM), F32),
                        pltpu.VMEM((RET_NCHUNK, RET_KEY_DIM, 2 * RET_VALUE_DIM), BF16)],
        compiler_params=_cparams(("parallel", "parallel")),
        name="retention",
    )(p, p, p, p, cos, sin, dl)


TQ = CTX_LEN
DIFF_NQ = LT // TQ
DIFF_Q_SPLIT = 4


def _axial_rope_tile(x, cos, sin_lo, sin_hi):
    q = DIFF_QK_DIM // 4
    return x * cos + pltpu.roll(x, 128 - q, 1) * sin_lo + pltpu.roll(x, q, 1) * sin_hi


def _diff_kernel(q_ref, k_ref, v_ref, cos_ref, slo_ref, shi_ref, lam_ref, w_ref, o_ref,
                 kb_ref, vb_ref, *, lambda_init):
    i = pl.program_id(2)

    @pl.when(i == 0)
    def _():
        kb_ref[...] = _axial_rope_tile(k_ref[0], cos_ref[...], slo_ref[...], shi_ref[...]).astype(BF16)
        ones_col = (lax.broadcasted_iota(jnp.int32, (LT, DIFF_V_DIM), 1) == 0).astype(BF16)
        vb_ref[...] = jnp.concatenate([v_ref[0].astype(BF16), ones_col], axis=1)

    lam = lam_ref[...]
    lam_full = (jnp.exp(jnp.sum(lam[0:1] * lam[1:2], axis=-1, keepdims=True))
                - jnp.exp(jnp.sum(lam[2:3] * lam[3:4], axis=-1, keepdims=True)) + lambda_init)
    r0 = pl.multiple_of(i * TQ, TQ)
    q = q_ref[0] * (DIFF_QK_DIM ** -0.5 * math.log2(math.e))
    qr = _axial_rope_tile(q, cos_ref[pl.ds(r0, TQ), :], slo_ref[pl.ds(r0, TQ), :], shi_ref[pl.ds(r0, TQ), :])
    first_map = lax.broadcasted_iota(jnp.int32, (TQ, DIFF_V_DIM), 1) < DIFF_QK_DIM
    q_maps = (jnp.where(first_map, qr, 0.0).astype(BF16), jnp.where(first_map, 0.0, qr).astype(BF16))
    out_scale = w_ref[...] * (1.0 - lambda_init)

    def attend_rows(r, n_keys):
        qs = jnp.concatenate([q_maps[0][r], q_maps[1][r]], axis=0)
        nr = qs.shape[0] // 2
        s = lax.dot_general(qs, kb_ref[:n_keys, :], (((1,), (1,)), ((), ())),
                            preferred_element_type=F32)
        yield
        e = jnp.exp2(s - jnp.max(s, axis=-1, keepdims=True)).astype(BF16)
        ob = _dot(e, vb_ref[:n_keys, :])
        yield
        ob = ob[:, :DIFF_V_DIM] / ob[:, DIFF_V_DIM:DIFF_V_DIM + 1]
        o = ob[:nr] - lam_full * ob[nr:]
        y = o * lax.rsqrt(jnp.mean(o * o, axis=-1, keepdims=True) + EPS) * out_scale
        o_ref[0, r, :] = y.astype(BF16)

    def attend(n_keys, n_split):
        step = TQ // n_split
        _run_in_lockstep([attend_rows(slice(j * step, (j + 1) * step), n_keys) for j in range(n_split)])

    @pl.when(i == 0)
    def _():
        attend(CTX_LEN, 1)

    @pl.when(i > 0)
    def _():
        attend(LT, DIFF_Q_SPLIT)


def _diff_rope_tables():
    rows = SEQ // GRID_W
    r, cidx = jnp.meshgrid(jnp.arange(rows), jnp.arange(GRID_W), indexing='ij')
    axis_dim = DIFF_QK_DIM // 2
    inv = ROPE_BASE ** (-jnp.arange(0, axis_dim, 2, dtype=F32) / axis_dim)
    ang_r = r.reshape(-1).astype(F32)[:, None] * inv
    ang_c = cidx.reshape(-1).astype(F32)[:, None] * inv
    zero = jnp.zeros_like(ang_r)

    def table(fn_lo, fn_hi):
        one = jnp.concatenate([fn_lo(ang_r), fn_hi(ang_r), fn_lo(ang_c), fn_hi(ang_c)], axis=-1)
        return jnp.concatenate([one, one], axis=-1)

    cos = table(jnp.cos, jnp.cos)
    sin_lo = table(lambda a: -jnp.sin(a), lambda a: zero)
    sin_hi = table(lambda a: zero, jnp.sin)
    ident = [jnp.ones((CTX_LEN, DIFF_V_DIM), F32), jnp.zeros((CTX_LEN, DIFF_V_DIM), F32),
             jnp.zeros((CTX_LEN, DIFF_V_DIM), F32)]
    return [jnp.concatenate([c, t], axis=0) for c, t in zip(ident, (cos, sin_lo, sin_hi))]


def _diff_mixer(p, lam, subln_w, layer_idx):
    bsz = p.shape[0]
    lambda_init = 0.8 - 0.6 * math.exp(-0.3 * layer_idx)
    cos, sin_lo, sin_hi = _diff_rope_tables()
    q0 = 4 * DN_WIDTH // DIFF_V_DIM
    k0 = q0 + DIFF_HEADS
    v0 = k0 + DIFF_HEADS
    tab = pl.BlockSpec((LT, DIFF_V_DIM), lambda b, h, i: (0, 0))
    return pl.pallas_call(
        functools.partial(_diff_kernel, lambda_init=lambda_init),
        grid=(bsz, DIFF_HEADS, DIFF_NQ),
        in_specs=[pl.BlockSpec((1, TQ, DIFF_V_DIM), lambda b, h, i: (b, i, q0 + h)),
                  pl.BlockSpec((1, LT, DIFF_V_DIM), lambda b, h, i: (b, 0, k0 + h)),
                  pl.BlockSpec((1, LT, DIFF_V_DIM), lambda b, h, i: (b, 0, v0 + h)),
                  tab, tab, tab,
                  pl.BlockSpec((4, DIFF_QK_DIM), lambda b, h, i: (0, 0)),
                  pl.BlockSpec((1, DIFF_V_DIM), lambda b, h, i: (0, 0))],
        out_specs=pl.BlockSpec((1, TQ, DIFF_V_DIM), lambda b, h, i: (b, i, h)),
        out_shape=jax.ShapeDtypeStruct((bsz, LT, DIFF_WIDTH), BF16),
        scratch_shapes=[pltpu.VMEM((LT, DIFF_V_DIM), BF16), pltpu.VMEM((LT, 2 * DIFF_V_DIM), BF16)],
        compiler_params=_cparams(("parallel", "parallel", "arbitrary")),
        name="diff_attention",
    )(p, p, p, cos, sin_lo, sin_hi, lam, subln_w.reshape(1, DIFF_V_DIM))


DC = DN_CHUNK
DN_NCHUNK = LT // DC
DN_CTX_CHUNKS = CTX_LEN // DC


DN_PREP_GROUP = 9


def _split2(x):
    hi = x.astype(BF16)
    return hi, (x - hi.astype(F32)).astype(BF16)


def _stack3(a_bf16):
    return jnp.concatenate([a_bf16, a_bf16, a_bf16], axis=1)


def _mm_exact_lhs(a3_bf16, x):
    h1 = x.astype(BF16)
    r1 = x - h1.astype(F32)
    h2 = r1.astype(BF16)
    h3 = (r1 - h2.astype(F32)).astype(BF16)
    return _dot(a3_bf16, jnp.concatenate([h1, h2, h3], axis=0))


def _softplus(x):
    return jnp.maximum(x, 0.0) + jnp.log1p(jnp.exp(-jnp.abs(x)))


def _dn_kernel(q_ref, k_ref, v_ref, z_ref, gt_ref, cq_ref, ck_ref, cv_ref, prm_ref, nw_ref, o_ref,
               qn_ref, kn_ref, vn_ref, la_ref, be_ref, wq_ref, u_ref, qk_ref, b_ref, p_ref, e_ref,
               sst_ref, s_ref):
    h = pl.program_id(1)
    tpos = lax.broadcasted_iota(jnp.int32, (LT, 1), 0)
    has_prev = jnp.logical_and(tpos != 0, tpos != CTX_LEN)
    has_next = jnp.logical_and(tpos != CTX_LEN - 1, tpos != LT - 1)

    def conv_silu(x_ref, c_ref):
        x = x_ref[0]
        w = c_ref[...]
        prev = jnp.where(has_prev, pltpu.roll(x, 1, 0), 0.0)
        nxt = jnp.where(has_next, pltpu.roll(x, LT - 1, 0), 0.0)
        return _silu(prev * w[0:1] + x * w[1:2] + nxt * w[2:3])

    def l2n(t):
        return t * lax.rsqrt(jnp.sum(t * t, axis=-1, keepdims=True) + EPS)

    qn_ref[...] = l2n(conv_silu(q_ref, cq_ref)) * (DN_HEAD_DIM ** -0.5)
    kn_ref[...] = l2n(conv_silu(k_ref, ck_ref))
    vn_ref[...] = conv_silu(v_ref, cv_ref)

    gates = gt_ref[0]
    glane = lax.broadcasted_iota(jnp.int32, gates.shape, 1)
    log_a_all = -jnp.exp(prm_ref[0:1, :]) * _softplus(gates + prm_ref[1:2, :])
    beta_all = 1.0 / (1.0 + jnp.exp(-gates))
    for d in range(2):
        log_a = jnp.sum(jnp.where(glane == d * DN_HEADS + h, log_a_all, 0.0), axis=-1, keepdims=True)
        beta = jnp.sum(jnp.where(glane == (2 + d) * DN_HEADS + h, beta_all, 0.0), axis=-1, keepdims=True)
        la_ref[d] = jnp.broadcast_to(log_a, (LT, DN_HEAD_DIM))
        be_ref[d] = jnp.broadcast_to(beta, (LT, DN_HEAD_DIM))

    row = lax.broadcasted_iota(jnp.int32, (DC, 2 * DC), 0)
    lane = lax.broadcasted_iota(jnp.int32, (DC, 2 * DC), 1)
    col = lane & (DC - 1)
    isf = lane < DC
    eye2 = row == col
    isb = jnp.logical_not(isf)
    incl = jnp.logical_or(jnp.logical_and(isf, row >= col), jnp.logical_and(isb, row <= col))
    strict = jnp.logical_or(jnp.logical_and(isf, row > col), jnp.logical_and(isb, row < col))
    r64 = lax.broadcasted_iota(jnp.int32, (DC, DC), 0)
    c64 = lax.broadcasted_iota(jnp.int32, (DC, DC), 1)
    cum_lhs = _stack3(jnp.concatenate([(r64 >= c64).astype(BF16), jnp.ones((DC, DC), BF16)], axis=0))
    ones_lhs = _stack3(jnp.ones((DC, DC), BF16))
    eye_f = eye2.astype(F32)
    same16 = (row >> 4) == (col >> 4)
    in32_not16 = jnp.logical_and((row >> 5) == (col >> 5), jnp.logical_not(same16))
    not32 = (row >> 5) != (col >> 5)
    nt_dims = (((1,), (1,)), ((), ()))

    def block_diag(x2):
        return jnp.concatenate([jnp.where(isf[:x2.shape[0]], x2, 0.0), jnp.where(isb[:x2.shape[0]], x2, 0.0)],
                               axis=0)

    def mm_pair(x2, y2):
        xh, xl = _split2(x2)
        yh, yl = _split2(block_diag(y2))
        rhs = jnp.concatenate([jnp.concatenate([yh, yl], axis=1),
                               jnp.concatenate([yh, jnp.zeros_like(yl)], axis=1)], axis=0)
        out = _dot(jnp.concatenate([xh, xl], axis=1), rhs)
        return out[:, :2 * DC] + out[:, 2 * DC:]

    def prep_chunk(n):
        r0 = pl.multiple_of(n * DC, DC)
        rows = pl.ds(r0, DC)
        qn, kn, vn = qn_ref[rows, :], kn_ref[rows, :], vn_ref[rows, :]
        la_f, la_b = la_ref[0, rows, :], la_ref[1, rows, :]
        be_f, be_b = be_ref[0, rows, :], be_ref[1, rows, :]
        pt_f = _mm_exact_lhs(cum_lhs, la_f)
        pt_b = _mm_exact_lhs(cum_lhs, la_b)
        kb16 = kn.astype(BF16)
        kcat = jnp.concatenate([kb16, kb16], axis=0)
        kk2 = lax.dot_general(kb16, kcat, nt_dims, preferred_element_type=F32)
        qk2 = lax.dot_general(qn.astype(BF16), kcat, nt_dims, preferred_element_type=F32)
        yield
        g_f, tot_f = pt_f[:DC], pt_f[DC:]
        tot_b = pt_b[DC:]
        g_b = tot_b - pt_b[:DC] + la_b
        gg = jnp.where(isf, g_f, g_b)
        g_row = _mm_exact_lhs(ones_lhs, jnp.where(eye2, gg, 0.0))
        yield
        gam = jnp.where(incl, jnp.exp(gg - g_row), 0.0)
        a2 = jnp.where(strict, kk2 * jnp.where(isf, be_f, be_b) * gam, 0.0)
        dmat = jnp.where(same16, a2, 0.0)
        d2 = mm_pair(dmat, dmat)
        yield
        d4 = mm_pair(d2, d2)
        inv = mm_pair(eye_f - dmat, eye_f + d2)
        yield
        d8 = mm_pair(d4, d4)
        inv = mm_pair(inv, eye_f + d4)
        yield
        inv = mm_pair(inv, eye_f + d8)
        yield
        t = mm_pair(inv, jnp.where(in32_not16, a2, 0.0))
        yield
        inv = inv - mm_pair(t, inv)
        yield
        t = mm_pair(inv, jnp.where(not32, a2, 0.0))
        yield
        inv = inv - mm_pair(t, inv)
        yield
        eg_f, eg_b = jnp.exp(g_f), jnp.exp(g_b)
        zeros_rhs = jnp.zeros((DC, 2 * DN_HEAD_DIM), BF16)
        rhs_f = jnp.concatenate([vn * be_f, kn * be_f * eg_f], axis=1).astype(BF16)
        rhs_b = jnp.concatenate([vn * be_b, kn * be_b * eg_b], axis=1).astype(BF16)
        rhs_bd = jnp.concatenate([jnp.concatenate([rhs_f, zeros_rhs], axis=1),
                                  jnp.concatenate([zeros_rhs, rhs_b], axis=1)], axis=0)
        ih, il = _split2(inv)
        uw = _dot(jnp.concatenate([ih, il], axis=1), jnp.concatenate([rhs_bd, rhs_bd], axis=0))
        uw_f, uw_b = uw[:, :2 * DN_HEAD_DIM], uw[:, 2 * DN_HEAD_DIM:]
        yield
        kdt_f = jnp.transpose(kn * jnp.exp(tot_f - g_f)).astype(BF16)
        kdt_b = jnp.transpose(kn * jnp.exp(tot_b - g_b)).astype(BF16)
        bp_f = _dot(kdt_f, uw_f.astype(BF16))
        bp_b = _dot(kdt_b, uw_b.astype(BF16))
        yield
        u_ref[n, 0] = uw_f[:, :DN_HEAD_DIM]
        u_ref[n, 1] = uw_b[:, :DN_HEAD_DIM]
        wq_ref[n, 0] = jnp.concatenate([uw_f[:, DN_HEAD_DIM:], qn * eg_f], axis=0).astype(BF16)
        wq_ref[n, 1] = jnp.concatenate([uw_b[:, DN_HEAD_DIM:], qn * eg_b], axis=0).astype(BF16)
        qk_ref[n] = (qk2 * gam).astype(BF16)
        b_ref[n, 0] = bp_f[:, :DN_HEAD_DIM]
        b_ref[n, 1] = bp_b[:, :DN_HEAD_DIM]
        p_ref[n, 0] = bp_f[:, DN_HEAD_DIM:].astype(BF16)
        p_ref[n, 1] = bp_b[:, DN_HEAD_DIM:].astype(BF16)
        e_ref[n, 0] = jnp.exp(tot_f[0:8])
        e_ref[n, 1] = jnp.exp(tot_b[0:8])

    def prep(i, carry):
        _run_in_lockstep([prep_chunk(i * DN_PREP_GROUP + j) for j in range(DN_PREP_GROUP)])
        return carry

    lax.fori_loop(0, DN_NCHUNK // DN_PREP_GROUP, prep, 0)

    s_ref[...] = jnp.zeros(s_ref.shape, F32)

    def scan_dir(d, n):
        state = s_ref[d]
        s16 = state.astype(BF16)
        sst_ref[n, d] = s16
        ps = _dot(p_ref[n, d], s16)
        yield
        s_ref[d] = state * e_ref[n, d][0:1, :] - ps + b_ref[n, d]

    def scan_step(s, carry):
        nb = jnp.where(s < DN_CTX_CHUNKS, DN_CTX_CHUNKS - 1 - s, DN_NCHUNK - 1 + DN_CTX_CHUNKS - s)
        _run_in_lockstep([scan_dir(0, s), scan_dir(1, nb)])
        return carry

    lax.fori_loop(0, DN_NCHUNK, scan_step, 0)

    def finish_chunk(n):
        rows = pl.ds(pl.multiple_of(n * DC, DC), DC)
        sw_f = _dot(wq_ref[n, 0], sst_ref[n, 0])
        sw_b = _dot(wq_ref[n, 1], sst_ref[n, 1])
        yield
        vcat = jnp.concatenate([u_ref[n, 0] - sw_f[:DC], u_ref[n, 1] - sw_b[:DC]], axis=0).astype(BF16)
        intra = _dot(qk_ref[n], vcat)
        yield
        o = sw_f[DC:] + sw_b[DC:] + intra
        y = o * lax.rsqrt(jnp.mean(o * o, axis=-1, keepdims=True) + EPS) * nw_ref[...]
        o_ref[0, rows, :] = (y * _silu(z_ref[0, rows, :])).astype(BF16)

    def finish(i, carry):
        _run_in_lockstep([finish_chunk(i * DN_PREP_GROUP + j) for j in range(DN_PREP_GROUP)])
        return carry

    lax.fori_loop(0, DN_NCHUNK // DN_PREP_GROUP, finish, 0)


def _dn_mixer(p, conv_w, a_log, dt_bias, norm_w):
    bsz = p.shape[0]
    hd = DN_HEAD_DIM
    prm = jnp.pad(jnp.stack([a_log.reshape(-1), dt_bias.reshape(-1)]), ((0, 0), (0, hd - 2 * DN_HEADS)))
    gate_blk = AB_MAIN_COLS // hd
    seq = lambda off: pl.BlockSpec((1, LT, hd), lambda b, h: (b, 0, off + h))
    cw = lambda off: pl.BlockSpec((CONV_WIDTH, hd), lambda b, h: (0, off + h))
    return pl.pallas_call(
        _dn_kernel,
        grid=(bsz, DN_HEADS),
        in_specs=[seq(0), seq(DN_HEADS), seq(2 * DN_HEADS), seq(3 * DN_HEADS),
                  pl.BlockSpec((1, LT, hd), lambda b, h: (b, 0, gate_blk)),
                  cw(0), cw(DN_HEADS), cw(2 * DN_HEADS),
                  pl.BlockSpec((2, hd), lambda b, h: (0, 0)),
                  pl.BlockSpec((1, hd), lambda b, h: (0, 0))],
        out_specs=pl.BlockSpec((1, LT, hd), lambda b, h: (b, 0, h)),
        out_shape=jax.ShapeDtypeStruct((bsz, LT, DN_WIDTH), BF16),
        scratch_shapes=[pltpu.VMEM((LT, hd), F32), pltpu.VMEM((LT, hd), F32), pltpu.VMEM((LT, hd), F32),
                        pltpu.VMEM((2, LT, hd), F32), pltpu.VMEM((2, LT, hd), F32),
                        pltpu.VMEM((DN_NCHUNK, 2, 2 * DC, hd), BF16),
                        pltpu.VMEM((DN_NCHUNK, 2, DC, hd), F32),
                        pltpu.VMEM((DN_NCHUNK, DC, 2 * DC), BF16),
                        pltpu.VMEM((DN_NCHUNK, 2, hd, hd), F32),
                        pltpu.VMEM((DN_NCHUNK, 2, hd, hd), BF16),
                        pltpu.VMEM((DN_NCHUNK, 2, 8, hd), F32),
                        pltpu.VMEM((DN_NCHUNK, 2, hd, hd), BF16),
                        pltpu.VMEM((2, hd, hd), F32)],
        compiler_params=_cparams(("parallel", "parallel")),
        name="gated_deltanet",
    )(p, p, p, p, p, conv_w, conv_w, conv_w, prm, norm_w.reshape(1, hd))


def _split_ab_weight(w):
    o_ab = 4 * DN_WIDTH
    n_ab = 4 * DN_HEADS
    main = jnp.concatenate([w[..., :o_ab], w[..., o_ab + n_ab:]], axis=-1)
    pad = [(0, 0)] * (w.ndim - 1) + [(0, AB_GATE_COLS - n_ab)]
    return jnp.concatenate([main, jnp.pad(w[..., o_ab:o_ab + n_ab], pad)], axis=-1)


def kernel(x, c, ctx, c_ctx, ada_w, ada_b, norm_w, final_norm_w, ffn_w_in, ffn_w_out, ab_w_in,
           ab_conv_w, dn_A_log, dn_dt_bias, dn_norm_w, diff_lambda, diff_subln_w, ab_w_out,
           ret_w_in, ret_decay_logit, ret_w_out):
    B = x.shape[0]
    xs = jnp.concatenate([ctx, x], axis=1).reshape(B * LT, D_MODEL)
    cvec = jnp.concatenate([c, c_ctx[None], jnp.zeros((MOD_ROWS - B - 1, D_MODEL), F32)], axis=0)
    mods_all = _ada_mods(cvec, ada_w, ada_b)
    mods_all = mods_all.reshape(DEPTH, MOD_ROWS, N_MOD, D_MODEL).transpose(0, 2, 1, 3)
    norms = norm_w.reshape(DEPTH, 3, 1, D_MODEL)
    ffn_w_in_b = ffn_w_in.astype(BF16)
    ffn_w_out_b = ffn_w_out.astype(BF16)
    ab_w_in_b = _split_ab_weight(ab_w_in).astype(BF16)
    ab_w_out_b = ab_w_out.astype(BF16)
    ret_w_in_b = ret_w_in.astype(BF16)
    ret_w_out_b = ret_w_out.astype(BF16)
    for l in range(DEPTH):
        last = l == DEPTH - 1
        xs = _ffn(xs, norms, mods_all, l, 0, ffn_w_in_b, ffn_w_out_b, 0)
        i = l // 2
        if l % 2 == 0:
            p = _proj(xs, norms, mods_all, l, ab_w_in_b, i, AB_PROJ_TN).reshape(B, LT, -1)
            y_dn = _dn_mixer(p, ab_conv_w[i], dn_A_log[i], dn_dt_bias[i], dn_norm_w[i])
            y_df = _diff_mixer(p, diff_lambda[i], diff_subln_w[i], l)
            xs = _out_proj(xs, mods_all, l, [y_dn.reshape(B * LT, DN_WIDTH), y_df.reshape(B * LT, DIFF_WIDTH)],
                           ab_w_out_b, i)
        else:
            p = _proj(xs, norms, mods_all, l, ret_w_in_b, i, RET_PROJ_TN).reshape(B, LT, -1)
            y = _ret_mixer(p, ret_decay_logit[i])
            xs = _out_proj(xs, mods_all, l, [y.reshape(B * LT, RET_V_WIDTH)], ret_w_out_b, i)
        xs = _ffn(xs, norms, mods_all, l, 2, ffn_w_in_b, ffn_w_out_b, 1,
                  final_w=final_norm_w if last else None)
    return xs.reshape(B, LT, D_MODEL)[:, CTX_LEN:]
```

```python
import functools
import math

import jax
import jax.numpy as jnp
from jax import lax
from jax.experimental import pallas as pl
from jax.experimental.pallas import tpu as pltpu

D_MODEL = 1024
BATCH = 8
SEQ = 2048
DEPTH = 4
GRID_W = 64
CTX_LEN = 256
EPS = 1e-6
N_MOD = 9
DN_HEADS = 4
DN_HEAD_DIM = 128
DN_WIDTH = DN_HEADS * DN_HEAD_DIM
CONV_WIDTH = 3
DN_CHUNK = 64
DIFF_HEADS = 4
DIFF_QK_DIM = 64
DIFF_V_DIM = 2 * DIFF_QK_DIM
DIFF_WIDTH = DIFF_HEADS * DIFF_V_DIM
ROPE_BASE = 10000.0
RET_HEADS = 8
RET_KEY_DIM = D_MODEL // RET_HEADS
RET_VALUE_DIM = 2 * RET_KEY_DIM
RET_V_WIDTH = RET_HEADS * RET_VALUE_DIM
D_FF = 2816
RET_IN = 2 * RET_HEADS * RET_KEY_DIM + 2 * RET_V_WIDTH
AB_MAIN_COLS = 4 * DN_WIDTH + 3 * DIFF_WIDTH
AB_GATE_COLS = 256
AB_PROJ_TN = (AB_MAIN_COLS + AB_GATE_COLS) // 3
RET_PROJ_TN = RET_IN // 3

F32 = jnp.float32
BF16 = jnp.bfloat16

LT = CTX_LEN + SEQ
TM = LT // 2
TILES_PER_BATCH = LT // TM
MOD_ROWS = 16
CTX_ROW = BATCH
V7X_VMEM_LIMIT = 56 * 1024 * 1024
FF_CHUNK = 256
FFN_ROW_SPLIT = 2


def _cparams(sem):
    return pltpu.CompilerParams(dimension_semantics=sem, vmem_limit_bytes=V7X_VMEM_LIMIT)


def _silu(x):
    return x * (1.0 / (1.0 + jnp.exp(-x)))


def _dot(a, b):
    return jnp.dot(a, b, preferred_element_type=F32)


def _run_in_lockstep(gens):
    while gens:
        alive = []
        for g in gens:
            try:
                next(g)
                alive.append(g)
            except StopIteration:
                pass
        gens = alive


def _ada_kernel(s_ref, w_ref, b_ref, o_ref):
    s = _silu(s_ref[...]).astype(BF16)
    o_ref[0] = _dot(s, w_ref[0].astype(BF16)) + b_ref[0]


def _ada_mods(cvec, ada_w, ada_b):
    tn = 1024
    n = N_MOD * D_MODEL
    return pl.pallas_call(
        _ada_kernel,
        grid=(DEPTH, n // tn),
        in_specs=[pl.BlockSpec((MOD_ROWS, D_MODEL), lambda l, j: (0, 0)),
                  pl.BlockSpec((1, D_MODEL, tn), lambda l, j: (l, 0, j)),
                  pl.BlockSpec((1, 1, tn), lambda l, j: (l, 0, j))],
        out_specs=pl.BlockSpec((1, MOD_ROWS, tn), lambda l, j: (l, 0, j)),
        out_shape=jax.ShapeDtypeStruct((DEPTH, MOD_ROWS, n), F32),
        compiler_params=_cparams(("parallel", "parallel")),
        name="ada_mods",
    )(cvec, ada_w, ada_b.reshape(DEPTH, 1, n))


def _tile_mod_rows(mod_ref, k):
    i = pl.program_id(0)
    lat = mod_ref[k, pl.ds(i // TILES_PER_BATCH, 1), :]
    head = jnp.where(i % TILES_PER_BATCH == 0, mod_ref[k, CTX_ROW:CTX_ROW + 1, :], lat)
    return head, lat


def _mod_row_blocks(r0, r1):
    if r0 < CTX_LEN < r1:
        return [(r0, CTX_LEN, 0), (CTX_LEN, r1, 1)]
    return [(r0, r1, 0 if r1 <= CTX_LEN else 1)]


def _ada_in_rows(x_ref, g_ref, mod_ref, h_ref, r0, r1):
    shift, scale = _tile_mod_rows(mod_ref, 0), _tile_mod_rows(mod_ref, 1)
    for a, b, k in _mod_row_blocks(r0, r1):
        x = x_ref[a:b, :]
        y = x * lax.rsqrt(jnp.mean(x * x, axis=-1, keepdims=True) + EPS) * g_ref[...]
        h_ref[a:b, :] = (y * (1.0 + scale[k]) + shift[k]).astype(BF16)


def _gated_residual_rows(x_ref, upd_ref, mod_ref, o_ref, gate_scale, final_w_ref, r0, r1):
    gate = _tile_mod_rows(mod_ref, 2)
    for a, b, k in _mod_row_blocks(r0, r1):
        v = x_ref[a:b, :] + (gate_scale * gate[k]) * upd_ref[a:b, :]
        if final_w_ref is not None:
            v = v * lax.rsqrt(jnp.mean(v * v, axis=-1, keepdims=True) + EPS) * final_w_ref[...]
        o_ref[a:b, :] = v


def _mod_spec(layer, sub):
    return pl.BlockSpec((None, 3, MOD_ROWS, D_MODEL), lambda *idx: (layer, sub, 0, 0))


def _norm_spec(layer, sub):
    return pl.BlockSpec((None, None, 1, D_MODEL), lambda *idx: (layer, sub, 0, 0))


def _ffn_kernel(*refs, with_final):
    if with_final:
        x_ref, g_ref, mod_ref, win_ref, wout_ref, fw_ref, o_ref, h_ref, acc_ref = refs
    else:
        x_ref, g_ref, mod_ref, win_ref, wout_ref, o_ref, h_ref, acc_ref = refs
        fw_ref = None
    step = TM // FFN_ROW_SPLIT
    for r0 in range(0, TM, step):
        r1 = r0 + step
        _ada_in_rows(x_ref, g_ref, mod_ref, h_ref, r0, r1)
        for f in range(D_FF // FF_CHUNK):
            h = h_ref[r0:r1, :]
            gg = _dot(h, win_ref[:, f * FF_CHUNK:(f + 1) * FF_CHUNK])
            uu = _dot(h, win_ref[:, D_FF + f * FF_CHUNK:D_FF + (f + 1) * FF_CHUNK])
            a = (_silu(gg) * uu).astype(BF16)
            part = _dot(a, wout_ref[f * FF_CHUNK:(f + 1) * FF_CHUNK, :])
            if f == 0:
                acc_ref[r0:r1, :] = part
            else:
                acc_ref[r0:r1, :] += part
        _gated_residual_rows(x_ref, acc_ref, mod_ref, o_ref, 0.5, fw_ref, r0, r1)


def _ffn(x, norms, mods, layer, sub, w_in, w_out, which, final_w=None):
    t = x.shape[0]
    single = pl.Buffered(1)
    in_specs = [pl.BlockSpec((TM, D_MODEL), lambda i: (i, 0)),
                _norm_spec(layer, sub),
                _mod_spec(layer, sub),
                pl.BlockSpec((None, None, D_MODEL, 2 * D_FF), lambda i: (layer, which, 0, 0),
                             pipeline_mode=single),
                pl.BlockSpec((None, None, D_FF, D_MODEL), lambda i: (layer, which, 0, 0),
                             pipeline_mode=single)]
    args = [x, norms, mods, w_in, w_out]
    if final_w is not None:
        in_specs.append(pl.BlockSpec((1, D_MODEL), lambda i: (0, 0)))
        args.append(final_w.reshape(1, D_MODEL))
    return pl.pallas_call(
        functools.partial(_ffn_kernel, with_final=final_w is not None),
        grid=(t // TM,),
        in_specs=in_specs,
        out_specs=pl.BlockSpec((TM, D_MODEL), lambda i: (i, 0)),
        out_shape=jax.ShapeDtypeStruct((t, D_MODEL), F32),
        scratch_shapes=[pltpu.VMEM((TM, D_MODEL), BF16), pltpu.VMEM((TM, D_MODEL), F32)],
        compiler_params=_cparams(("parallel",)),
        name="ffn",
    )(*args)


def _proj_kernel(x_ref, g_ref, mod_ref, w_ref, o_ref, h_ref):
    @pl.when(pl.program_id(1) == 0)
    def _():
        step = TM // FFN_ROW_SPLIT
        for r0 in range(0, TM, step):
            _ada_in_rows(x_ref, g_ref, mod_ref, h_ref, r0, r0 + step)
            o_ref[r0:r0 + step, :] = _dot(h_ref[r0:r0 + step, :], w_ref[...])

    @pl.when(pl.program_id(1) != 0)
    def _():
        o_ref[...] = _dot(h_ref[...], w_ref[...])


def _proj(x, norms, mods, layer, w, which, tn):
    t = x.shape[0]
    n = w.shape[2]
    return pl.pallas_call(
        _proj_kernel,
        grid=(t // TM, n // tn),
        in_specs=[pl.BlockSpec((TM, D_MODEL), lambda i, j: (i, 0)),
                  _norm_spec(layer, 1),
                  _mod_spec(layer, 1),
                  pl.BlockSpec((None, D_MODEL, tn), lambda i, j: (which, 0, j))],
        out_specs=pl.BlockSpec((TM, tn), lambda i, j: (i, j)),
        out_shape=jax.ShapeDtypeStruct((t, n), F32),
        scratch_shapes=[pltpu.VMEM((TM, D_MODEL), BF16)],
        compiler_params=_cparams(("parallel", "arbitrary")),
        name="mixer_in_proj",
    )(x, norms, mods, w)


def _out_proj_kernel(*refs, n_parts):
    x_ref, mod_ref = refs[0], refs[1]
    y_refs = refs[2:2 + n_parts]
    w_refs = refs[2 + n_parts:2 + 2 * n_parts]
    o_ref = refs[2 + 2 * n_parts]
    upd = None
    for y_ref, w_ref in zip(y_refs, w_refs):
        part = _dot(y_ref[...].astype(BF16), w_ref[...])
        upd = part if upd is None else upd + part
    _gated_residual_rows(x_ref, upd, mod_ref, o_ref, 1.0, None, 0, TM)


def _out_proj(x, mods, layer, ys, w, which):
    t = x.shape[0]
    n_parts = len(ys)
    kp = ys[0].shape[1]
    in_specs = [pl.BlockSpec((TM, D_MODEL), lambda i: (i, 0)), _mod_spec(layer, 1)]
    in_specs += [pl.BlockSpec((TM, kp), lambda i: (i, 0)) for _ in ys]
    in_specs += [pl.BlockSpec((None, kp, D_MODEL), lambda i, p=p: (which, p, 0)) for p in range(n_parts)]
    return pl.pallas_call(
        functools.partial(_out_proj_kernel, n_parts=n_parts),
        grid=(t // TM,),
        in_specs=in_specs,
        out_specs=pl.BlockSpec((TM, D_MODEL), lambda i: (i, 0)),
        out_shape=jax.ShapeDtypeStruct((t, D_MODEL), F32),
        compiler_params=_cparams(("parallel",)),
        name="mixer_out_proj",
    )(x, mods, *ys, *([w] * n_parts))


RC = CTX_LEN
RET_NCHUNK = LT // RC


def _log_sigmoid(x):
    return jnp.minimum(x, 0.0) - jnp.log1p(jnp.exp(-jnp.abs(x)))


def _ret_kernel(q_ref, k_ref, v_ref, g_ref, cos_ref, sin_ref, dl_ref, o_ref,
                qb_ref, kb_ref, kv_ref, s_ref):
    lg = _log_sigmoid(dl_ref[0])
    lg_f, lg_b = lg[0:1, :], lg[1:2, :]
    cos, sin = cos_ref[...], sin_ref[...]
    q = q_ref[0]
    k = k_ref[0] * (RET_KEY_DIM ** -0.5)
    half = RET_KEY_DIM // 2
    qb_ref[...] = (q * cos + pltpu.roll(q, half, 1) * sin).astype(BF16)
    kr = k * cos + pltpu.roll(k, half, 1) * sin
    kb_ref[...] = kr.astype(BF16)

    pos = lax.broadcasted_iota(jnp.int32, (RC, 1), 0).astype(F32)
    kdec_f = jnp.exp(lg_f * (RC - 1.0 - pos))
    kdec_b = jnp.exp(lg_b * pos)
    qdec_f = jnp.exp(lg_f * (pos + 1.0))
    qdec_b = jnp.exp(lg_b * (RC - pos))
    c_f = jnp.exp(lg_f * RC)
    c_b = jnp.exp(lg_b * RC)
    ii = lax.broadcasted_iota(jnp.int32, (RC, RC), 0)
    jj = lax.broadcasted_iota(jnp.int32, (RC, RC), 1)
    dist = (ii - jj).astype(F32)
    dmat = (jnp.where(ii >= jj, jnp.exp(lg_f * dist), 0.0)
            + jnp.where(ii <= jj, jnp.exp(-lg_b * dist), 0.0))

    tn_dims = (((0,), (0,)), ((), ()))
    for n in range(RET_NCHUNK):
        rows = slice(n * RC, (n + 1) * RC)
        kn = kr[rows]
        vn = v_ref[0, rows, :].astype(BF16)
        kv_ref[0, n] = lax.dot_general((kn * kdec_f).astype(BF16), vn, tn_dims,
                                       preferred_element_type=F32)
        kv_ref[1, n] = lax.dot_general((kn * kdec_b).astype(BF16), vn, tn_dims,
                                       preferred_element_type=F32)

    sf = jnp.zeros((RET_KEY_DIM, RET_VALUE_DIM), F32)
    for n in range(RET_NCHUNK):
        s_ref[n, :, :RET_VALUE_DIM] = sf.astype(BF16)
        sf = sf * c_f + kv_ref[0, n]
    s_ref[0, :, RET_VALUE_DIM:] = jnp.zeros((RET_KEY_DIM, RET_VALUE_DIM), BF16)
    sb = kv_ref[1, 0]
    for n in range(RET_NCHUNK - 1, 0, -1):
        s_ref[n, :, RET_VALUE_DIM:] = sb.astype(BF16)
        sb = sb * c_b + kv_ref[1, n]

    nt_dims = (((1,), (1,)), ((), ()))

    for n in range(RET_NCHUNK):
        rows = slice(n * RC, (n + 1) * RC)
        qn = qb_ref[rows, :]
        s = lax.dot_general(qn, kb_ref[rows, :], nt_dims, preferred_element_type=F32)
        cross = _dot(qn, s_ref[n])
        o = _dot((s * dmat).astype(BF16), v_ref[0, rows, :].astype(BF16))
        o = o + cross[:, :RET_VALUE_DIM] * qdec_f + cross[:, RET_VALUE_DIM:] * qdec_b
        y = o * lax.rsqrt(jnp.mean(o * o, axis=-1, keepdims=True) + EPS)
        o_ref[0, rows, :] = (y * _silu(g_ref[0, rows, :])).astype(BF16)


def _ret_rope_tables():
    inv = ROPE_BASE ** (-jnp.linspace(0.0, 1.0, RET_KEY_DIM // 2, dtype=F32))
    ang = jnp.arange(SEQ, dtype=F32)[:, None] * inv
    ang = jnp.concatenate([jnp.zeros((CTX_LEN, RET_KEY_DIM // 2), F32), ang], axis=0)
    cos = jnp.concatenate([jnp.cos(ang)] * 2, axis=-1)
    sin = jnp.concatenate([-jnp.sin(ang), jnp.sin(ang)], axis=-1)
    return cos, sin


def _ret_mixer(p, decay_logit):
    bsz = p.shape[0]
    cos, sin = _ret_rope_tables()
    dl = decay_logit.T.reshape(RET_HEADS, 2, 1)
    kq = RET_HEADS
    kv = (2 * RET_HEADS * RET_KEY_DIM) // RET_VALUE_DIM
    return pl.pallas_call(
        _ret_kernel,
        grid=(bsz, RET_HEADS),
        in_specs=[pl.BlockSpec((1, LT, RET_KEY_DIM), lambda b, h: (b, 0, h)),
                  pl.BlockSpec((1, LT, RET_KEY_DIM), lambda b, h: (b, 0, kq + h)),
                  pl.BlockSpec((1, LT, RET_VALUE_DIM), lambda b, h: (b, 0, kv + h)),
                  pl.BlockSpec((1, LT, RET_VALUE_DIM), lambda b, h: (b, 0, kv + RET_HEADS + h)),
                  pl.BlockSpec((LT, RET_KEY_DIM), lambda b, h: (0, 0)),
                  pl.BlockSpec((LT, RET_KEY_DIM), lambda b, h: (0, 0)),
                  pl.BlockSpec((1, 2, 1), lambda b, h: (h, 0, 0))],
        out_specs=pl.BlockSpec((1, LT, RET_VALUE_DIM), lambda b, h: (b, 0, h)),
        out_shape=jax.ShapeDtypeStruct((bsz, LT, RET_V_WIDTH), BF16),
        scratch_shapes=[pltpu.VMEM((LT, RET_KEY_DIM), BF16),
                        pltpu.VMEM((LT, RET_KEY_DIM), BF16),
                        pltpu.VMEM((2, RET_NCHUNK, RET_KEY_DIM, RET_VALUE_DIM), F32),
                        pltpu.VMEM((RET_NCHUNK, RET_KEY_DIM, 2 * RET_VALUE_DIM), BF16)],
        compiler_params=_cparams(("parallel", "parallel")),
        name="retention",
    )(p, p, p, p, cos, sin, dl)


TQ = CTX_LEN
DIFF_NQ = LT // TQ
DIFF_Q_SPLIT = 4


def _axial_rope_tile(x, cos, sin_lo, sin_hi):
    q = DIFF_QK_DIM // 4
    return x * cos + pltpu.roll(x, 128 - q, 1) * sin_lo + pltpu.roll(x, q, 1) * sin_hi


def _diff_kernel(q_ref, k_ref, v_ref, cos_ref, slo_ref, shi_ref, lam_ref, w_ref, o_ref,
                 kb_ref, vb_ref, *, lambda_init):
    kb_ref[...] = _axial_rope_tile(k_ref[0], cos_ref[...], slo_ref[...], shi_ref[...]).astype(BF16)
    ones_col = (lax.broadcasted_iota(jnp.int32, (LT, DIFF_V_DIM), 1) == 0).astype(BF16)
    vb_ref[...] = jnp.concatenate([v_ref[0].astype(BF16), ones_col], axis=1)

    lam = lam_ref[...]
    lam_full = (jnp.exp(jnp.sum(lam[0:1] * lam[1:2], axis=-1, keepdims=True))
                - jnp.exp(jnp.sum(lam[2:3] * lam[3:4], axis=-1, keepdims=True)) + lambda_init)
    first_map = lax.broadcasted_iota(jnp.int32, (TQ, DIFF_V_DIM), 1) < DIFF_QK_DIM
    out_scale = w_ref[...] * (1.0 - lambda_init)

    def attend_tile(r0, n_keys, n_split):
        rows = pl.ds(r0, TQ)
        q = q_ref[0, rows, :] * (DIFF_QK_DIM ** -0.5 * math.log2(math.e))
        qr = _axial_rope_tile(q, cos_ref[rows, :], slo_ref[rows, :], shi_ref[rows, :])
        q_maps = (jnp.where(first_map, qr, 0.0).astype(BF16), jnp.where(first_map, 0.0, qr).astype(BF16))
        step = TQ // n_split

        def attend_rows(j):
            r = slice(j * step, (j + 1) * step)
            qs = jnp.concatenate([q_maps[0][r], q_maps[1][r]], axis=0)
            s = lax.dot_general(qs, kb_ref[:n_keys, :], (((1,), (1,)), ((), ())),
                                preferred_element_type=F32)
            yield
            e = jnp.exp2(s - jnp.max(s, axis=-1, keepdims=True)).astype(BF16)
            ob = _dot(e, vb_ref[:n_keys, :])
            yield
            ob = ob[:, :DIFF_V_DIM] / ob[:, DIFF_V_DIM:DIFF_V_DIM + 1]
            o = ob[:step] - lam_full * ob[step:]
            y = o * lax.rsqrt(jnp.mean(o * o, axis=-1, keepdims=True) + EPS) * out_scale
            o_ref[0, pl.ds(r0 + j * step, step), :] = y.astype(BF16)

        _run_in_lockstep([attend_rows(j) for j in range(n_split)])

    attend_tile(0, CTX_LEN, 1)

    def latent_tile(i, carry):
        attend_tile(pl.multiple_of(i * TQ, TQ), LT, DIFF_Q_SPLIT)
        return carry

    lax.fori_loop(1, DIFF_NQ, latent_tile, 0)


def _diff_rope_tables():
    rows = SEQ // GRID_W
    r, cidx = jnp.meshgrid(jnp.arange(rows), jnp.arange(GRID_W), indexing='ij')
    axis_dim = DIFF_QK_DIM // 2
    inv = ROPE_BASE ** (-jnp.arange(0, axis_dim, 2, dtype=F32) / axis_dim)
    ang_r = r.reshape(-1).astype(F32)[:, None] * inv
    ang_c = cidx.reshape(-1).astype(F32)[:, None] * inv
    zero = jnp.zeros_like(ang_r)

    def table(fn_lo, fn_hi):
        one = jnp.concatenate([fn_lo(ang_r), fn_hi(ang_r), fn_lo(ang_c), fn_hi(ang_c)], axis=-1)
        return jnp.concatenate([one, one], axis=-1)

    cos = table(jnp.cos, jnp.cos)
    sin_lo = table(lambda a: -jnp.sin(a), lambda a: zero)
    sin_hi = table(lambda a: zero, jnp.sin)
    ident = [jnp.ones((CTX_LEN, DIFF_V_DIM), F32), jnp.zeros((CTX_LEN, DIFF_V_DIM), F32),
             jnp.zeros((CTX_LEN, DIFF_V_DIM), F32)]
    return [jnp.concatenate([c, t], axis=0) for c, t in zip(ident, (cos, sin_lo, sin_hi))]


def _diff_mixer(p, lam, subln_w, layer_idx):
    bsz = p.shape[0]
    lambda_init = 0.8 - 0.6 * math.exp(-0.3 * layer_idx)
    cos, sin_lo, sin_hi = _diff_rope_tables()
    q0 = 4 * DN_WIDTH // DIFF_V_DIM
    k0 = q0 + DIFF_HEADS
    v0 = k0 + DIFF_HEADS
    tab = pl.BlockSpec((LT, DIFF_V_DIM), lambda b, h: (0, 0))
    seq = lambda off: pl.BlockSpec((1, LT, DIFF_V_DIM), lambda b, h: (b, 0, off + h))
    return pl.pallas_call(
        functools.partial(_diff_kernel, lambda_init=lambda_init),
        grid=(bsz, DIFF_HEADS),
        in_specs=[seq(q0), seq(k0), seq(v0), tab, tab, tab,
                  pl.BlockSpec((4, DIFF_QK_DIM), lambda b, h: (0, 0)),
                  pl.BlockSpec((1, DIFF_V_DIM), lambda b, h: (0, 0))],
        out_specs=pl.BlockSpec((1, LT, DIFF_V_DIM), lambda b, h: (b, 0, h)),
        out_shape=jax.ShapeDtypeStruct((bsz, LT, DIFF_WIDTH), BF16),
        scratch_shapes=[pltpu.VMEM((LT, DIFF_V_DIM), BF16), pltpu.VMEM((LT, 2 * DIFF_V_DIM), BF16)],
        compiler_params=_cparams(("parallel", "parallel")),
        name="diff_attention",
    )(p, p, p, cos, sin_lo, sin_hi, lam, subln_w.reshape(1, DIFF_V_DIM))


DC = DN_CHUNK
DN_NCHUNK = LT // DC
DN_CTX_CHUNKS = CTX_LEN // DC


DN_PREP_GROUP = 9


def _split2(x):
    hi = x.astype(BF16)
    return hi, (x - hi.astype(F32)).astype(BF16)


def _stack3(a_bf16):
    return jnp.concatenate([a_bf16, a_bf16, a_bf16], axis=1)


def _mm_exact_lhs(a3_bf16, x):
    h1 = x.astype(BF16)
    r1 = x - h1.astype(F32)
    h2 = r1.astype(BF16)
    h3 = (r1 - h2.astype(F32)).astype(BF16)
    return _dot(a3_bf16, jnp.concatenate([h1, h2, h3], axis=0))


def _softplus(x):
    return jnp.maximum(x, 0.0) + jnp.log1p(jnp.exp(-jnp.abs(x)))


def _dn_kernel(q_ref, k_ref, v_ref, z_ref, gt_ref, cq_ref, ck_ref, cv_ref, prm_ref, nw_ref, o_ref,
               qn_ref, kn_ref, vn_ref, la_ref, be_ref, wq_ref, u_ref, qk_ref, b_ref, p_ref, e_ref,
               sst_ref, s_ref):
    h = pl.program_id(1)
    tpos = lax.broadcasted_iota(jnp.int32, (LT, 1), 0)
    has_prev = jnp.logical_and(tpos != 0, tpos != CTX_LEN)
    has_next = jnp.logical_and(tpos != CTX_LEN - 1, tpos != LT - 1)

    def conv_silu(x_ref, c_ref):
        x = x_ref[0]
        w = c_ref[...]
        prev = jnp.where(has_prev, pltpu.roll(x, 1, 0), 0.0)
        nxt = jnp.where(has_next, pltpu.roll(x, LT - 1, 0), 0.0)
        return _silu(prev * w[0:1] + x * w[1:2] + nxt * w[2:3])

    def l2n(t):
        return t * lax.rsqrt(jnp.sum(t * t, axis=-1, keepdims=True) + EPS)

    qn_ref[...] = l2n(conv_silu(q_ref, cq_ref)) * (DN_HEAD_DIM ** -0.5)
    kn_ref[...] = l2n(conv_silu(k_ref, ck_ref))
    vn_ref[...] = conv_silu(v_ref, cv_ref)

    gates = gt_ref[0]
    glane = lax.broadcasted_iota(jnp.int32, gates.shape, 1)
    log_a_all = -jnp.exp(prm_ref[0:1, :]) * _softplus(gates + prm_ref[1:2, :])
    beta_all = 1.0 / (1.0 + jnp.exp(-gates))
    for d in range(2):
        log_a = jnp.sum(jnp.where(glane == d * DN_HEADS + h, log_a_all, 0.0), axis=-1, keepdims=True)
        beta = jnp.sum(jnp.where(glane == (2 + d) * DN_HEADS + h, beta_all, 0.0), axis=-1, keepdims=True)
        la_ref[d] = jnp.broadcast_to(log_a, (LT, DN_HEAD_DIM))
        be_ref[d] = jnp.broadcast_to(beta, (LT, DN_HEAD_DIM))

    row = lax.broadcasted_iota(jnp.int32, (DC, 2 * DC), 0)
    lane = lax.broadcasted_iota(jnp.int32, (DC, 2 * DC), 1)
    col = lane & (DC - 1)
    isf = lane < DC
    eye2 = row == col
    isb = jnp.logical_not(isf)
    incl = jnp.logical_or(jnp.logical_and(isf, row >= col), jnp.logical_and(isb, row <= col))
    strict = jnp.logical_or(jnp.logical_and(isf, row > col), jnp.logical_and(isb, row < col))
    r64 = lax.broadcasted_iota(jnp.int32, (DC, DC), 0)
    c64 = lax.broadcasted_iota(jnp.int32, (DC, DC), 1)
    cum_lhs = _stack3(jnp.concatenate([(r64 >= c64).astype(BF16), jnp.ones((DC, DC), BF16)], axis=0))
    ones_lhs = _stack3(jnp.ones((DC, DC), BF16))
    eye_f = eye2.astype(F32)
    same16 = (row >> 4) == (col >> 4)
    in32_not16 = jnp.logical_and((row >> 5) == (col >> 5), jnp.logical_not(same16))
    not32 = (row >> 5) != (col >> 5)
    nt_dims = (((1,), (1,)), ((), ()))

    def block_diag(x2):
        return jnp.concatenate([jnp.where(isf[:x2.shape[0]], x2, 0.0), jnp.where(isb[:x2.shape[0]], x2, 0.0)],
                               axis=0)

    def mm_pair(x2, y2):
        xh, xl = _split2(x2)
        yh, yl = _split2(block_diag(y2))
        rhs = jnp.concatenate([jnp.concatenate([yh, yl], axis=1),
                               jnp.concatenate([yh, jnp.zeros_like(yl)], axis=1)], axis=0)
        out = _dot(jnp.concatenate([xh, xl], axis=1), rhs)
        return out[:, :2 * DC] + out[:, 2 * DC:]

    def prep_chunk(n):
        r0 = pl.multiple_of(n * DC, DC)
        rows = pl.ds(r0, DC)
        qn, kn, vn = qn_ref[rows, :], kn_ref[rows, :], vn_ref[rows, :]
        la_f, la_b = la_ref[0, rows, :], la_ref[1, rows, :]
        be_f, be_b = be_ref[0, rows, :], be_ref[1, rows, :]
        pt_f = _mm_exact_lhs(cum_lhs, la_f)
        pt_b = _mm_exact_lhs(cum_lhs, la_b)
        kb16 = kn.astype(BF16)
        kcat = jnp.concatenate([kb16, kb16], axis=0)
        kk2 = lax.dot_general(kb16, kcat, nt_dims, preferred_element_type=F32)
        qk2 = lax.dot_general(qn.astype(BF16), kcat, nt_dims, preferred_element_type=F32)
        yield
        g_f, tot_f = pt_f[:DC], pt_f[DC:]
        tot_b = pt_b[DC:]
        g_b = tot_b - pt_b[:DC] + la_b
        gg = jnp.where(isf, g_f, g_b)
        g_row = _mm_exact_lhs(ones_lhs, jnp.where(eye2, gg, 0.0))
        yield
        gam = jnp.where(incl, jnp.exp(gg - g_row), 0.0)
        a2 = jnp.where(strict, kk2 * jnp.where(isf, be_f, be_b) * gam, 0.0)
        dmat = jnp.where(same16, a2, 0.0)
        d2 = mm_pair(dmat, dmat)
        yield
        d4 = mm_pair(d2, d2)
        inv = mm_pair(eye_f - dmat, eye_f + d2)
        yield
        d8 = mm_pair(d4, d4)
        inv = mm_pair(inv, eye_f + d4)
        yield
        inv = mm_pair(inv, eye_f + d8)
        yield
        t = mm_pair(inv, jnp.where(in32_not16, a2, 0.0))
        yield
        inv = inv - mm_pair(t, inv)
        yield
        t = mm_pair(inv, jnp.where(not32, a2, 0.0))
        yield
        inv = inv - mm_pair(t, inv)
        yield
        eg_f, eg_b = jnp.exp(g_f), jnp.exp(g_b)
        zeros_rhs = jnp.zeros((DC, 2 * DN_HEAD_DIM), BF16)
        rhs_f = jnp.concatenate([vn * be_f, kn * be_f * eg_f], axis=1).astype(BF16)
        rhs_b = jnp.concatenate([vn * be_b, kn * be_b * eg_b], axis=1).astype(BF16)
        rhs_bd = jnp.concatenate([jnp.concatenate([rhs_f, zeros_rhs], axis=1),
                                  jnp.concatenate([zeros_rhs, rhs_b], axis=1)], axis=0)
        ih, il = _split2(inv)
        uw = _dot(jnp.concatenate([ih, il], axis=1), jnp.concatenate([rhs_bd, rhs_bd], axis=0))
        uw_f, uw_b = uw[:, :2 * DN_HEAD_DIM], uw[:, 2 * DN_HEAD_DIM:]
        yield
        kdt_f = jnp.transpose(kn * jnp.exp(tot_f - g_f)).astype(BF16)
        kdt_b = jnp.transpose(kn * jnp.exp(tot_b - g_b)).astype(BF16)
        bp_f = _dot(kdt_f, uw_f.astype(BF16))
        bp_b = _dot(kdt_b, uw_b.astype(BF16))
        yield
        u_ref[n, 0] = uw_f[:, :DN_HEAD_DIM]
        u_ref[n, 1] = uw_b[:, :DN_HEAD_DIM]
        wq_ref[n, 0] = jnp.concatenate([uw_f[:, DN_HEAD_DIM:], qn * eg_f], axis=0).astype(BF16)
        wq_ref[n, 1] = jnp.concatenate([uw_b[:, DN_HEAD_DIM:], qn * eg_b], axis=0).astype(BF16)
        qk_ref[n] = (qk2 * gam).astype(BF16)
        b_ref[n, 0] = bp_f[:, :DN_HEAD_DIM]
        b_ref[n, 1] = bp_b[:, :DN_HEAD_DIM]
        p_ref[n, 0] = bp_f[:, DN_HEAD_DIM:].astype(BF16)
        p_ref[n, 1] = bp_b[:, DN_HEAD_DIM:].astype(BF16)
        e_ref[n, 0] = jnp.exp(tot_f[0:8])
        e_ref[n, 1] = jnp.exp(tot_b[0:8])

    def prep(i, carry):
        _run_in_lockstep([prep_chunk(i * DN_PREP_GROUP + j) for j in range(DN_PREP_GROUP)])
        return carry

    lax.fori_loop(0, DN_NCHUNK // DN_PREP_GROUP, prep, 0)

    s_ref[...] = jnp.zeros(s_ref.shape, F32)

    def scan_dir(d, n):
        state = s_ref[d]
        s16 = state.astype(BF16)
        sst_ref[n, d] = s16
        ps = _dot(p_ref[n, d], s16)
        yield
        s_ref[d] = state * e_ref[n, d][0:1, :] - ps + b_ref[n, d]

    def scan_step(s, carry):
        nb = jnp.where(s < DN_CTX_CHUNKS, DN_CTX_CHUNKS - 1 - s, DN_NCHUNK - 1 + DN_CTX_CHUNKS - s)
        _run_in_lockstep([scan_dir(0, s), scan_dir(1, nb)])
        return carry

    lax.fori_loop(0, DN_NCHUNK, scan_step, 0)

    def finish_chunk(n):
        rows = pl.ds(pl.multiple_of(n * DC, DC), DC)
        sw_f = _dot(wq_ref[n, 0], sst_ref[n, 0])
        sw_b = _dot(wq_ref[n, 1], sst_ref[n, 1])
        yield
        vcat = jnp.concatenate([u_ref[n, 0] - sw_f[:DC], u_ref[n, 1] - sw_b[:DC]], axis=0).astype(BF16)
        intra = _dot(qk_ref[n], vcat)
        yield
        o = sw_f[DC:] + sw_b[DC:] + intra
        y = o * lax.rsqrt(jnp.mean(o * o, axis=-1, keepdims=True) + EPS) * nw_ref[...]
        o_ref[0, rows, :] = (y * _silu(z_ref[0, rows, :])).astype(BF16)

    def finish(i, carry):
        _run_in_lockstep([finish_chunk(i * DN_PREP_GROUP + j) for j in range(DN_PREP_GROUP)])
        return carry

    lax.fori_loop(0, DN_NCHUNK // DN_PREP_GROUP, finish, 0)


def _dn_mixer(p, conv_w, a_log, dt_bias, norm_w):
    bsz = p.shape[0]
    hd = DN_HEAD_DIM
    prm = jnp.pad(jnp.stack([a_log.reshape(-1), dt_bias.reshape(-1)]), ((0, 0), (0, hd - 2 * DN_HEADS)))
    gate_blk = AB_MAIN_COLS // hd
    seq = lambda off: pl.BlockSpec((1, LT, hd), lambda b, h: (b, 0, off + h))
    cw = lambda off: pl.BlockSpec((CONV_WIDTH, hd), lambda b, h: (0, off + h))
    return pl.pallas_call(
        _dn_kernel,
        grid=(bsz, DN_HEADS),
        in_specs=[seq(0), seq(DN_HEADS), seq(2 * DN_HEADS), seq(3 * DN_HEADS),
                  pl.BlockSpec((1, LT, hd), lambda b, h: (b, 0, gate_blk)),
                  cw(0), cw(DN_HEADS), cw(2 * DN_HEADS),
                  pl.BlockSpec((2, hd), lambda b, h: (0, 0)),
                  pl.BlockSpec((1, hd), lambda b, h: (0, 0))],
        out_specs=pl.BlockSpec((1, LT, hd), lambda b, h: (b, 0, h)),
        out_shape=jax.ShapeDtypeStruct((bsz, LT, DN_WIDTH), BF16),
        scratch_shapes=[pltpu.VMEM((LT, hd), F32), pltpu.VMEM((LT, hd), F32), pltpu.VMEM((LT, hd), F32),
                        pltpu.VMEM((2, LT, hd), F32), pltpu.VMEM((2, LT, hd), F32),
                        pltpu.VMEM((DN_NCHUNK, 2, 2 * DC, hd), BF16),
                        pltpu.VMEM((DN_NCHUNK, 2, DC, hd), F32),
                        pltpu.VMEM((DN_NCHUNK, DC, 2 * DC), BF16),
                        pltpu.VMEM((DN_NCHUNK, 2, hd, hd), F32),
                        pltpu.VMEM((DN_NCHUNK, 2, hd, hd), BF16),
                        pltpu.VMEM((DN_NCHUNK, 2, 8, hd), F32),
                        pltpu.VMEM((DN_NCHUNK, 2, hd, hd), BF16),
                        pltpu.VMEM((2, hd, hd), F32)],
        compiler_params=_cparams(("parallel", "parallel")),
        name="gated_deltanet",
    )(p, p, p, p, p, conv_w, conv_w, conv_w, prm, norm_w.reshape(1, hd))


def _split_ab_weight(w):
    o_ab = 4 * DN_WIDTH
    n_ab = 4 * DN_HEADS
    main = jnp.concatenate([w[..., :o_ab], w[..., o_ab + n_ab:]], axis=-1)
    pad = [(0, 0)] * (w.ndim - 1) + [(0, AB_GATE_COLS - n_ab)]
    return jnp.concatenate([main, jnp.pad(w[..., o_ab:o_ab + n_ab], pad)], axis=-1)


def kernel(x, c, ctx, c_ctx, ada_w, ada_b, norm_w, final_norm_w, ffn_w_in, ffn_w_out, ab_w_in,
           ab_conv_w, dn_A_log, dn_dt_bias, dn_norm_w, diff_lambda, diff_subln_w, ab_w_out,
           ret_w_in, ret_decay_logit, ret_w_out):
    B = x.shape[0]
    xs = jnp.concatenate([ctx, x], axis=1).reshape(B * LT, D_MODEL)
    cvec = jnp.concatenate([c, c_ctx[None], jnp.zeros((MOD_ROWS - B - 1, D_MODEL), F32)], axis=0)
    mods_all = _ada_mods(cvec, ada_w, ada_b)
    mods_all = mods_all.reshape(DEPTH, MOD_ROWS, N_MOD, D_MODEL).transpose(0, 2, 1, 3)
    norms = norm_w.reshape(DEPTH, 3, 1, D_MODEL)
    ffn_w_in_b = ffn_w_in.astype(BF16)
    ffn_w_out_b = ffn_w_out.astype(BF16)
    ab_w_in_b = _split_ab_weight(ab_w_in).astype(BF16)
    ab_w_out_b = ab_w_out.astype(BF16)
    ret_w_in_b = ret_w_in.astype(BF16)
    ret_w_out_b = ret_w_out.astype(BF16)
    for l in range(DEPTH):
        last = l == DEPTH - 1
        xs = _ffn(xs, norms, mods_all, l, 0, ffn_w_in_b, ffn_w_out_b, 0)
        i = l // 2
        if l % 2 == 0:
            p = _proj(xs, norms, mods_all, l, ab_w_in_b, i, AB_PROJ_TN).reshape(B, LT, -1)
            y_dn = _dn_mixer(p, ab_conv_w[i], dn_A_log[i], dn_dt_bias[i], dn_norm_w[i])
            y_df = _diff_mixer(p, diff_lambda[i], diff_subln_w[i], l)
            xs = _out_proj(xs, mods_all, l, [y_dn.reshape(B * LT, DN_WIDTH), y_df.reshape(B * LT, DIFF_WIDTH)],
                           ab_w_out_b, i)
        else:
            p = _proj(xs, norms, mods_all, l, ret_w_in_b, i, RET_PROJ_TN).reshape(B, LT, -1)
            y = _ret_mixer(p, ret_decay_logit[i])
            xs = _out_proj(xs, mods_all, l, [y.reshape(B * LT, RET_V_WIDTH)], ret_w_out_b, i)
        xs = _ffn(xs, norms, mods_all, l, 2, ffn_w_in_b, ffn_w_out_b, 1,
                  final_w=final_norm_w if last else None)
    return xs.reshape(B, LT, D_MODEL)[:, CTX_LEN:]
```

```python
import functools
import math

import jax
import jax.numpy as jnp
from jax import lax
from jax.experimental import pallas as pl
from jax.experimental.pallas import tpu as pltpu

D_MODEL = 1024
BATCH = 8
SEQ = 2048
DEPTH = 4
GRID_W = 64
CTX_LEN = 256
EPS = 1e-6
N_MOD = 9
DN_HEADS = 4
DN_HEAD_DIM = 128
DN_WIDTH = DN_HEADS * DN_HEAD_DIM
CONV_WIDTH = 3
DN_CHUNK = 64
DIFF_HEADS = 4
DIFF_QK_DIM = 64
DIFF_V_DIM = 2 * DIFF_QK_DIM
DIFF_WIDTH = DIFF_HEADS * DIFF_V_DIM
ROPE_BASE = 10000.0
RET_HEADS = 8
RET_KEY_DIM = D_MODEL // RET_HEADS
RET_VALUE_DIM = 2 * RET_KEY_DIM
RET_V_WIDTH = RET_HEADS * RET_VALUE_DIM
D_FF = 2816
RET_IN = 2 * RET_HEADS * RET_KEY_DIM + 2 * RET_V_WIDTH
AB_MAIN_COLS = 4 * DN_WIDTH + 3 * DIFF_WIDTH
AB_GATE_COLS = 256
AB_PROJ_TN = (AB_MAIN_COLS + AB_GATE_COLS) // 3
RET_PROJ_TN = RET_IN // 3

F32 = jnp.float32
BF16 = jnp.bfloat16

LT = CTX_LEN + SEQ
TM = LT // 2
TILES_PER_BATCH = LT // TM
MOD_ROWS = 16
CTX_ROW = BATCH
V7X_VMEM_LIMIT = 56 * 1024 * 1024
FF_CHUNK = 256
FFN_ROW_SPLIT = 2


def _cparams(sem):
    return pltpu.CompilerParams(dimension_semantics=sem, vmem_limit_bytes=V7X_VMEM_LIMIT)


def _silu(x):
    return x * (1.0 / (1.0 + jnp.exp(-x)))


def _dot(a, b):
    return jnp.dot(a, b, preferred_element_type=F32)


def _run_in_lockstep(gens):
    while gens:
        alive = []
        for g in gens:
            try:
                next(g)
                alive.append(g)
            except StopIteration:
                pass
        gens = alive


def _ada_kernel(s_ref, w_ref, b_ref, o_ref):
    s = _silu(s_ref[...]).astype(BF16)
    o_ref[0] = _dot(s, w_ref[0].astype(BF16)) + b_ref[0]


def _ada_mods(cvec, ada_w, ada_b):
    tn = 1024
    n = N_MOD * D_MODEL
    return pl.pallas_call(
        _ada_kernel,
        grid=(DEPTH, n // tn),
        in_specs=[pl.BlockSpec((MOD_ROWS, D_MODEL), lambda l, j: (0, 0)),
                  pl.BlockSpec((1, D_MODEL, tn), lambda l, j: (l, 0, j)),
                  pl.BlockSpec((1, 1, tn), lambda l, j: (l, 0, j))],
        out_specs=pl.BlockSpec((1, MOD_ROWS, tn), lambda l, j: (l, 0, j)),
        out_shape=jax.ShapeDtypeStruct((DEPTH, MOD_ROWS, n), F32),
        compiler_params=_cparams(("parallel", "parallel")),
        name="ada_mods",
    )(cvec, ada_w, ada_b.reshape(DEPTH, 1, n))


def _tile_mod_rows(mod_ref, k):
    i = pl.program_id(0)
    lat = mod_ref[k, pl.ds(i // TILES_PER_BATCH, 1), :]
    head = jnp.where(i % TILES_PER_BATCH == 0, mod_ref[k, CTX_ROW:CTX_ROW + 1, :], lat)
    return head, lat


def _mod_row_blocks(r0, r1):
    if r0 < CTX_LEN < r1:
        return [(r0, CTX_LEN, 0), (CTX_LEN, r1, 1)]
    return [(r0, r1, 0 if r1 <= CTX_LEN else 1)]


def _ada_in_rows(x_ref, g_ref, mod_ref, h_ref, r0, r1):
    shift, scale = _tile_mod_rows(mod_ref, 0), _tile_mod_rows(mod_ref, 1)
    for a, b, k in _mod_row_blocks(r0, r1):
        x = x_ref[a:b, :]
        y = x * lax.rsqrt(jnp.mean(x * x, axis=-1, keepdims=True) + EPS) * g_ref[...]
        h_ref[a:b, :] = (y * (1.0 + scale[k]) + shift[k]).astype(BF16)


def _gated_residual_rows(x_ref, upd_ref, mod_ref, o_ref, gate_scale, final_w_ref, r0, r1):
    gate = _tile_mod_rows(mod_ref, 2)
    for a, b, k in _mod_row_blocks(r0, r1):
        v = x_ref[a:b, :] + (gate_scale * gate[k]) * upd_ref[a:b, :]
        if final_w_ref is not None:
            v = v * lax.rsqrt(jnp.mean(v * v, axis=-1, keepdims=True) + EPS) * final_w_ref[...]
        o_ref[a:b, :] = v


def _mod_spec(layer, sub):
    return pl.BlockSpec((None, 3, MOD_ROWS, D_MODEL), lambda *idx: (layer, sub, 0, 0))


def _norm_spec(layer, sub):
    return pl.BlockSpec((None, None, 1, D_MODEL), lambda *idx: (layer, sub, 0, 0))


def _ffn_kernel(*refs, with_final):
    if with_final:
        x_ref, g_ref, mod_ref, win_ref, wout_ref, fw_ref, o_ref, h_ref, acc_ref = refs
    else:
        x_ref, g_ref, mod_ref, win_ref, wout_ref, o_ref, h_ref, acc_ref = refs
        fw_ref = None
    step = TM // FFN_ROW_SPLIT
    for r0 in range(0, TM, step):
        r1 = r0 + step
        _ada_in_rows(x_ref, g_ref, mod_ref, h_ref, r0, r1)
        for f in range(D_FF // FF_CHUNK):
            h = h_ref[r0:r1, :]
            gg = _dot(h, win_ref[:, f * FF_CHUNK:(f + 1) * FF_CHUNK])
            uu = _dot(h, win_ref[:, D_FF + f * FF_CHUNK:D_FF + (f + 1) * FF_CHUNK])
            a = (_silu(gg) * uu).astype(BF16)
            part = _dot(a, wout_ref[f * FF_CHUNK:(f + 1) * FF_CHUNK, :])
            if f == 0:
                acc_ref[r0:r1, :] = part
            else:
                acc_ref[r0:r1, :] += part
        _gated_residual_rows(x_ref, acc_ref, mod_ref, o_ref, 0.5, fw_ref, r0, r1)


def _ffn(x, norms, mods, layer, sub, w_in, w_out, which, final_w=None):
    t = x.shape[0]
    single = pl.Buffered(1)
    in_specs = [pl.BlockSpec((TM, D_MODEL), lambda i: (i, 0)),
                _norm_spec(layer, sub),
                _mod_spec(layer, sub),
                pl.BlockSpec((None, None, D_MODEL, 2 * D_FF), lambda i: (layer, which, 0, 0),
                             pipeline_mode=single),
                pl.BlockSpec((None, None, D_FF, D_MODEL), lambda i: (layer, which, 0, 0),
                             pipeline_mode=single)]
    args = [x, norms, mods, w_in, w_out]
    if final_w is not None:
        in_specs.append(pl.BlockSpec((1, D_MODEL), lambda i: (0, 0)))
        args.append(final_w.reshape(1, D_MODEL))
    return pl.pallas_call(
        functools.partial(_ffn_kernel, with_final=final_w is not None),
        grid=(t // TM,),
        in_specs=in_specs,
        out_specs=pl.BlockSpec((TM, D_MODEL), lambda i: (i, 0)),
        out_shape=jax.ShapeDtypeStruct((t, D_MODEL), F32),
        scratch_shapes=[pltpu.VMEM((TM, D_MODEL), BF16), pltpu.VMEM((TM, D_MODEL), F32)],
        compiler_params=_cparams(("parallel",)),
        name="ffn",
    )(*args)


def _proj_kernel(x_ref, g_ref, mod_ref, w_ref, o_ref, h_ref):
    @pl.when(pl.program_id(1) == 0)
    def _():
        step = TM // FFN_ROW_SPLIT
        for r0 in range(0, TM, step):
            _ada_in_rows(x_ref, g_ref, mod_ref, h_ref, r0, r0 + step)
            o_ref[r0:r0 + step, :] = _dot(h_ref[r0:r0 + step, :], w_ref[...])

    @pl.when(pl.program_id(1) != 0)
    def _():
        o_ref[...] = _dot(h_ref[...], w_ref[...])


def _proj(x, norms, mods, layer, w, which, tn):
    t = x.shape[0]
    n = w.shape[2]
    return pl.pallas_call(
        _proj_kernel,
        grid=(t // TM, n // tn),
        in_specs=[pl.BlockSpec((TM, D_MODEL), lambda i, j: (i, 0)),
                  _norm_spec(layer, 1),
                  _mod_spec(layer, 1),
                  pl.BlockSpec((None, D_MODEL, tn), lambda i, j: (which, 0, j))],
        out_specs=pl.BlockSpec((TM, tn), lambda i, j: (i, j)),
        out_shape=jax.ShapeDtypeStruct((t, n), F32),
        scratch_shapes=[pltpu.VMEM((TM, D_MODEL), BF16)],
        compiler_params=_cparams(("parallel", "arbitrary")),
        name="mixer_in_proj",
    )(x, norms, mods, w)


def _out_proj_kernel(*refs, n_parts):
    x_ref, mod_ref = refs[0], refs[1]
    y_refs = refs[2:2 + n_parts]
    w_refs = refs[2 + n_parts:2 + 2 * n_parts]
    o_ref = refs[2 + 2 * n_parts]
    upd = None
    for y_ref, w_ref in zip(y_refs, w_refs):
        part = _dot(y_ref[...].astype(BF16), w_ref[...])
        upd = part if upd is None else upd + part
    _gated_residual_rows(x_ref, upd, mod_ref, o_ref, 1.0, None, 0, TM)


def _out_proj(x, mods, layer, ys, w, which):
    t = x.shape[0]
    n_parts = len(ys)
    kp = ys[0].shape[1]
    in_specs = [pl.BlockSpec((TM, D_MODEL), lambda i: (i, 0)), _mod_spec(layer, 1)]
    in_specs += [pl.BlockSpec((TM, kp), lambda i: (i, 0)) for _ in ys]
    in_specs += [pl.BlockSpec((None, kp, D_MODEL), lambda i, p=p: (which, p, 0)) for p in range(n_parts)]
    return pl.pallas_call(
        functools.partial(_out_proj_kernel, n_parts=n_parts),
        grid=(t // TM,),
        in_specs=in_specs,
        out_specs=pl.BlockSpec((TM, D_MODEL), lambda i: (i, 0)),
        out_shape=jax.ShapeDtypeStruct((t, D_MODEL), F32),
        compiler_params=_cparams(("parallel",)),
        name="mixer_out_proj",
    )(x, mods, *ys, *([w] * n_parts))


RC = CTX_LEN
RET_NCHUNK = LT // RC


def _log_sigmoid(x):
    return jnp.minimum(x, 0.0) - jnp.log1p(jnp.exp(-jnp.abs(x)))


def _ret_kernel(q_ref, k_ref, v_ref, g_ref, cos_ref, sin_ref, dl_ref, o_ref,
                qb_ref, kb_ref, kv_ref, s_ref):
    lg = _log_sigmoid(dl_ref[0])
    lg_f, lg_b = lg[0:1, :], lg[1:2, :]
    cos, sin = cos_ref[...], sin_ref[...]
    q = q_ref[0]
    k = k_ref[0] * (RET_KEY_DIM ** -0.5)
    half = RET_KEY_DIM // 2
    qb_ref[...] = (q * cos + pltpu.roll(q, half, 1) * sin).astype(BF16)
    kr = k * cos + pltpu.roll(k, half, 1) * sin
    kb_ref[...] = kr.astype(BF16)

    pos = lax.broadcasted_iota(jnp.int32, (RC, 1), 0).astype(F32)
    kdec_f = jnp.exp(lg_f * (RC - 1.0 - pos))
    kdec_b = jnp.exp(lg_b * pos)
    qdec_f = jnp.exp(lg_f * (pos + 1.0))
    qdec_b = jnp.exp(lg_b * (RC - pos))
    c_f = jnp.exp(lg_f * RC)
    c_b = jnp.exp(lg_b * RC)
    ii = lax.broadcasted_iota(jnp.int32, (RC, RC), 0)
    jj = lax.broadcasted_iota(jnp.int32, (RC, RC), 1)
    dist = (ii - jj).astype(F32)
    dmat = (jnp.where(ii >= jj, jnp.exp(lg_f * dist), 0.0)
            + jnp.where(ii <= jj, jnp.exp(-lg_b * dist), 0.0))

    tn_dims = (((0,), (0,)), ((), ()))
    for n in range(RET_NCHUNK):
        rows = slice(n * RC, (n + 1) * RC)
        kn = kr[rows]
        vn = v_ref[0, rows, :].astype(BF16)
        kv_ref[0, n] = lax.dot_general((kn * kdec_f).astype(BF16), vn, tn_dims,
                                       preferred_element_type=F32)
        kv_ref[1, n] = lax.dot_general((kn * kdec_b).astype(BF16), vn, tn_dims,
                                       preferred_element_type=F32)

    sf = jnp.zeros((RET_KEY_DIM, RET_VALUE_DIM), F32)
    for n in range(RET_NCHUNK):
        s_ref[n, :, :RET_VALUE_DIM] = sf.astype(BF16)
        sf = sf * c_f + kv_ref[0, n]
    s_ref[0, :, RET_VALUE_DIM:] = jnp.zeros((RET_KEY_DIM, RET_VALUE_DIM), BF16)
    sb = kv_ref[1, 0]
    for n in range(RET_NCHUNK - 1, 0, -1):
        s_ref[n, :, RET_VALUE_DIM:] = sb.astype(BF16)
        sb = sb * c_b + kv_ref[1, n]

    nt_dims = (((1,), (1,)), ((), ()))

    for n in range(RET_NCHUNK):
        rows = slice(n * RC, (n + 1) * RC)
        qn = qb_ref[rows, :]
        s = lax.dot_general(qn, kb_ref[rows, :], nt_dims, preferred_element_type=F32)
        cross = _dot(qn, s_ref[n])
        o = _dot((s * dmat).astype(BF16), v_ref[0, rows, :].astype(BF16))
        o = o + cross[:, :RET_VALUE_DIM] * qdec_f + cross[:, RET_VALUE_DIM:] * qdec_b
        y = o * lax.rsqrt(jnp.mean(o * o, axis=-1, keepdims=True) + EPS)
        o_ref[0, rows, :] = (y * _silu(g_ref[0, rows, :])).astype(BF16)


def _ret_rope_tables():
    inv = ROPE_BASE ** (-jnp.linspace(0.0, 1.0, RET_KEY_DIM // 2, dtype=F32))
    ang = jnp.arange(SEQ, dtype=F32)[:, None] * inv
    ang = jnp.concatenate([jnp.zeros((CTX_LEN, RET_KEY_DIM // 2), F32), ang], axis=0)
    cos = jnp.concatenate([jnp.cos(ang)] * 2, axis=-1)
    sin = jnp.concatenate([-jnp.sin(ang), jnp.sin(ang)], axis=-1)
    return cos, sin


def _ret_mixer(p, decay_logit):
    bsz = p.shape[0]
    cos, sin = _ret_rope_tables()
    dl = decay_logit.T.reshape(RET_HEADS, 2, 1)
    kq = RET_HEADS
    kv = (2 * RET_HEADS * RET_KEY_DIM) // RET_VALUE_DIM
    return pl.pallas_call(
        _ret_kernel,
        grid=(bsz, RET_HEADS),
        in_specs=[pl.BlockSpec((1, LT, RET_KEY_DIM), lambda b, h: (b, 0, h)),
                  pl.BlockSpec((1, LT, RET_KEY_DIM), lambda b, h: (b, 0, kq + h)),
                  pl.BlockSpec((1, LT, RET_VALUE_DIM), lambda b, h: (b, 0, kv + h)),
                  pl.BlockSpec((1, LT, RET_VALUE_DIM), lambda b, h: (b, 0, kv + RET_HEADS + h)),
                  pl.BlockSpec((LT, RET_KEY_DIM), lambda b, h: (0, 0)),
                  pl.BlockSpec((LT, RET_KEY_DIM), lambda b, h: (0, 0)),
                  pl.BlockSpec((1, 2, 1), lambda b, h: (h, 0, 0))],
        out_specs=pl.BlockSpec((1, LT, RET_VALUE_DIM), lambda b, h: (b, 0, h)),
        out_shape=jax.ShapeDtypeStruct((bsz, LT, RET_V_WIDTH), BF16),
        scratch_shapes=[pltpu.VMEM((LT, RET_KEY_DIM), BF16),
                        pltpu.VMEM((LT, RET_KEY_DIM), BF16),
                        pltpu.VMEM((2, RET_NCHUNK, RET_KEY_DIM, RET_VALUE_DIM), F32),
                        pltpu.VMEM((RET_NCHUNK, RET_KEY_DIM, 2 * RET_VALUE_DIM), BF16)],
        compiler_params=_cparams(("parallel", "parallel")),
        name="retention",
    )(p, p, p, p, cos, sin, dl)


TQ = CTX_LEN
DIFF_NQ = LT // TQ
DIFF_Q_SPLIT = 4


def _axial_rope_tile(x, cos, sin_lo, sin_hi):
    q = DIFF_QK_DIM // 4
    return x * cos + pltpu.roll(x, 128 - q, 1) * sin_lo + pltpu.roll(x, q, 1) * sin_hi


def _diff_kernel(q_ref, k_ref, v_ref, cos_ref, slo_ref, shi_ref, lam_ref, w_ref, o_ref,
                 kb_ref, vb_ref, *, lambda_init):
    kb_ref[...] = _axial_rope_tile(k_ref[0], cos_ref[...], slo_ref[...], shi_ref[...]).astype(BF16)
    ones_col = (lax.broadcasted_iota(jnp.int32, (LT, DIFF_V_DIM), 1) == 0).astype(BF16)
    vb_ref[...] = jnp.concatenate([v_ref[0].astype(BF16), ones_col], axis=1)

    lam = lam_ref[...]
    lam_full = (jnp.exp(jnp.sum(lam[0:1] * lam[1:2], axis=-1, keepdims=True))
                - jnp.exp(jnp.sum(lam[2:3] * lam[3:4], axis=-1, keepdims=True)) + lambda_init)
    first_map = lax.broadcasted_iota(jnp.int32, (TQ, DIFF_V_DIM), 1) < DIFF_QK_DIM
    out_scale = w_ref[...] * (1.0 - lambda_init)

    def attend_tile(r0, n_keys, n_split):
        rows = pl.ds(r0, TQ)
        q = q_ref[0, rows, :] * (DIFF_QK_DIM ** -0.5 * math.log2(math.e))
        qr = _axial_rope_tile(q, cos_ref[rows, :], slo_ref[rows, :], shi_ref[rows, :])
        q_maps = (jnp.where(first_map, qr, 0.0).astype(BF16), jnp.where(first_map, 0.0, qr).astype(BF16))
        step = TQ // n_split

        def attend_rows(j):
            r = slice(j * step, (j + 1) * step)
            qs = jnp.concatenate([q_maps[0][r], q_maps[1][r]], axis=0)
            s = lax.dot_general(qs, kb_ref[:n_keys, :], (((1,), (1,)), ((), ())),
                                preferred_element_type=F32)
            yield
            e = jnp.exp2(s - jnp.max(s, axis=-1, keepdims=True)).astype(BF16)
            ob = _dot(e, vb_ref[:n_keys, :])
            yield
            ob = ob[:, :DIFF_V_DIM] / ob[:, DIFF_V_DIM:DIFF_V_DIM + 1]
            o = ob[:step] - lam_full * ob[step:]
            y = o * lax.rsqrt(jnp.mean(o * o, axis=-1, keepdims=True) + EPS) * out_scale
            o_ref[0, pl.ds(r0 + j * step, step), :] = y.astype(BF16)

        _run_in_lockstep([attend_rows(j) for j in range(n_split)])

    attend_tile(0, CTX_LEN, 1)

    def latent_tile(i, carry):
        attend_tile(pl.multiple_of(i * TQ, TQ), LT, DIFF_Q_SPLIT)
        return carry

    lax.fori_loop(1, DIFF_NQ, latent_tile, 0)


def _diff_rope_tables():
    rows = SEQ // GRID_W
    r, cidx = jnp.meshgrid(jnp.arange(rows), jnp.arange(GRID_W), indexing='ij')
    axis_dim = DIFF_QK_DIM // 2
    inv = ROPE_BASE ** (-jnp.arange(0, axis_dim, 2, dtype=F32) / axis_dim)
    ang_r = r.reshape(-1).astype(F32)[:, None] * inv
    ang_c = cidx.reshape(-1).astype(F32)[:, None] * inv
    zero = jnp.zeros_like(ang_r)

    def table(fn_lo, fn_hi):
        one = jnp.concatenate([fn_lo(ang_r), fn_hi(ang_r), fn_lo(ang_c), fn_hi(ang_c)], axis=-1)
        return jnp.concatenate([one, one], axis=-1)

    cos = table(jnp.cos, jnp.cos)
    sin_lo = table(lambda a: -jnp.sin(a), lambda a: zero)
    sin_hi = table(lambda a: zero, jnp.sin)
    ident = [jnp.ones((CTX_LEN, DIFF_V_DIM), F32), jnp.zeros((CTX_LEN, DIFF_V_DIM), F32),
             jnp.zeros((CTX_LEN, DIFF_V_DIM), F32)]
    return [jnp.concatenate([c, t], axis=0) for c, t in zip(ident, (cos, sin_lo, sin_hi))]


def _diff_mixer(p, lam, subln_w, layer_idx):
    bsz = p.shape[0]
    lambda_init = 0.8 - 0.6 * math.exp(-0.3 * layer_idx)
    cos, sin_lo, sin_hi = _diff_rope_tables()
    q0 = 4 * DN_WIDTH // DIFF_V_DIM
    k0 = q0 + DIFF_HEADS
    v0 = k0 + DIFF_HEADS
    tab = pl.BlockSpec((LT, DIFF_V_DIM), lambda b, h: (0, 0))
    seq = lambda off: pl.BlockSpec((1, LT, DIFF_V_DIM), lambda b, h: (b, 0, off + h))
    return pl.pallas_call(
        functools.partial(_diff_kernel, lambda_init=lambda_init),
        grid=(bsz, DIFF_HEADS),
        in_specs=[seq(q0), seq(k0), seq(v0), tab, tab, tab,
                  pl.BlockSpec((4, DIFF_QK_DIM), lambda b, h: (0, 0)),
                  pl.BlockSpec((1, DIFF_V_DIM), lambda b, h: (0, 0))],
        out_specs=pl.BlockSpec((1, LT, DIFF_V_DIM), lambda b, h: (b, 0, h)),
        out_shape=jax.ShapeDtypeStruct((bsz, LT, DIFF_WIDTH), BF16),
        scratch_shapes=[pltpu.VMEM((LT, DIFF_V_DIM), BF16), pltpu.VMEM((LT, 2 * DIFF_V_DIM), BF16)],
        compiler_params=_cparams(("parallel", "parallel")),
        name="diff_attention",
    )(p, p, p, cos, sin_lo, sin_hi, lam, subln_w.reshape(1, DIFF_V_DIM))


DC = DN_CHUNK
DN_NCHUNK = LT // DC
DN_CTX_CHUNKS = CTX_LEN // DC


DN_PREP_GROUP = 12


def _split2(x):
    hi = x.astype(BF16)
    return hi, (x - hi.astype(F32)).astype(BF16)


def _stack3(a_bf16):
    return jnp.concatenate([a_bf16, a_bf16, a_bf16], axis=1)


def _mm_exact_lhs(a3_bf16, x):
    h1 = x.astype(BF16)
    r1 = x - h1.astype(F32)
    h2 = r1.astype(BF16)
    h3 = (r1 - h2.astype(F32)).astype(BF16)
    return _dot(a3_bf16, jnp.concatenate([h1, h2, h3], axis=0))


def _softplus(x):
    return jnp.maximum(x, 0.0) + jnp.log1p(jnp.exp(-jnp.abs(x)))


def _dn_kernel(q_ref, k_ref, v_ref, z_ref, gt_ref, cq_ref, ck_ref, cv_ref, prm_ref, nw_ref, o_ref,
               qn_ref, kn_ref, vn_ref, la_ref, be_ref, wq_ref, u_ref, qk_ref, b_ref, p_ref, e_ref,
               sst_ref, s_ref):
    h = pl.program_id(1)
    tpos = lax.broadcasted_iota(jnp.int32, (LT, 1), 0)
    has_prev = jnp.logical_and(tpos != 0, tpos != CTX_LEN)
    has_next = jnp.logical_and(tpos != CTX_LEN - 1, tpos != LT - 1)

    def conv_silu(x_ref, c_ref):
        x = x_ref[0]
        w = c_ref[...]
        prev = jnp.where(has_prev, pltpu.roll(x, 1, 0), 0.0)
        nxt = jnp.where(has_next, pltpu.roll(x, LT - 1, 0), 0.0)
        return _silu(prev * w[0:1] + x * w[1:2] + nxt * w[2:3])

    def l2n(t):
        return t * lax.rsqrt(jnp.sum(t * t, axis=-1, keepdims=True) + EPS)

    qn_ref[...] = l2n(conv_silu(q_ref, cq_ref)) * (DN_HEAD_DIM ** -0.5)
    kn_ref[...] = l2n(conv_silu(k_ref, ck_ref))
    vn_ref[...] = conv_silu(v_ref, cv_ref)

    gates = gt_ref[0]
    glane = lax.broadcasted_iota(jnp.int32, gates.shape, 1)
    log_a_all = -jnp.exp(prm_ref[0:1, :]) * _softplus(gates + prm_ref[1:2, :])
    beta_all = 1.0 / (1.0 + jnp.exp(-gates))
    for d in range(2):
        log_a = jnp.sum(jnp.where(glane == d * DN_HEADS + h, log_a_all, 0.0), axis=-1, keepdims=True)
        beta = jnp.sum(jnp.where(glane == (2 + d) * DN_HEADS + h, beta_all, 0.0), axis=-1, keepdims=True)
        la_ref[d] = jnp.broadcast_to(log_a, (LT, DN_HEAD_DIM))
        be_ref[d] = jnp.broadcast_to(beta, (LT, DN_HEAD_DIM))

    row = lax.broadcasted_iota(jnp.int32, (DC, 2 * DC), 0)
    lane = lax.broadcasted_iota(jnp.int32, (DC, 2 * DC), 1)
    col = lane & (DC - 1)
    isf = lane < DC
    eye2 = row == col
    isb = jnp.logical_not(isf)
    incl = jnp.logical_or(jnp.logical_and(isf, row >= col), jnp.logical_and(isb, row <= col))
    strict = jnp.logical_or(jnp.logical_and(isf, row > col), jnp.logical_and(isb, row < col))
    r64 = lax.broadcasted_iota(jnp.int32, (DC, DC), 0)
    c64 = lax.broadcasted_iota(jnp.int32, (DC, DC), 1)
    cum_lhs = _stack3(jnp.concatenate([(r64 >= c64).astype(BF16), jnp.ones((DC, DC), BF16)], axis=0))
    ones_lhs = _stack3(jnp.ones((DC, DC), BF16))
    eye_f = eye2.astype(F32)
    same16 = (row >> 4) == (col >> 4)
    in32_not16 = jnp.logical_and((row >> 5) == (col >> 5), jnp.logical_not(same16))
    not32 = (row >> 5) != (col >> 5)
    nt_dims = (((1,), (1,)), ((), ()))

    def block_diag(x2):
        return jnp.concatenate([jnp.where(isf[:x2.shape[0]], x2, 0.0), jnp.where(isb[:x2.shape[0]], x2, 0.0)],
                               axis=0)

    def mm_pair(x2, y2):
        xh, xl = _split2(x2)
        yh, yl = _split2(block_diag(y2))
        rhs = jnp.concatenate([jnp.concatenate([yh, yl], axis=1),
                               jnp.concatenate([yh, jnp.zeros_like(yl)], axis=1)], axis=0)
        out = _dot(jnp.concatenate([xh, xl], axis=1), rhs)
        return out[:, :2 * DC] + out[:, 2 * DC:]

    def prep_chunk(n):
        r0 = pl.multiple_of(n * DC, DC)
        rows = pl.ds(r0, DC)
        qn, kn, vn = qn_ref[rows, :], kn_ref[rows, :], vn_ref[rows, :]
        la_f, la_b = la_ref[0, rows, :], la_ref[1, rows, :]
        be_f, be_b = be_ref[0, rows, :], be_ref[1, rows, :]
        pt_f = _mm_exact_lhs(cum_lhs, la_f)
        pt_b = _mm_exact_lhs(cum_lhs, la_b)
        kb16 = kn.astype(BF16)
        kcat = jnp.concatenate([kb16, kb16], axis=0)
        kk2 = lax.dot_general(kb16, kcat, nt_dims, preferred_element_type=F32)
        qk2 = lax.dot_general(qn.astype(BF16), kcat, nt_dims, preferred_element_type=F32)
        yield
        g_f, tot_f = pt_f[:DC], pt_f[DC:]
        tot_b = pt_b[DC:]
        g_b = tot_b - pt_b[:DC] + la_b
        gg = jnp.where(isf, g_f, g_b)
        g_row = _mm_exact_lhs(ones_lhs, jnp.where(eye2, gg, 0.0))
        yield
        gam = jnp.where(incl, jnp.exp(gg - g_row), 0.0)
        a2 = jnp.where(strict, kk2 * jnp.where(isf, be_f, be_b) * gam, 0.0)
        dmat = jnp.where(same16, a2, 0.0)
        d2 = mm_pair(dmat, dmat)
        yield
        d4 = mm_pair(d2, d2)
        inv = mm_pair(eye_f - dmat, eye_f + d2)
        yield
        d8 = mm_pair(d4, d4)
        inv = mm_pair(inv, eye_f + d4)
        yield
        inv = mm_pair(inv, eye_f + d8)
        yield
        t = mm_pair(inv, jnp.where(in32_not16, a2, 0.0))
        yield
        inv = inv - mm_pair(t, inv)
        yield
        t = mm_pair(inv, jnp.where(not32, a2, 0.0))
        yield
        inv = inv - mm_pair(t, inv)
        yield
        eg_f, eg_b = jnp.exp(g_f), jnp.exp(g_b)
        zeros_rhs = jnp.zeros((DC, 2 * DN_HEAD_DIM), BF16)
        rhs_f = jnp.concatenate([vn * be_f, kn * be_f * eg_f], axis=1).astype(BF16)
        rhs_b = jnp.concatenate([vn * be_b, kn * be_b * eg_b], axis=1).astype(BF16)
        rhs_bd = jnp.concatenate([jnp.concatenate([rhs_f, zeros_rhs], axis=1),
                                  jnp.concatenate([zeros_rhs, rhs_b], axis=1)], axis=0)
        ih, il = _split2(inv)
        uw = _dot(jnp.concatenate([ih, il], axis=1), jnp.concatenate([rhs_bd, rhs_bd], axis=0))
        uw_f, uw_b = uw[:, :2 * DN_HEAD_DIM], uw[:, 2 * DN_HEAD_DIM:]
        yield
        kdt_f = jnp.transpose(kn * jnp.exp(tot_f - g_f)).astype(BF16)
        kdt_b = jnp.transpose(kn * jnp.exp(tot_b - g_b)).astype(BF16)
        bp_f = _dot(kdt_f, uw_f.astype(BF16))
        bp_b = _dot(kdt_b, uw_b.astype(BF16))
        yield
        u_ref[n, 0] = uw_f[:, :DN_HEAD_DIM]
        u_ref[n, 1] = uw_b[:, :DN_HEAD_DIM]
        wq_ref[n, 0] = jnp.concatenate([uw_f[:, DN_HEAD_DIM:], qn * eg_f], axis=0).astype(BF16)
        wq_ref[n, 1] = jnp.concatenate([uw_b[:, DN_HEAD_DIM:], qn * eg_b], axis=0).astype(BF16)
        qk_ref[n] = (qk2 * gam).astype(BF16)
        b_ref[n, 0] = bp_f[:, :DN_HEAD_DIM]
        b_ref[n, 1] = bp_b[:, :DN_HEAD_DIM]
        p_ref[n, 0] = bp_f[:, DN_HEAD_DIM:].astype(BF16)
        p_ref[n, 1] = bp_b[:, DN_HEAD_DIM:].astype(BF16)
        e_ref[n, 0] = jnp.exp(tot_f[0:8])
        e_ref[n, 1] = jnp.exp(tot_b[0:8])

    def prep(i, carry):
        _run_in_lockstep([prep_chunk(i * DN_PREP_GROUP + j) for j in range(DN_PREP_GROUP)])
        return carry

    lax.fori_loop(0, DN_NCHUNK // DN_PREP_GROUP, prep, 0)

    s_ref[...] = jnp.zeros(s_ref.shape, F32)

    def scan_dir(d, n):
        state = s_ref[d]
        s16 = state.astype(BF16)
        sst_ref[n, d] = s16
        ps = _dot(p_ref[n, d], s16)
        yield
        s_ref[d] = state * e_ref[n, d][0:1, :] - ps + b_ref[n, d]

    def scan_step(s, carry):
        nb = jnp.where(s < DN_CTX_CHUNKS, DN_CTX_CHUNKS - 1 - s, DN_NCHUNK - 1 + DN_CTX_CHUNKS - s)
        _run_in_lockstep([scan_dir(0, s), scan_dir(1, nb)])
        return carry

    lax.fori_loop(0, DN_NCHUNK, scan_step, 0)

    def finish_chunk(n):
        rows = pl.ds(pl.multiple_of(n * DC, DC), DC)
        sw_f = _dot(wq_ref[n, 0], sst_ref[n, 0])
        sw_b = _dot(wq_ref[n, 1], sst_ref[n, 1])
        yield
        vcat = jnp.concatenate([u_ref[n, 0] - sw_f[:DC], u_ref[n, 1] - sw_b[:DC]], axis=0).astype(BF16)
        intra = _dot(qk_ref[n], vcat)
        yield
        o = sw_f[DC:] + sw_b[DC:] + intra
        y = o * lax.rsqrt(jnp.mean(o * o, axis=-1, keepdims=True) + EPS) * nw_ref[...]
        o_ref[0, rows, :] = (y * _silu(z_ref[0, rows, :])).astype(BF16)

    def finish(i, carry):
        _run_in_lockstep([finish_chunk(i * DN_PREP_GROUP + j) for j in range(DN_PREP_GROUP)])
        return carry

    lax.fori_loop(0, DN_NCHUNK // DN_PREP_GROUP, finish, 0)


def _dn_mixer(p, conv_w, a_log, dt_bias, norm_w):
    bsz = p.shape[0]
    hd = DN_HEAD_DIM
    prm = jnp.pad(jnp.stack([a_log.reshape(-1), dt_bias.reshape(-1)]), ((0, 0), (0, hd - 2 * DN_HEADS)))
    gate_blk = AB_MAIN_COLS // hd
    seq = lambda off: pl.BlockSpec((1, LT, hd), lambda b, h: (b, 0, off + h))
    cw = lambda off: pl.BlockSpec((CONV_WIDTH, hd), lambda b, h: (0, off + h))
    return pl.pallas_call(
        _dn_kernel,
        grid=(bsz, DN_HEADS),
        in_specs=[seq(0), seq(DN_HEADS), seq(2 * DN_HEADS), seq(3 * DN_HEADS),
                  pl.BlockSpec((1, LT, hd), lambda b, h: (b, 0, gate_blk)),
                  cw(0), cw(DN_HEADS), cw(2 * DN_HEADS),
                  pl.BlockSpec((2, hd), lambda b, h: (0, 0)),
                  pl.BlockSpec((1, hd), lambda b, h: (0, 0))],
        out_specs=pl.BlockSpec((1, LT, hd), lambda b, h: (b, 0, h)),
        out_shape=jax.ShapeDtypeStruct((bsz, LT, DN_WIDTH), BF16),
        scratch_shapes=[pltpu.VMEM((LT, hd), F32), pltpu.VMEM((LT, hd), F32), pltpu.VMEM((LT, hd), F32),
                        pltpu.VMEM((2, LT, hd), F32), pltpu.VMEM((2, LT, hd), F32),
                        pltpu.VMEM((DN_NCHUNK, 2, 2 * DC, hd), BF16),
                        pltpu.VMEM((DN_NCHUNK, 2, DC, hd), F32),
                        pltpu.VMEM((DN_NCHUNK, DC, 2 * DC), BF16),
                        pltpu.VMEM((DN_NCHUNK, 2, hd, hd), F32),
                        pltpu.VMEM((DN_NCHUNK, 2, hd, hd), BF16),
                        pltpu.VMEM((DN_NCHUNK, 2, 8, hd), F32),
                        pltpu.VMEM((DN_NCHUNK, 2, hd, hd), BF16),
                        pltpu.VMEM((2, hd, hd), F32)],
        compiler_params=_cparams(("parallel", "parallel")),
        name="gated_deltanet",
    )(p, p, p, p, p, conv_w, conv_w, conv_w, prm, norm_w.reshape(1, hd))


def _split_ab_weight(w):
    o_ab = 4 * DN_WIDTH
    n_ab = 4 * DN_HEADS
    main = jnp.concatenate([w[..., :o_ab], w[..., o_ab + n_ab:]], axis=-1)
    pad = [(0, 0)] * (w.ndim - 1) + [(0, AB_GATE_COLS - n_ab)]
    return jnp.concatenate([main, jnp.pad(w[..., o_ab:o_ab + n_ab], pad)], axis=-1)


def kernel(x, c, ctx, c_ctx, ada_w, ada_b, norm_w, final_norm_w, ffn_w_in, ffn_w_out, ab_w_in,
           ab_conv_w, dn_A_log, dn_dt_bias, dn_norm_w, diff_lambda, diff_subln_w, ab_w_out,
           ret_w_in, ret_decay_logit, ret_w_out):
    B = x.shape[0]
    xs = jnp.concatenate([ctx, x], axis=1).reshape(B * LT, D_MODEL)
    cvec = jnp.concatenate([c, c_ctx[None], jnp.zeros((MOD_ROWS - B - 1, D_MODEL), F32)], axis=0)
    mods_all = _ada_mods(cvec, ada_w, ada_b)
    mods_all = mods_all.reshape(DEPTH, MOD_ROWS, N_MOD, D_MODEL).transpose(0, 2, 1, 3)
    norms = norm_w.reshape(DEPTH, 3, 1, D_MODEL)
    ffn_w_in_b = ffn_w_in.astype(BF16)
    ffn_w_out_b = ffn_w_out.astype(BF16)
    ab_w_in_b = _split_ab_weight(ab_w_in).astype(BF16)
    ab_w_out_b = ab_w_out.astype(BF16)
    ret_w_in_b = ret_w_in.astype(BF16)
    ret_w_out_b = ret_w_out.astype(BF16)
    for l in range(DEPTH):
        last = l == DEPTH - 1
        xs = _ffn(xs, norms, mods_all, l, 0, ffn_w_in_b, ffn_w_out_b, 0)
        i = l // 2
        if l % 2 == 0:
            p = _proj(xs, norms, mods_all, l, ab_w_in_b, i, AB_PROJ_TN).reshape(B, LT, -1)
            y_dn = _dn_mixer(p, ab_conv_w[i], dn_A_log[i], dn_dt_bias[i], dn_norm_w[i])
            y_df = _diff_mixer(p, diff_lambda[i], diff_subln_w[i], l)
            xs = _out_proj(xs, mods_all, l, [y_dn.reshape(B * LT, DN_WIDTH), y_df.reshape(B * LT, DIFF_WIDTH)],
                           ab_w_out_b, i)
        else:
            p = _proj(xs, norms, mods_all, l, ret_w_in_b, i, RET_PROJ_TN).reshape(B, LT, -1)
            y = _ret_mixer(p, ret_decay_logit[i])
            xs = _out_proj(xs, mods_all, l, [y.reshape(B * LT, RET_V_WIDTH)], ret_w_out_b, i)
        xs = _ffn(xs, norms, mods_all, l, 2, ffn_w_in_b, ffn_w_out_b, 1,
                  final_w=final_norm_w if last else None)
    return xs.reshape(B, LT, D_MODEL)[:, CTX_LEN:]
```

```python
import functools
import math

import jax
import jax.numpy as jnp
from jax import lax
from jax.experimental import pallas as pl
from jax.experimental.pallas import tpu as pltpu

D_MODEL = 1024
BATCH = 8
SEQ = 2048
DEPTH = 4
GRID_W = 64
CTX_LEN = 256
EPS = 1e-6
N_MOD = 9
DN_HEADS = 4
DN_HEAD_DIM = 128
DN_WIDTH = DN_HEADS * DN_HEAD_DIM
CONV_WIDTH = 3
DN_CHUNK = 64
DIFF_HEADS = 4
DIFF_QK_DIM = 64
DIFF_V_DIM = 2 * DIFF_QK_DIM
DIFF_WIDTH = DIFF_HEADS * DIFF_V_DIM
ROPE_BASE = 10000.0
RET_HEADS = 8
RET_KEY_DIM = D_MODEL // RET_HEADS
RET_VALUE_DIM = 2 * RET_KEY_DIM
RET_V_WIDTH = RET_HEADS * RET_VALUE_DIM
D_FF = 2816
RET_IN = 2 * RET_HEADS * RET_KEY_DIM + 2 * RET_V_WIDTH
AB_MAIN_COLS = 4 * DN_WIDTH + 3 * DIFF_WIDTH
AB_GATE_COLS = 256
AB_PROJ_TN = (AB_MAIN_COLS + AB_GATE_COLS) // 3
RET_PROJ_TN = RET_IN // 3

F32 = jnp.float32
BF16 = jnp.bfloat16

LT = CTX_LEN + SEQ
TM = LT // 2
TILES_PER_BATCH = LT // TM
MOD_ROWS = 16
CTX_ROW = BATCH
V7X_VMEM_LIMIT = 56 * 1024 * 1024
FF_CHUNK = 256
FFN_ROW_SPLIT = 2


def _cparams(sem):
    return pltpu.CompilerParams(dimension_semantics=sem, vmem_limit_bytes=V7X_VMEM_LIMIT)


def _silu(x):
    return x * (1.0 / (1.0 + jnp.exp(-x)))


def _dot(a, b):
    return jnp.dot(a, b, preferred_element_type=F32)


def _run_in_lockstep(gens):
    while gens:
        alive = []
        for g in gens:
            try:
                next(g)
                alive.append(g)
            except StopIteration:
                pass
        gens = alive


def _ada_kernel(s_ref, w_ref, b_ref, o_ref):
    s = _silu(s_ref[...]).astype(BF16)
    o_ref[0] = _dot(s, w_ref[0].astype(BF16)) + b_ref[0]


def _ada_mods(cvec, ada_w, ada_b):
    tn = 1024
    n = N_MOD * D_MODEL
    return pl.pallas_call(
        _ada_kernel,
        grid=(DEPTH, n // tn),
        in_specs=[pl.BlockSpec((MOD_ROWS, D_MODEL), lambda l, j: (0, 0)),
                  pl.BlockSpec((1, D_MODEL, tn), lambda l, j: (l, 0, j)),
                  pl.BlockSpec((1, 1, tn), lambda l, j: (l, 0, j))],
        out_specs=pl.BlockSpec((1, MOD_ROWS, tn), lambda l, j: (l, 0, j)),
        out_shape=jax.ShapeDtypeStruct((DEPTH, MOD_ROWS, n), F32),
        compiler_params=_cparams(("parallel", "parallel")),
        name="ada_mods",
    )(cvec, ada_w, ada_b.reshape(DEPTH, 1, n))


def _tile_mod_rows(mod_ref, k):
    i = pl.program_id(0)
    lat = mod_ref[k, pl.ds(i // TILES_PER_BATCH, 1), :]
    head = jnp.where(i % TILES_PER_BATCH == 0, mod_ref[k, CTX_ROW:CTX_ROW + 1, :], lat)
    return head, lat


def _mod_row_blocks(r0, r1):
    if r0 < CTX_LEN < r1:
        return [(r0, CTX_LEN, 0), (CTX_LEN, r1, 1)]
    return [(r0, r1, 0 if r1 <= CTX_LEN else 1)]


def _ada_in_rows(x_ref, g_ref, mod_ref, h_ref, r0, r1):
    shift, scale = _tile_mod_rows(mod_ref, 0), _tile_mod_rows(mod_ref, 1)
    for a, b, k in _mod_row_blocks(r0, r1):
        x = x_ref[a:b, :]
        y = x * lax.rsqrt(jnp.mean(x * x, axis=-1, keepdims=True) + EPS) * g_ref[...]
        h_ref[a:b, :] = (y * (1.0 + scale[k]) + shift[k]).astype(BF16)


def _gated_residual_rows(x_ref, upd_ref, mod_ref, o_ref, gate_scale, final_w_ref, r0, r1):
    gate = _tile_mod_rows(mod_ref, 2)
    for a, b, k in _mod_row_blocks(r0, r1):
        v = x_ref[a:b, :] + (gate_scale * gate[k]) * upd_ref[a:b, :]
        if final_w_ref is not None:
            v = v * lax.rsqrt(jnp.mean(v * v, axis=-1, keepdims=True) + EPS) * final_w_ref[...]
        o_ref[a:b, :] = v


def _mod_spec(layer, sub):
    return pl.BlockSpec((None, 3, MOD_ROWS, D_MODEL), lambda *idx: (layer, sub, 0, 0))


def _norm_spec(layer, sub):
    return pl.BlockSpec((None, None, 1, D_MODEL), lambda *idx: (layer, sub, 0, 0))


def _ffn_kernel(*refs, with_final):
    if with_final:
        x_ref, g_ref, mod_ref, win_ref, wout_ref, fw_ref, o_ref, h_ref, acc_ref = refs
    else:
        x_ref, g_ref, mod_ref, win_ref, wout_ref, o_ref, h_ref, acc_ref = refs
        fw_ref = None
    step = TM // FFN_ROW_SPLIT
    for r0 in range(0, TM, step):
        r1 = r0 + step
        _ada_in_rows(x_ref, g_ref, mod_ref, h_ref, r0, r1)
        for f in range(D_FF // FF_CHUNK):
            h = h_ref[r0:r1, :]
            gg = _dot(h, win_ref[:, f * FF_CHUNK:(f + 1) * FF_CHUNK])
            uu = _dot(h, win_ref[:, D_FF + f * FF_CHUNK:D_FF + (f + 1) * FF_CHUNK])
            a = (_silu(gg) * uu).astype(BF16)
            part = _dot(a, wout_ref[f * FF_CHUNK:(f + 1) * FF_CHUNK, :])
            if f == 0:
                acc_ref[r0:r1, :] = part
            else:
                acc_ref[r0:r1, :] += part
        _gated_residual_rows(x_ref, acc_ref, mod_ref, o_ref, 0.5, fw_ref, r0, r1)


def _ffn(x, norms, mods, layer, sub, w_in, w_out, which, final_w=None):
    t = x.shape[0]
    single = pl.Buffered(1)
    in_specs = [pl.BlockSpec((TM, D_MODEL), lambda i: (i, 0)),
                _norm_spec(layer, sub),
                _mod_spec(layer, sub),
                pl.BlockSpec((None, None, D_MODEL, 2 * D_FF), lambda i: (layer, which, 0, 0),
                             pipeline_mode=single),
                pl.BlockSpec((None, None, D_FF, D_MODEL), lambda i: (layer, which, 0, 0),
                             pipeline_mode=single)]
    args = [x, norms, mods, w_in, w_out]
    if final_w is not None:
        in_specs.append(pl.BlockSpec((1, D_MODEL), lambda i: (0, 0)))
        args.append(final_w.reshape(1, D_MODEL))
    return pl.pallas_call(
        functools.partial(_ffn_kernel, with_final=final_w is not None),
        grid=(t // TM,),
        in_specs=in_specs,
        out_specs=pl.BlockSpec((TM, D_MODEL), lambda i: (i, 0)),
        out_shape=jax.ShapeDtypeStruct((t, D_MODEL), F32),
        scratch_shapes=[pltpu.VMEM((TM, D_MODEL), BF16), pltpu.VMEM((TM, D_MODEL), F32)],
        compiler_params=_cparams(("parallel",)),
        name="ffn",
    )(*args)


def _proj_kernel(x_ref, g_ref, mod_ref, w_ref, o_ref, h_ref):
    @pl.when(pl.program_id(1) == 0)
    def _():
        step = TM // FFN_ROW_SPLIT
        for r0 in range(0, TM, step):
            _ada_in_rows(x_ref, g_ref, mod_ref, h_ref, r0, r0 + step)
            o_ref[r0:r0 + step, :] = _dot(h_ref[r0:r0 + step, :], w_ref[...])

    @pl.when(pl.program_id(1) != 0)
    def _():
        o_ref[...] = _dot(h_ref[...], w_ref[...])


def _proj(x, norms, mods, layer, w, which, tn):
    t = x.shape[0]
    n = w.shape[2]
    return pl.pallas_call(
        _proj_kernel,
        grid=(t // TM, n // tn),
        in_specs=[pl.BlockSpec((TM, D_MODEL), lambda i, j: (i, 0)),
                  _norm_spec(layer, 1),
                  _mod_spec(layer, 1),
                  pl.BlockSpec((None, D_MODEL, tn), lambda i, j: (which, 0, j))],
        out_specs=pl.BlockSpec((TM, tn), lambda i, j: (i, j)),
        out_shape=jax.ShapeDtypeStruct((t, n), F32),
        scratch_shapes=[pltpu.VMEM((TM, D_MODEL), BF16)],
        compiler_params=_cparams(("parallel", "arbitrary")),
        name="mixer_in_proj",
    )(x, norms, mods, w)


def _out_proj_kernel(*refs, n_parts):
    x_ref, mod_ref = refs[0], refs[1]
    y_refs = refs[2:2 + n_parts]
    w_refs = refs[2 + n_parts:2 + 2 * n_parts]
    o_ref = refs[2 + 2 * n_parts]
    upd = None
    for y_ref, w_ref in zip(y_refs, w_refs):
        part = _dot(y_ref[...].astype(BF16), w_ref[...])
        upd = part if upd is None else upd + part
    _gated_residual_rows(x_ref, upd, mod_ref, o_ref, 1.0, None, 0, TM)


def _out_proj(x, mods, layer, ys, w, which):
    t = x.shape[0]
    n_parts = len(ys)
    kp = ys[0].shape[1]
    in_specs = [pl.BlockSpec((TM, D_MODEL), lambda i: (i, 0)), _mod_spec(layer, 1)]
    in_specs += [pl.BlockSpec((TM, kp), lambda i: (i, 0)) for _ in ys]
    in_specs += [pl.BlockSpec((None, kp, D_MODEL), lambda i, p=p: (which, p, 0)) for p in range(n_parts)]
    return pl.pallas_call(
        functools.partial(_out_proj_kernel, n_parts=n_parts),
        grid=(t // TM,),
        in_specs=in_specs,
        out_specs=pl.BlockSpec((TM, D_MODEL), lambda i: (i, 0)),
        out_shape=jax.ShapeDtypeStruct((t, D_MODEL), F32),
        compiler_params=_cparams(("parallel",)),
        name="mixer_out_proj",
    )(x, mods, *ys, *([w] * n_parts))


RC = CTX_LEN
RET_NCHUNK = LT // RC


def _log_sigmoid(x):
    return jnp.minimum(x, 0.0) - jnp.log1p(jnp.exp(-jnp.abs(x)))


def _ret_kernel(q_ref, k_ref, v_ref, g_ref, cos_ref, sin_ref, dl_ref, o_ref,
                qb_ref, kb_ref, kv_ref, s_ref):
    lg = _log_sigmoid(dl_ref[0])
    lg_f, lg_b = lg[0:1, :], lg[1:2, :]
    cos, sin = cos_ref[...], sin_ref[...]
    q = q_ref[0]
    k = k_ref[0] * (RET_KEY_DIM ** -0.5)
    half = RET_KEY_DIM // 2
    qb_ref[...] = (q * cos + pltpu.roll(q, half, 1) * sin).astype(BF16)
    kr = k * cos + pltpu.roll(k, half, 1) * sin
    kb_ref[...] = kr.astype(BF16)

    pos = lax.broadcasted_iota(jnp.int32, (RC, 1), 0).astype(F32)
    kdec_f = jnp.exp(lg_f * (RC - 1.0 - pos))
    kdec_b = jnp.exp(lg_b * pos)
    qdec_f = jnp.exp(lg_f * (pos + 1.0))
    qdec_b = jnp.exp(lg_b * (RC - pos))
    c_f = jnp.exp(lg_f * RC)
    c_b = jnp.exp(lg_b * RC)
    ii = lax.broadcasted_iota(jnp.int32, (RC, RC), 0)
    jj = lax.broadcasted_iota(jnp.int32, (RC, RC), 1)
    dist = (ii - jj).astype(F32)
    dmat = (jnp.where(ii >= jj, jnp.exp(lg_f * dist), 0.0)
            + jnp.where(ii <= jj, jnp.exp(-lg_b * dist), 0.0))

    tn_dims = (((0,), (0,)), ((), ()))
    for n in range(RET_NCHUNK):
        rows = slice(n * RC, (n + 1) * RC)
        kn = kr[rows]
        vn = v_ref[0, rows, :].astype(BF16)
        kv_ref[0, n] = lax.dot_general((kn * kdec_f).astype(BF16), vn, tn_dims,
                                       preferred_element_type=F32)
        kv_ref[1, n] = lax.dot_general((kn * kdec_b).astype(BF16), vn, tn_dims,
                                       preferred_element_type=F32)

    sf = jnp.zeros((RET_KEY_DIM, RET_VALUE_DIM), F32)
    for n in range(RET_NCHUNK):
        s_ref[n, :, :RET_VALUE_DIM] = sf.astype(BF16)
        sf = sf * c_f + kv_ref[0, n]
    s_ref[0, :, RET_VALUE_DIM:] = jnp.zeros((RET_KEY_DIM, RET_VALUE_DIM), BF16)
    sb = kv_ref[1, 0]
    for n in range(RET_NCHUNK - 1, 0, -1):
        s_ref[n, :, RET_VALUE_DIM:] = sb.astype(BF16)
        sb = sb * c_b + kv_ref[1, n]

    nt_dims = (((1,), (1,)), ((), ()))

    for n in range(RET_NCHUNK):
        rows = slice(n * RC, (n + 1) * RC)
        qn = qb_ref[rows, :]
        s = lax.dot_general(qn, kb_ref[rows, :], nt_dims, preferred_element_type=F32)
        cross = _dot(qn, s_ref[n])
        o = _dot((s * dmat).astype(BF16), v_ref[0, rows, :].astype(BF16))
        o = o + cross[:, :RET_VALUE_DIM] * qdec_f + cross[:, RET_VALUE_DIM:] * qdec_b
        y = o * lax.rsqrt(jnp.mean(o * o, axis=-1, keepdims=True) + EPS)
        o_ref[0, rows, :] = (y * _silu(g_ref[0, rows, :])).astype(BF16)


def _ret_rope_tables():
    inv = ROPE_BASE ** (-jnp.linspace(0.0, 1.0, RET_KEY_DIM // 2, dtype=F32))
    ang = jnp.arange(SEQ, dtype=F32)[:, None] * inv
    ang = jnp.concatenate([jnp.zeros((CTX_LEN, RET_KEY_DIM // 2), F32), ang], axis=0)
    cos = jnp.concatenate([jnp.cos(ang)] * 2, axis=-1)
    sin = jnp.concatenate([-jnp.sin(ang), jnp.sin(ang)], axis=-1)
    return cos, sin


def _ret_mixer(p, decay_logit):
    bsz = p.shape[0]
    cos, sin = _ret_rope_tables()
    dl = decay_logit.T.reshape(RET_HEADS, 2, 1)
    kq = RET_HEADS
    kv = (2 * RET_HEADS * RET_KEY_DIM) // RET_VALUE_DIM
    return pl.pallas_call(
        _ret_kernel,
        grid=(bsz, RET_HEADS),
        in_specs=[pl.BlockSpec((1, LT, RET_KEY_DIM), lambda b, h: (b, 0, h)),
                  pl.BlockSpec((1, LT, RET_KEY_DIM), lambda b, h: (b, 0, kq + h)),
                  pl.BlockSpec((1, LT, RET_VALUE_DIM), lambda b, h: (b, 0, kv + h)),
                  pl.BlockSpec((1, LT, RET_VALUE_DIM), lambda b, h: (b, 0, kv + RET_HEADS + h)),
                  pl.BlockSpec((LT, RET_KEY_DIM), lambda b, h: (0, 0)),
                  pl.BlockSpec((LT, RET_KEY_DIM), lambda b, h: (0, 0)),
                  pl.BlockSpec((1, 2, 1), lambda b, h: (h, 0, 0))],
        out_specs=pl.BlockSpec((1, LT, RET_VALUE_DIM), lambda b, h: (b, 0, h)),
        out_shape=jax.ShapeDtypeStruct((bsz, LT, RET_V_WIDTH), BF16),
        scratch_shapes=[pltpu.VMEM((LT, RET_KEY_DIM), BF16),
                        pltpu.VMEM((LT, RET_KEY_DIM), BF16),
                        pltpu.VMEM((2, RET_NCHUNK, RET_KEY_DIM, RET_VALUE_DIM), F32),
                        pltpu.VMEM((RET_NCHUNK, RET_KEY_DIM, 2 * RET_VALUE_DIM), BF16)],
        compiler_params=_cparams(("parallel", "parallel")),
        name="retention",
    )(p, p, p, p, cos, sin, dl)


TQ = CTX_LEN
DIFF_LAT_ROWS = 2 * TQ
DIFF_Q_SPLIT = 8


def _axial_rope_tile(x, cos, sin_lo, sin_hi):
    q = DIFF_QK_DIM // 4
    return x * cos + pltpu.roll(x, 128 - q, 1) * sin_lo + pltpu.roll(x, q, 1) * sin_hi


def _diff_kernel(q_ref, k_ref, v_ref, cos_ref, slo_ref, shi_ref, lam_ref, w_ref, o_ref,
                 kb_ref, vb_ref, *, lambda_init):
    kb_ref[...] = _axial_rope_tile(k_ref[0], cos_ref[...], slo_ref[...], shi_ref[...]).astype(BF16)
    ones_col = (lax.broadcasted_iota(jnp.int32, (LT, DIFF_V_DIM), 1) == 0).astype(BF16)
    vb_ref[...] = jnp.concatenate([v_ref[0].astype(BF16), ones_col], axis=1)

    lam = lam_ref[...]
    lam_full = (jnp.exp(jnp.sum(lam[0:1] * lam[1:2], axis=-1, keepdims=True))
                - jnp.exp(jnp.sum(lam[2:3] * lam[3:4], axis=-1, keepdims=True)) + lambda_init)
    out_scale = w_ref[...] * (1.0 - lambda_init)

    def attend_tile(r0, n_rows, n_keys, n_split):
        rows = pl.ds(r0, n_rows)
        q = q_ref[0, rows, :] * (DIFF_QK_DIM ** -0.5 * math.log2(math.e))
        qr = _axial_rope_tile(q, cos_ref[rows, :], slo_ref[rows, :], shi_ref[rows, :])
        first_map = lax.broadcasted_iota(jnp.int32, (n_rows, DIFF_V_DIM), 1) < DIFF_QK_DIM
        q_maps = (jnp.where(first_map, qr, 0.0).astype(BF16), jnp.where(first_map, 0.0, qr).astype(BF16))
        step = n_rows // n_split

        def attend_rows(j):
            r = slice(j * step, (j + 1) * step)
            qs = jnp.concatenate([q_maps[0][r], q_maps[1][r]], axis=0)
            s = lax.dot_general(qs, kb_ref[:n_keys, :], (((1,), (1,)), ((), ())),
                                preferred_element_type=F32)
            yield
            e = jnp.exp2(s - jnp.max(s, axis=-1, keepdims=True)).astype(BF16)
            ob = _dot(e, vb_ref[:n_keys, :])
            yield
            ob = ob[:, :DIFF_V_DIM] / ob[:, DIFF_V_DIM:DIFF_V_DIM + 1]
            o = ob[:step] - lam_full * ob[step:]
            y = o * lax.rsqrt(jnp.mean(o * o, axis=-1, keepdims=True) + EPS) * out_scale
            o_ref[0, pl.ds(r0 + j * step, step), :] = y.astype(BF16)

        _run_in_lockstep([attend_rows(j) for j in range(n_split)])

    attend_tile(0, CTX_LEN, CTX_LEN, 1)

    def latent_tile(i, carry):
        attend_tile(pl.multiple_of(CTX_LEN + i * DIFF_LAT_ROWS, TQ), DIFF_LAT_ROWS, LT, DIFF_Q_SPLIT)
        return carry

    lax.fori_loop(0, SEQ // DIFF_LAT_ROWS, latent_tile, 0)


def _diff_rope_tables():
    rows = SEQ // GRID_W
    r, cidx = jnp.meshgrid(jnp.arange(rows), jnp.arange(GRID_W), indexing='ij')
    axis_dim = DIFF_QK_DIM // 2
    inv = ROPE_BASE ** (-jnp.arange(0, axis_dim, 2, dtype=F32) / axis_dim)
    ang_r = r.reshape(-1).astype(F32)[:, None] * inv
    ang_c = cidx.reshape(-1).astype(F32)[:, None] * inv
    zero = jnp.zeros_like(ang_r)

    def table(fn_lo, fn_hi):
        one = jnp.concatenate([fn_lo(ang_r), fn_hi(ang_r), fn_lo(ang_c), fn_hi(ang_c)], axis=-1)
        return jnp.concatenate([one, one], axis=-1)

    cos = table(jnp.cos, jnp.cos)
    sin_lo = table(lambda a: -jnp.sin(a), lambda a: zero)
    sin_hi = table(lambda a: zero, jnp.sin)
    ident = [jnp.ones((CTX_LEN, DIFF_V_DIM), F32), jnp.zeros((CTX_LEN, DIFF_V_DIM), F32),
             jnp.zeros((CTX_LEN, DIFF_V_DIM), F32)]
    return [jnp.concatenate([c, t], axis=0) for c, t in zip(ident, (cos, sin_lo, sin_hi))]


def _diff_mixer(p, lam, subln_w, layer_idx):
    bsz = p.shape[0]
    lambda_init = 0.8 - 0.6 * math.exp(-0.3 * layer_idx)
    cos, sin_lo, sin_hi = _diff_rope_tables()
    q0 = 4 * DN_WIDTH // DIFF_V_DIM
    k0 = q0 + DIFF_HEADS
    v0 = k0 + DIFF_HEADS
    tab = pl.BlockSpec((LT, DIFF_V_DIM), lambda b, h: (0, 0))
    seq = lambda off: pl.BlockSpec((1, LT, DIFF_V_DIM), lambda b, h: (b, 0, off + h))
    return pl.pallas_call(
        functools.partial(_diff_kernel, lambda_init=lambda_init),
        grid=(bsz, DIFF_HEADS),
        in_specs=[seq(q0), seq(k0), seq(v0), tab, tab, tab,
                  pl.BlockSpec((4, DIFF_QK_DIM), lambda b, h: (0, 0)),
                  pl.BlockSpec((1, DIFF_V_DIM), lambda b, h: (0, 0))],
        out_specs=pl.BlockSpec((1, LT, DIFF_V_DIM), lambda b, h: (b, 0, h)),
        out_shape=jax.ShapeDtypeStruct((bsz, LT, DIFF_WIDTH), BF16),
        scratch_shapes=[pltpu.VMEM((LT, DIFF_V_DIM), BF16), pltpu.VMEM((LT, 2 * DIFF_V_DIM), BF16)],
        compiler_params=_cparams(("parallel", "parallel")),
        name="diff_attention",
    )(p, p, p, cos, sin_lo, sin_hi, lam, subln_w.reshape(1, DIFF_V_DIM))


DC = DN_CHUNK
DN_NCHUNK = LT // DC
DN_CTX_CHUNKS = CTX_LEN // DC


DN_PREP_GROUP = 9


def _split2(x):
    hi = x.astype(BF16)
    return hi, (x - hi.astype(F32)).astype(BF16)


def _stack3(a_bf16):
    return jnp.concatenate([a_bf16, a_bf16, a_bf16], axis=1)


def _mm_exact_lhs(a3_bf16, x):
    h1 = x.astype(BF16)
    r1 = x - h1.astype(F32)
    h2 = r1.astype(BF16)
    h3 = (r1 - h2.astype(F32)).astype(BF16)
    return _dot(a3_bf16, jnp.concatenate([h1, h2, h3], axis=0))


def _softplus(x):
    return jnp.maximum(x, 0.0) + jnp.log1p(jnp.exp(-jnp.abs(x)))


def _dn_kernel(q_ref, k_ref, v_ref, z_ref, gt_ref, cq_ref, ck_ref, cv_ref, prm_ref, nw_ref, o_ref,
               qn_ref, kn_ref, vn_ref, la_ref, be_ref, wq_ref, u_ref, qk_ref, b_ref, p_ref, e_ref,
               sst_ref, s_ref):
    h = pl.program_id(1)
    tpos = lax.broadcasted_iota(jnp.int32, (LT, 1), 0)
    has_prev = jnp.logical_and(tpos != 0, tpos != CTX_LEN)
    has_next = jnp.logical_and(tpos != CTX_LEN - 1, tpos != LT - 1)

    def conv_silu(x_ref, c_ref):
        x = x_ref[0]
        w = c_ref[...]
        prev = jnp.where(has_prev, pltpu.roll(x, 1, 0), 0.0)
        nxt = jnp.where(has_next, pltpu.roll(x, LT - 1, 0), 0.0)
        return _silu(prev * w[0:1] + x * w[1:2] + nxt * w[2:3])

    def l2n(t):
        return t * lax.rsqrt(jnp.sum(t * t, axis=-1, keepdims=True) + EPS)

    qn_ref[...] = l2n(conv_silu(q_ref, cq_ref)) * (DN_HEAD_DIM ** -0.5)
    kn_ref[...] = l2n(conv_silu(k_ref, ck_ref))
    vn_ref[...] = conv_silu(v_ref, cv_ref)

    gates = gt_ref[0]
    glane = lax.broadcasted_iota(jnp.int32, gates.shape, 1)
    log_a_all = -jnp.exp(prm_ref[0:1, :]) * _softplus(gates + prm_ref[1:2, :])
    beta_all = 1.0 / (1.0 + jnp.exp(-gates))
    for d in range(2):
        log_a = jnp.sum(jnp.where(glane == d * DN_HEADS + h, log_a_all, 0.0), axis=-1, keepdims=True)
        beta = jnp.sum(jnp.where(glane == (2 + d) * DN_HEADS + h, beta_all, 0.0), axis=-1, keepdims=True)
        la_ref[d] = jnp.broadcast_to(log_a, (LT, DN_HEAD_DIM))
        be_ref[d] = jnp.broadcast_to(beta, (LT, DN_HEAD_DIM))

    row = lax.broadcasted_iota(jnp.int32, (DC, 2 * DC), 0)
    lane = lax.broadcasted_iota(jnp.int32, (DC, 2 * DC), 1)
    col = lane & (DC - 1)
    isf = lane < DC
    eye2 = row == col
    isb = jnp.logical_not(isf)
    incl = jnp.logical_or(jnp.logical_and(isf, row >= col), jnp.logical_and(isb, row <= col))
    strict = jnp.logical_or(jnp.logical_and(isf, row > col), jnp.logical_and(isb, row < col))
    r64 = lax.broadcasted_iota(jnp.int32, (DC, DC), 0)
    c64 = lax.broadcasted_iota(jnp.int32, (DC, DC), 1)
    cum_lhs = _stack3(jnp.concatenate([(r64 >= c64).astype(BF16), jnp.ones((DC, DC), BF16)], axis=0))
    ones_lhs = _stack3(jnp.ones((DC, DC), BF16))
    eye_f = eye2.astype(F32)
    same16 = (row >> 4) == (col >> 4)
    in32_not16 = jnp.logical_and((row >> 5) == (col >> 5), jnp.logical_not(same16))
    not32 = (row >> 5) != (col >> 5)
    nt_dims = (((1,), (1,)), ((), ()))

    def block_diag(x2):
        return jnp.concatenate([jnp.where(isf[:x2.shape[0]], x2, 0.0), jnp.where(isb[:x2.shape[0]], x2, 0.0)],
                               axis=0)

    def mm_pair(x2, y2):
        xh, xl = _split2(x2)
        yh, yl = _split2(block_diag(y2))
        rhs = jnp.concatenate([jnp.concatenate([yh, yl], axis=1),
                               jnp.concatenate([yh, jnp.zeros_like(yl)], axis=1)], axis=0)
        out = _dot(jnp.concatenate([xh, xl], axis=1), rhs)
        return out[:, :2 * DC] + out[:, 2 * DC:]

    def prep_chunk(n):
        r0 = pl.multiple_of(n * DC, DC)
        rows = pl.ds(r0, DC)
        qn, kn, vn = qn_ref[rows, :], kn_ref[rows, :], vn_ref[rows, :]
        la_f, la_b = la_ref[0, rows, :], la_ref[1, rows, :]
        be_f, be_b = be_ref[0, rows, :], be_ref[1, rows, :]
        pt_f = _mm_exact_lhs(cum_lhs, la_f)
        pt_b = _mm_exact_lhs(cum_lhs, la_b)
        kb16 = kn.astype(BF16)
        kcat = jnp.concatenate([kb16, kb16], axis=0)
        kk2 = lax.dot_general(kb16, kcat, nt_dims, preferred_element_type=F32)
        qk2 = lax.dot_general(qn.astype(BF16), kcat, nt_dims, preferred_element_type=F32)
        yield
        g_f, tot_f = pt_f[:DC], pt_f[DC:]
        tot_b = pt_b[DC:]
        g_b = tot_b - pt_b[:DC] + la_b
        gg = jnp.where(isf, g_f, g_b)
        g_row = _mm_exact_lhs(ones_lhs, jnp.where(eye2, gg, 0.0))
        yield
        gam = jnp.where(incl, jnp.exp(gg - g_row), 0.0)
        a2 = jnp.where(strict, kk2 * jnp.where(isf, be_f, be_b) * gam, 0.0)
        dmat = jnp.where(same16, a2, 0.0)
        d2 = mm_pair(dmat, dmat)
        yield
        d4 = mm_pair(d2, d2)
        inv = mm_pair(eye_f - dmat, eye_f + d2)
        yield
        d8 = mm_pair(d4, d4)
        inv = mm_pair(inv, eye_f + d4)
        yield
        inv = mm_pair(inv, eye_f + d8)
        yield
        t = mm_pair(inv, jnp.where(in32_not16, a2, 0.0))
        yield
        inv = inv - mm_pair(t, inv)
        yield
        t = mm_pair(inv, jnp.where(not32, a2, 0.0))
        yield
        inv = inv - mm_pair(t, inv)
        yield
        eg_f, eg_b = jnp.exp(g_f), jnp.exp(g_b)
        zeros_rhs = jnp.zeros((DC, 2 * DN_HEAD_DIM), BF16)
        rhs_f = jnp.concatenate([vn * be_f, kn * be_f * eg_f], axis=1).astype(BF16)
        rhs_b = jnp.concatenate([vn * be_b, kn * be_b * eg_b], axis=1).astype(BF16)
        rhs_bd = jnp.concatenate([jnp.concatenate([rhs_f, zeros_rhs], axis=1),
                                  jnp.concatenate([zeros_rhs, rhs_b], axis=1)], axis=0)
        ih, il = _split2(inv)
        uw = _dot(jnp.concatenate([ih, il], axis=1), jnp.concatenate([rhs_bd, rhs_bd], axis=0))
        uw_f, uw_b = uw[:, :2 * DN_HEAD_DIM], uw[:, 2 * DN_HEAD_DIM:]
        yield
        kdt_f = jnp.transpose(kn * jnp.exp(tot_f - g_f)).astype(BF16)
        kdt_b = jnp.transpose(kn * jnp.exp(tot_b - g_b)).astype(BF16)
        bp_f = _dot(kdt_f, uw_f.astype(BF16))
        bp_b = _dot(kdt_b, uw_b.astype(BF16))
        yield
        u_ref[n, 0] = uw_f[:, :DN_HEAD_DIM]
        u_ref[n, 1] = uw_b[:, :DN_HEAD_DIM]
        wq_ref[n, 0] = jnp.concatenate([uw_f[:, DN_HEAD_DIM:], qn * eg_f], axis=0).astype(BF16)
        wq_ref[n, 1] = jnp.concatenate([uw_b[:, DN_HEAD_DIM:], qn * eg_b], axis=0).astype(BF16)
        qk_ref[n] = (qk2 * gam).astype(BF16)
        b_ref[n, 0] = bp_f[:, :DN_HEAD_DIM]
        b_ref[n, 1] = bp_b[:, :DN_HEAD_DIM]
        p_ref[n, 0] = bp_f[:, DN_HEAD_DIM:].astype(BF16)
        p_ref[n, 1] = bp_b[:, DN_HEAD_DIM:].astype(BF16)
        e_ref[n, 0] = jnp.exp(tot_f[0:8])
        e_ref[n, 1] = jnp.exp(tot_b[0:8])

    def prep(i, carry):
        _run_in_lockstep([prep_chunk(i * DN_PREP_GROUP + j) for j in range(DN_PREP_GROUP)])
        return carry

    lax.fori_loop(0, DN_NCHUNK // DN_PREP_GROUP, prep, 0)

    s_ref[...] = jnp.zeros(s_ref.shape, F32)

    def scan_dir(d, n):
        state = s_ref[d]
        s16 = state.astype(BF16)
        sst_ref[n, d] = s16
        ps = _dot(p_ref[n, d], s16)
        yield
        s_ref[d] = state * e_ref[n, d][0:1, :] - ps + b_ref[n, d]

    def scan_step(s, carry):
        nb = jnp.where(s < DN_CTX_CHUNKS, DN_CTX_CHUNKS - 1 - s, DN_NCHUNK - 1 + DN_CTX_CHUNKS - s)
        _run_in_lockstep([scan_dir(0, s), scan_dir(1, nb)])
        return carry

    lax.fori_loop(0, DN_NCHUNK, scan_step, 0)

    def finish_chunk(n):
        rows = pl.ds(pl.multiple_of(n * DC, DC), DC)
        sw_f = _dot(wq_ref[n, 0], sst_ref[n, 0])
        sw_b = _dot(wq_ref[n, 1], sst_ref[n, 1])
        yield
        vcat = jnp.concatenate([u_ref[n, 0] - sw_f[:DC], u_ref[n, 1] - sw_b[:DC]], axis=0).astype(BF16)
        intra = _dot(qk_ref[n], vcat)
        yield
        o = sw_f[DC:] + sw_b[DC:] + intra
        y = o * lax.rsqrt(jnp.mean(o * o, axis=-1, keepdims=True) + EPS) * nw_ref[...]
        o_ref[0, rows, :] = (y * _silu(z_ref[0, rows, :])).astype(BF16)

    def finish(i, carry):
        _run_in_lockstep([finish_chunk(i * DN_PREP_GROUP + j) for j in range(DN_PREP_GROUP)])
        return carry

    lax.fori_loop(0, DN_NCHUNK // DN_PREP_GROUP, finish, 0)


def _dn_mixer(p, conv_w, a_log, dt_bias, norm_w):
    bsz = p.shape[0]
    hd = DN_HEAD_DIM
    prm = jnp.pad(jnp.stack([a_log.reshape(-1), dt_bias.reshape(-1)]), ((0, 0), (0, hd - 2 * DN_HEADS)))
    gate_blk = AB_MAIN_COLS // hd
    seq = lambda off: pl.BlockSpec((1, LT, hd), lambda b, h: (b, 0, off + h))
    cw = lambda off: pl.BlockSpec((CONV_WIDTH, hd), lambda b, h: (0, off + h))
    return pl.pallas_call(
        _dn_kernel,
        grid=(bsz, DN_HEADS),
        in_specs=[seq(0), seq(DN_HEADS), seq(2 * DN_HEADS), seq(3 * DN_HEADS),
                  pl.BlockSpec((1, LT, hd), lambda b, h: (b, 0, gate_blk)),
                  cw(0), cw(DN_HEADS), cw(2 * DN_HEADS),
                  pl.BlockSpec((2, hd), lambda b, h: (0, 0)),
                  pl.BlockSpec((1, hd), lambda b, h: (0, 0))],
        out_specs=pl.BlockSpec((1, LT, hd), lambda b, h: (b, 0, h)),
        out_shape=jax.ShapeDtypeStruct((bsz, LT, DN_WIDTH), BF16),
        scratch_shapes=[pltpu.VMEM((LT, hd), F32), pltpu.VMEM((LT, hd), F32), pltpu.VMEM((LT, hd), F32),
                        pltpu.VMEM((2, LT, hd), F32), pltpu.VMEM((2, LT, hd), F32),
                        pltpu.VMEM((DN_NCHUNK, 2, 2 * DC, hd), BF16),
                        pltpu.VMEM((DN_NCHUNK, 2, DC, hd), F32),
                        pltpu.VMEM((DN_NCHUNK, DC, 2 * DC), BF16),
                        pltpu.VMEM((DN_NCHUNK, 2, hd, hd), F32),
                        pltpu.VMEM((DN_NCHUNK, 2, hd, hd), BF16),
                        pltpu.VMEM((DN_NCHUNK, 2, 8, hd), F32),
                        pltpu.VMEM((DN_NCHUNK, 2, hd, hd), BF16),
                        pltpu.VMEM((2, hd, hd), F32)],
        compiler_params=_cparams(("parallel", "parallel")),
        name="gated_deltanet",
    )(p, p, p, p, p, conv_w, conv_w, conv_w, prm, norm_w.reshape(1, hd))


def _split_ab_weight(w):
    o_ab = 4 * DN_WIDTH
    n_ab = 4 * DN_HEADS
    main = jnp.concatenate([w[..., :o_ab], w[..., o_ab + n_ab:]], axis=-1)
    pad = [(0, 0)] * (w.ndim - 1) + [(0, AB_GATE_COLS - n_ab)]
    return jnp.concatenate([main, jnp.pad(w[..., o_ab:o_ab + n_ab], pad)], axis=-1)


def kernel(x, c, ctx, c_ctx, ada_w, ada_b, norm_w, final_norm_w, ffn_w_in, ffn_w_out, ab_w_in,
           ab_conv_w, dn_A_log, dn_dt_bias, dn_norm_w, diff_lambda, diff_subln_w, ab_w_out,
           ret_w_in, ret_decay_logit, ret_w_out):
    B = x.shape[0]
    xs = jnp.concatenate([ctx, x], axis=1).reshape(B * LT, D_MODEL)
    cvec = jnp.concatenate([c, c_ctx[None], jnp.zeros((MOD_ROWS - B - 1, D_MODEL), F32)], axis=0)
    mods_all = _ada_mods(cvec, ada_w, ada_b)
    mods_all = mods_all.reshape(DEPTH, MOD_ROWS, N_MOD, D_MODEL).transpose(0, 2, 1, 3)
    norms = norm_w.reshape(DEPTH, 3, 1, D_MODEL)
    ffn_w_in_b = ffn_w_in.astype(BF16)
    ffn_w_out_b = ffn_w_out.astype(BF16)
    ab_w_in_b = _split_ab_weight(ab_w_in).astype(BF16)
    ab_w_out_b = ab_w_out.astype(BF16)
    ret_w_in_b = ret_w_in.astype(BF16)
    ret_w_out_b = ret_w_out.astype(BF16)
    for l in range(DEPTH):
        last = l == DEPTH - 1
        xs = _ffn(xs, norms, mods_all, l, 0, ffn_w_in_b, ffn_w_out_b, 0)
        i = l // 2
        if l % 2 == 0:
            p = _proj(xs, norms, mods_all, l, ab_w_in_b, i, AB_PROJ_TN).reshape(B, LT, -1)
            y_dn = _dn_mixer(p, ab_conv_w[i], dn_A_log[i], dn_dt_bias[i], dn_norm_w[i])
            y_df = _diff_mixer(p, diff_lambda[i], diff_subln_w[i], l)
            xs = _out_proj(xs, mods_all, l, [y_dn.reshape(B * LT, DN_WIDTH), y_df.reshape(B * LT, DIFF_WIDTH)],
                           ab_w_out_b, i)
        else:
            p = _proj(xs, norms, mods_all, l, ret_w_in_b, i, RET_PROJ_TN).reshape(B, LT, -1)
            y = _ret_mixer(p, ret_decay_logit[i])
            xs = _out_proj(xs, mods_all, l, [y.reshape(B * LT, RET_V_WIDTH)], ret_w_out_b, i)
        xs = _ffn(xs, norms, mods_all, l, 2, ffn_w_in_b, ffn_w_out_b, 1,
                  final_w=final_norm_w if last else None)
    return xs.reshape(B, LT, D_MODEL)[:, CTX_LEN:]
```
